```python
import math
import jax, jax.numpy as jnp
from jax import lax
import numpy as np

D_MODEL = 1024
BATCH = 8
SEQ = 2048
DEPTH = 1
DEC_BATCH = 4
DEC_SEQ = 4096
PAST_LEN = 128

A_HEADS = 4
A_HEAD_DIM = 128
A_WIDTH = A_HEADS * A_HEAD_DIM
A_CHUNK = 64
S5_WIDTH = 512
S5_GROUP = 16
S5_GROUPS = S5_WIDTH // S5_GROUP
S5_STATE = 64
S5_DT_MIN = 0.001
S5_DT_MAX = 0.1
N_EXPERTS = 16
EXPERT_FF = 2048
EC_CAPACITY_FACTOR = 2
N_BRANCHES = 2
RMS_EPS = 1e-6

IN_COLS = 5 * A_WIDTH + S5_WIDTH + N_BRANCHES * D_MODEL
IN_SPLITS = (A_WIDTH, 2 * A_WIDTH, 3 * A_WIDTH, 4 * A_WIDTH, 5 * A_WIDTH,
             5 * A_WIDTH + S5_WIDTH, 5 * A_WIDTH + S5_WIDTH + D_MODEL)

kernel_name = "hgrn2_s5_gated_parallel_ec_moe_encoder"


def rms_norm(x, g):
    xf = x.astype(jnp.float32)
    y = xf * lax.rsqrt(jnp.mean(xf * xf, axis=-1, keepdims=True) + RMS_EPS)
    return (y * g.astype(jnp.float32)).astype(x.dtype)


def hgrn_lower_bound(gamma, layer):
    p = jax.nn.softmax(gamma.astype(jnp.float32), axis=0)
    return jnp.cumsum(p, axis=0)[layer]


def hgrn2_scan(q, logf, k, v):
    b_, L, H, _ = q.shape
    C = A_CHUNK
    N = L // C
    rs = lambda t: t.reshape(b_, N, C, H, t.shape[-1])
    q, logf, k, v = rs(q), rs(logf), rs(k), rs(v)
    bcum = jnp.cumsum(logf, axis=2)
    b_mid = bcum[:, :, C // 2 - 1:C // 2]
    q_in = q * jnp.exp(bcum - b_mid)
    k_in = k * jnp.exp(b_mid - bcum)
    scores = jnp.einsum('bnthk,bnshk->bnhts', q_in, k_in)
    mask = jnp.tril(jnp.ones((C, C), dtype=bool))
    scores = jnp.where(mask, scores, 0.0)
    o_intra = jnp.einsum('bnhts,bnshv->bnthv', scores, v)
    b_last = bcum[:, :, -1]
    kv = jnp.einsum('bnshk,bnshv->bnhkv', k * jnp.exp(b_last[:, :, None] - bcum), v)
    decay = jnp.exp(b_last)

    def step(S, inp):
        d, u = inp
        return d[..., None] * S + u, S

    S0 = jnp.zeros((b_, H, q.shape[-1], v.shape[-1]), jnp.float32)
    _, S_prev = lax.scan(step, S0, (jnp.moveaxis(decay, 1, 0), jnp.moveaxis(kv, 1, 0)))
    S_prev = jnp.moveaxis(S_prev, 0, 1)
    o_inter = jnp.einsum('bnthk,bnhkv->bnthv', q * jnp.exp(bcum), S_prev)
    return (o_intra + o_inter).reshape(b_, L, H, v.shape[-1])


def s5_direction(u, a_re, a_im, log_dt, b_re, b_im, c_re, c_im):
    f32 = jnp.float32
    a_re, a_im = a_re.astype(f32), a_im.astype(f32)
    b_re, b_im = b_re.astype(f32), b_im.astype(f32)
    dt = jnp.exp(log_dt.astype(f32))[:, None]
    mag = jnp.exp(dt * a_re)
    ab_re = mag * jnp.cos(dt * a_im)
    ab_im = mag * jnp.sin(dt * a_im)
    nr, ni = ab_re - 1.0, ab_im
    den = a_re * a_re + a_im * a_im
    f_re = (nr * a_re + ni * a_im) / den
    f_im = (ni * a_re - nr * a_im) / den
    bb_re = f_re[..., None] * b_re - f_im[..., None] * b_im
    bb_im = f_re[..., None] * b_im + f_im[..., None] * b_re
    bu_re = jnp.einsum('blgc,gpc->blgp', u, bb_re)
    bu_im = jnp.einsum('blgc,gpc->blgp', u, bb_im)
    lam_re = jnp.broadcast_to(ab_re, bu_re.shape)
    lam_im = jnp.broadcast_to(ab_im, bu_re.shape)

    def combine(e1, e2):
        a1r, a1i, b1r, b1i = e1
        a2r, a2i, b2r, b2i = e2
        return (a2r * a1r - a2i * a1i, a2r * a1i + a2i * a1r,
                a2r * b1r - a2i * b1i + b2r, a2r * b1i + a2i * b1r + b2i)

    _, _, x_re, x_im = lax.associative_scan(combine, (lam_re, lam_im, bu_re, bu_im), axis=1)
    return (jnp.einsum('blgp,gcp->blgc', x_re, c_re.astype(f32))
            - jnp.einsum('blgp,gcp->blgc', x_im, c_im.astype(f32)))


def expert_choice_ffn(h, w_router, w_gate, w_up, w_down):
    T = h.shape[0]
    cap = max(1, EC_CAPACITY_FACTOR * T // N_EXPERTS)
    aff = jax.nn.softmax(jnp.einsum('td,de->te', h, w_router).astype(jnp.float32), axis=-1)
    g, idx = lax.top_k(aff.T, cap)
    xe = h[idx]
    a = jnp.einsum('ecd,edf->ecf', xe, w_gate)
    u = jnp.einsum('ecd,edf->ecf', xe, w_up)
    ye = jnp.einsum('ecf,efd->ecd', jax.nn.silu(a) * u, w_down)
    ye = (ye * g[..., None]).astype(h.dtype)
    return jnp.zeros_like(h).at[idx.reshape(-1)].add(ye.reshape(-1, h.shape[-1]))


def encoder_layer(x, l, norm_mix, w_in, hgrn_gamma, hgrn_norm,
                  s5_a_re, s5_a_im, s5_log_dt, s5_b_re, s5_b_im, s5_c_re, s5_c_im,
                  s5_d, s5_w_glu, s5_b_glu, w_branch_a, w_branch_b, w_out,
                  norm_ffn, w_router, w_exp_gate, w_exp_up, w_exp_down):
    f32 = jnp.float32
    Bsz, L, _ = x.shape
    h = rms_norm(x, norm_mix[l])
    proj = jnp.einsum('bld,dc->blc', h, w_in[l])
    q, f_fw, f_bw, i_in, g_out, u, gate_a, gate_b = jnp.split(proj, IN_SPLITS, axis=-1)

    heads = lambda t: t.astype(f32).reshape(Bsz, L, A_HEADS, A_HEAD_DIM)
    lb = hgrn_lower_bound(hgrn_gamma, l)
    qh, vh = heads(q), heads(i_in)
    fw = lb[0] + (1.0 - lb[0]) * jax.nn.sigmoid(f_fw.astype(f32))
    bw = lb[1] + (1.0 - lb[1]) * jax.nn.sigmoid(f_bw.astype(f32))
    o_fw = hgrn2_scan(qh, heads(jnp.log(fw)), heads(1.0 - fw), vh)
    o_bw = hgrn2_scan(qh[:, ::-1], heads(jnp.log(bw))[:, ::-1],
                      heads(1.0 - bw)[:, ::-1], vh[:, ::-1])[:, ::-1]
    o = o_fw + o_bw
    o = o * lax.rsqrt(jnp.mean(o * o, axis=-1, keepdims=True) + RMS_EPS)
    o = o * hgrn_norm[l].astype(f32).reshape(A_HEADS, A_HEAD_DIM)
    h_a = o.reshape(Bsz, L, A_WIDTH) * jax.nn.silu(g_out.astype(f32))

    ug = u.astype(f32).reshape(Bsz, L, S5_GROUPS, S5_GROUP)
    y_f = s5_direction(ug, s5_a_re[l, 0], s5_a_im[l, 0], s5_log_dt[l, 0],
                       s5_b_re[l], s5_b_im[l], s5_c_re[l], s5_c_im[l])
    y_b = s5_direction(ug[:, ::-1], s5_a_re[l, 1], s5_a_im[l, 1], s5_log_dt[l, 1],
                       s5_b_re[l], s5_b_im[l], s5_c_re[l], s5_c_im[l])[:, ::-1]
    ys = (y_f + y_b).reshape(Bsz, L, S5_WIDTH) + s5_d[l].astype(f32) * u.astype(f32)
    ys = jax.nn.gelu(ys)
    h_b = ys * jax.nn.sigmoid(jnp.einsum('blc,ce->ble', ys, s5_w_glu[l].astype(f32))
                              + s5_b_glu[l].astype(f32))

    y_a = jnp.einsum('blc,cd->bld', h_a.astype(x.dtype), w_branch_a[l])
    y_b2 = jnp.einsum('blc,cd->bld', h_b.astype(x.dtype), w_branch_b[l])
    merged = jax.nn.sigmoid(gate_a) * y_a + jax.nn.sigmoid(gate_b) * y_b2
    x = x + jnp.einsum('bld,de->ble', merged, w_out[l]).astype(x.dtype)

    h2 = rms_norm(x, norm_ffn[l]).reshape(Bsz * L, D_MODEL)
    m = expert_choice_ffn(h2, w_router[l], w_exp_gate[l], w_exp_up[l], w_exp_down[l])
    return x + m.reshape(Bsz, L, D_MODEL).astype(x.dtype)


def setup_inputs(seed: int = 0) -> dict:
    key = jax.random.key(seed)
    ks = jax.random.split(key, 32)
    nrm = lambda k, shape, scale: (jax.random.normal(k, shape, jnp.float32) * scale)
    n_idx = jnp.arange(S5_STATE, dtype=jnp.float32)
    a_re = -0.5 + 0.01 * jax.random.normal(ks[5], (DEPTH, 2, S5_GROUPS, S5_STATE), jnp.float32)
    a_im = math.pi * n_idx + 0.01 * jax.random.normal(ks[6], (DEPTH, 2, S5_GROUPS, S5_STATE), jnp.float32)
    log_dt = jax.random.uniform(ks[7], (DEPTH, 2, S5_GROUPS), jnp.float32,
                                math.log(S5_DT_MIN), math.log(S5_DT_MAX))
    return {
        'x_prompt': nrm(ks[0], (BATCH, SEQ, D_MODEL), 1.0),
        'x_sample': nrm(ks[1], (DEC_BATCH, DEC_SEQ, D_MODEL), 1.0),
        'norm_mix': 1.0 + nrm(ks[2], (DEPTH, D_MODEL), 0.02),
        'w_in': nrm(ks[3], (DEPTH, D_MODEL, IN_COLS), D_MODEL ** -0.5),
        'hgrn_gamma': nrm(ks[4], (DEPTH + 1, 2, A_WIDTH), 0.1),
        'hgrn_norm': 1.0 + nrm(ks[8], (DEPTH, A_WIDTH), 0.02),
        's5_a_re': a_re,
        's5_a_im': a_im,
        's5_log_dt': log_dt,
        's5_b_re': nrm(ks[9], (DEPTH, S5_GROUPS, S5_STATE, S5_GROUP), (2 * S5_GROUP) ** -0.5),
        's5_b_im': nrm(ks[10], (DEPTH, S5_GROUPS, S5_STATE, S5_GROUP), (2 * S5_GROUP) ** -0.5),
        's5_c_re': nrm(ks[11], (DEPTH, S5_GROUPS, S5_GROUP, S5_STATE), (2 * S5_STATE) ** -0.5),
        's5_c_im': nrm(ks[12], (DEPTH, S5_GROUPS, S5_GROUP, S5_STATE), (2 * S5_STATE) ** -0.5),
        's5_d': nrm(ks[13], (DEPTH, S5_WIDTH), 0.5),
        's5_w_glu': nrm(ks[14], (DEPTH, S5_WIDTH, S5_WIDTH), S5_WIDTH ** -0.5),
        's5_b_glu': nrm(ks[15], (DEPTH, S5_WIDTH), 0.02),
        'w_branch_a': nrm(ks[16], (DEPTH, A_WIDTH, D_MODEL), A_WIDTH ** -0.5),
        'w_branch_b': nrm(ks[17], (DEPTH, S5_WIDTH, D_MODEL), S5_WIDTH ** -0.5),
        'w_out': nrm(ks[18], (DEPTH, D_MODEL, D_MODEL), D_MODEL ** -0.5),
        'norm_ffn': 1.0 + nrm(ks[19], (DEPTH, D_MODEL), 0.02),
        'w_router': nrm(ks[20], (DEPTH, D_MODEL, N_EXPERTS), D_MODEL ** -0.5),
        'w_exp_gate': nrm(ks[21], (DEPTH, N_EXPERTS, D_MODEL, EXPERT_FF), D_MODEL ** -0.5),
        'w_exp_up': nrm(ks[22], (DEPTH, N_EXPERTS, D_MODEL, EXPERT_FF), D_MODEL ** -0.5),
        'w_exp_down': nrm(ks[23], (DEPTH, N_EXPERTS, EXPERT_FF, D_MODEL), EXPERT_FF ** -0.5),
        'norm_final': 1.0 + nrm(ks[24], (D_MODEL,), 0.02),
    }


def _trunk(x, norm_mix, w_in, hgrn_gamma, hgrn_norm, s5_a_re, s5_a_im, s5_log_dt,
           s5_b_re, s5_b_im, s5_c_re, s5_c_im, s5_d, s5_w_glu, s5_b_glu,
           w_branch_a, w_branch_b, w_out, norm_ffn, w_router, w_exp_gate, w_exp_up,
           w_exp_down, norm_final):
    for l in range(DEPTH):
        x = encoder_layer(x, l, norm_mix, w_in, hgrn_gamma, hgrn_norm,
                          s5_a_re, s5_a_im, s5_log_dt, s5_b_re, s5_b_im, s5_c_re, s5_c_im,
                          s5_d, s5_w_glu, s5_b_glu, w_branch_a, w_branch_b, w_out,
                          norm_ffn, w_router, w_exp_gate, w_exp_up, w_exp_down)
    return rms_norm(x, norm_final)


def reference(x_prompt, x_sample, norm_mix, w_in, hgrn_gamma, hgrn_norm, s5_a_re, s5_a_im,
              s5_log_dt, s5_b_re, s5_b_im, s5_c_re, s5_c_im, s5_d, s5_w_glu, s5_b_glu,
              w_branch_a, w_branch_b, w_out, norm_ffn, w_router, w_exp_gate, w_exp_up,
              w_exp_down, norm_final):
    y_prompt = _trunk(x_prompt, norm_mix, w_in, hgrn_gamma, hgrn_norm, s5_a_re, s5_a_im,
                      s5_log_dt, s5_b_re, s5_b_im, s5_c_re, s5_c_im, s5_d, s5_w_glu, s5_b_glu,
                      w_branch_a, w_branch_b, w_out, norm_ffn, w_router, w_exp_gate, w_exp_up,
                      w_exp_down, norm_final)
    y_sample = _trunk(x_sample, norm_mix, w_in, hgrn_gamma, hgrn_norm, s5_a_re, s5_a_im,
                      s5_log_dt, s5_b_re, s5_b_im, s5_c_re, s5_c_im, s5_d, s5_w_glu, s5_b_glu,
                      w_branch_a, w_branch_b, w_out, norm_ffn, w_router, w_exp_gate, w_exp_up,
                      w_exp_down, norm_final)
    return (y_prompt, y_sample)
```

```python
import functools
import math

import jax
import jax.numpy as jnp
from jax import lax
from jax.experimental import pallas as pl
from jax.experimental.pallas import tpu as pltpu

F32 = jnp.float32
BF16 = jnp.bfloat16
I32 = jnp.int32

D_MODEL = 1024
A_HEADS = 4
A_HEAD_DIM = 128
A_WIDTH = A_HEADS * A_HEAD_DIM
A_CHUNK = 64
S5_WIDTH = 512
S5_GROUP = 16
S5_GROUPS = S5_WIDTH // S5_GROUP
S5_STATE = 64
N_EXPERTS = 16
EXPERT_FF = 2048
EC_CAPACITY_FACTOR = 2
RMS_EPS = 1e-6
IN_COLS = 5 * A_WIDTH + S5_WIDTH + 2 * D_MODEL

LANES = 128
ROW_TILE = 8
S5_CHUNK = 16
S5_COLS = S5_CHUNK * S5_GROUP
TOK_BLOCK = 256
WIN = 64
MIB = 2 ** 20


def _params(semantics, vmem_mib):
    return pltpu.CompilerParams(dimension_semantics=semantics, vmem_limit_bytes=vmem_mib * MIB)


def _nt_dot(a, b):
    return lax.dot_general(a, b, (((1,), (1,)), ((), ())), preferred_element_type=F32)


def _tn_dot(a, b):
    return lax.dot_general(a, b, (((0,), (0,)), ((), ())), preferred_element_type=F32)


def _inproj_body(x_ref, g_ref, w_ref, hg_ref, u_ref, gt_ref):
    x = x_ref[...]
    ms = jnp.mean(x * x, axis=-1, keepdims=True)
    h = ((x * lax.rsqrt(ms + RMS_EPS)) * g_ref[...]).astype(BF16)
    n_hg = 5 * A_WIDTH // 512
    for c in range(IN_COLS // 512):
        r = jnp.dot(h, w_ref[:, c * 512:(c + 1) * 512], preferred_element_type=F32).astype(BF16)
        if c < n_hg:
            hg_ref[:, c * 512:(c + 1) * 512] = r
        elif c == n_hg:
            u_ref[...] = r
        else:
            gt_ref[:, (c - n_hg - 1) * 512:(c - n_hg) * 512] = r


def _inproj(x, gain, w_bf16, tm=512):
    t = x.shape[0]
    return pl.pallas_call(
        _inproj_body,
        grid=(t // tm,),
        in_specs=[pl.BlockSpec((tm, D_MODEL), lambda i: (i, 0)),
                  pl.BlockSpec((1, D_MODEL), lambda i: (0, 0)),
                  pl.BlockSpec((D_MODEL, IN_COLS), lambda i: (0, 0))],
        out_specs=[pl.BlockSpec((tm, 5 * A_WIDTH), lambda i: (i, 0)),
                   pl.BlockSpec((tm, S5_WIDTH), lambda i: (i, 0)),
                   pl.BlockSpec((tm, 2 * D_MODEL), lambda i: (i, 0))],
        out_shape=[jax.ShapeDtypeStruct((t, 5 * A_WIDTH), BF16),
                   jax.ShapeDtypeStruct((t, S5_WIDTH), BF16),
                   jax.ShapeDtypeStruct((t, 2 * D_MODEL), BF16)],
        compiler_params=_params(("arbitrary",), 48),
        name="inproj",
    )(x, gain, w_bf16)


def _split3(x):
    hi = x.astype(BF16)
    r1 = x - hi.astype(F32)
    mid = r1.astype(BF16)
    lo = (r1 - mid.astype(F32)).astype(BF16)
    return hi, mid, lo


def _hgrn_chunk(q, fr, v, lb, s_ref, tri, mid, last):
    q = q.astype(F32)
    f = lb + (1.0 - lb) * jax.nn.sigmoid(fr.astype(F32))
    lf = jnp.log(f)
    kk = 1.0 - f
    tri_b = jnp.where(tri, 1.0, 0.0).astype(BF16)
    hi, md, lo = _split3(lf)
    bcum = (jnp.dot(tri_b, hi, preferred_element_type=F32)
            + jnp.dot(tri_b, md, preferred_element_type=F32)
            + jnp.dot(tri_b, lo, preferred_element_type=F32))
    bmid = bcum[mid:mid + 1, :]
    blast = bcum[last:last + 1, :]
    q_in = (q * jnp.exp(bcum - bmid)).astype(BF16)
    k_in = (kk * jnp.exp(bmid - bcum)).astype(BF16)
    qdec = (q * jnp.exp(bcum)).astype(BF16)
    kdec = (kk * jnp.exp(blast - bcum)).astype(BF16)
    decay = jnp.exp(blast)
    outs = []
    for h in range(A_HEADS):
        sl = slice(h * A_HEAD_DIM, (h + 1) * A_HEAD_DIM)
        s = jnp.where(tri, _nt_dot(q_in[:, sl], k_in[:, sl]), 0.0).astype(BF16)
        st = s_ref[h]
        o = jnp.dot(s, v[:, sl], preferred_element_type=F32) + _nt_dot(qdec[:, sl], st.astype(BF16))
        s_ref[h] = decay[:, sl] * st + _tn_dot(v[:, sl], kdec[:, sl])
        outs.append(o)
    return jnp.concatenate(outs, axis=1).astype(BF16)


def _hgrn_body(qf_ref, ff_ref, vf_ref, qb_ref, fb_ref, vb_ref, lb_ref, of_ref, ob_ref, sf_ref, sb_ref,
               *, n_chunks):
    @pl.when(pl.program_id(1) == 0)
    def _():
        sf_ref[...] = jnp.zeros_like(sf_ref)
        sb_ref[...] = jnp.zeros_like(sb_ref)

    c = A_CHUNK
    t_idx = lax.broadcasted_iota(I32, (c, c), 0)
    s_idx = lax.broadcasted_iota(I32, (c, c), 1)
    tri_f = s_idx <= t_idx
    tri_b = s_idx >= t_idx

    def step(j, carry):
        rf = pl.ds(pl.multiple_of(j * c, c), c)
        of_ref[0, rf, :] = _hgrn_chunk(qf_ref[0, rf, :], ff_ref[0, rf, :], vf_ref[0, rf, :],
                                       lb_ref[0:1, :], sf_ref, tri_f, c // 2 - 1, c - 1)
        rb = pl.ds(pl.multiple_of((n_chunks - 1 - j) * c, c), c)
        ob_ref[0, rb, :] = _hgrn_chunk(qb_ref[0, rb, :], fb_ref[0, rb, :], vb_ref[0, rb, :],
                                       lb_ref[1:2, :], sb_ref, tri_b, c // 2, 0)
        return carry

    lax.fori_loop(0, n_chunks, step, 0)


def _hgrn(hg, lb, lb_tokens=512):
    b, l, _ = hg.shape
    nb = l // lb_tokens
    blk = (1, lb_tokens, A_WIDTH)
    fwd = lambda col: pl.BlockSpec(blk, lambda bi, i: (bi, i, col))
    bwd = lambda col: pl.BlockSpec(blk, lambda bi, i: (bi, nb - 1 - i, col))
    return pl.pallas_call(
        functools.partial(_hgrn_body, n_chunks=lb_tokens // A_CHUNK),
        grid=(b, nb),
        in_specs=[fwd(0), fwd(1), fwd(3), bwd(0), bwd(2), bwd(3),
                  pl.BlockSpec((2, A_WIDTH), lambda bi, i: (0, 0))],
        out_specs=[pl.BlockSpec(blk, lambda bi, i: (bi, i, 0)),
                   pl.BlockSpec(blk, lambda bi, i: (bi, nb - 1 - i, 0))],
        out_shape=[jax.ShapeDtypeStruct((b, l, A_WIDTH), BF16)] * 2,
        scratch_shapes=[pltpu.VMEM((A_HEADS, A_HEAD_DIM, A_HEAD_DIM), F32)] * 2,
        compiler_params=_params(("arbitrary", "arbitrary"), 32),
        name="hgrn2",
    )(hg, hg, hg, hg, hg, hg, lb)


def _s5_direction_terms(a_re, a_im, log_dt, b_re, b_im):
    a_re, a_im = a_re.astype(F32), a_im.astype(F32)
    b_re, b_im = b_re.astype(F32), b_im.astype(F32)
    dt = jnp.exp(log_dt.astype(F32))[:, None]
    j = jnp.arange(S5_CHUNK + 1, dtype=F32)[:, None, None]
    mag = jnp.exp(j * (dt * a_re))
    pw_re = mag * jnp.cos(j * (dt * a_im))
    pw_im = mag * jnp.sin(j * (dt * a_im))
    nr, ni = pw_re[1] - 1.0, pw_im[1]
    den = a_re * a_re + a_im * a_im
    f_re = (nr * a_re + ni * a_im) / den
    f_im = (ni * a_re - nr * a_im) / den
    bb_re = f_re[..., None] * b_re - f_im[..., None] * b_im
    bb_im = f_re[..., None] * b_im + f_im[..., None] * b_re
    pb_re = pw_re[..., None] * bb_re - pw_im[..., None] * bb_im
    pb_im = pw_re[..., None] * bb_im + pw_im[..., None] * bb_re
    return pw_re, pw_im, pb_re, pb_im


def _s5_operators(a_re, a_im, log_dt, b_re, b_im, c_re, c_im, n_steps):
    hp = lax.Precision.HIGHEST
    c_re, c_im = c_re.astype(F32), c_im.astype(F32)
    g = S5_GROUPS
    terms = [_s5_direction_terms(a_re[d], a_im[d], log_dt[d], b_re, b_im) for d in range(2)]
    taps = []
    for pw_re, pw_im, pb_re, pb_im in terms:
        k = (jnp.einsum('gop,jgpi->jgoi', c_re, pb_re[:S5_CHUNK], precision=hp)
             - jnp.einsum('gop,jgpi->jgoi', c_im, pb_im[:S5_CHUNK], precision=hp))
        taps.append(k)
    sig = jnp.arange(S5_CHUNK)[:, None]
    tau = jnp.arange(S5_CHUNK)[None, :]
    d = tau - sig
    kf = jnp.where((d >= 0)[:, :, None, None, None], taps[0][jnp.clip(d, 0, S5_CHUNK - 1)], 0.0)
    kb = jnp.where((d <= 0)[:, :, None, None, None], taps[1][jnp.clip(-d, 0, S5_CHUNK - 1)], 0.0)
    w_intra = (kf + kb).transpose(2, 0, 4, 1, 3).reshape(g, S5_COLS, S5_COLS)

    def w_in(pb_re, pb_im, order):
        re = pb_re[order].transpose(1, 0, 3, 2)
        im = pb_im[order].transpose(1, 0, 3, 2)
        return jnp.concatenate([re, im], axis=-1).reshape(g, S5_COLS, 2 * S5_STATE)

    def w_out(pw_re, pw_im, order):
        pr, pi = pw_re[order], pw_im[order]
        m_re = c_re[None] * pr[:, :, None, :] - c_im[None] * pi[:, :, None, :]
        m_im = c_re[None] * pi[:, :, None, :] + c_im[None] * pr[:, :, None, :]
        re = m_re.transpose(1, 3, 0, 2).reshape(g, S5_STATE, S5_COLS)
        im = -m_im.transpose(1, 3, 0, 2).reshape(g, S5_STATE, S5_COLS)
        return jnp.concatenate([re, im], axis=1)

    asc = jnp.arange(S5_CHUNK)
    w_in_f = w_in(terms[0][2], terms[0][3], S5_CHUNK - 1 - asc)
    w_in_b = w_in(terms[1][2], terms[1][3], asc)
    w_out_f = w_out(terms[0][0], terms[0][1], asc + 1)
    w_out_b = w_out(terms[1][0], terms[1][1], S5_CHUNK - asc)
    w1 = jnp.concatenate([w_intra, w_in_f, w_in_b], axis=-1).astype(BF16)
    w2 = jnp.concatenate([w_out_f, w_out_b], axis=1).astype(BF16)

    ars, ais = [], []
    cur = [(t[0][S5_CHUNK], t[1][S5_CHUNK]) for t in terms]
    for _ in range(n_steps):
        ars.append(jnp.concatenate([cur[0][0], cur[0][0], cur[1][0], cur[1][0]], axis=-1))
        ais.append(jnp.concatenate([-cur[0][1], cur[0][1], -cur[1][1], cur[1][1]], axis=-1))
        cur = [(r * r - i * i, 2.0 * r * i) for r, i in cur]
    return w1, w2, jnp.stack(ars, axis=1), jnp.stack(ais, axis=1)


def _s5_body(u_ref, w1_ref, w2_ref, ar_ref, ai_ref, y_ref, *, n_seq, n_steps):
    u = u_ref[0]
    rows = u.shape[0]
    half = 2 * S5_STATE
    yv = jnp.dot(u, w1_ref[0], preferred_element_type=F32)
    n = lax.broadcasted_iota(I32, (rows, half), 0) & (n_seq - 1)
    zf = jnp.where(n >= 1, pltpu.roll(yv[:, S5_COLS:S5_COLS + half], 1, axis=0), 0.0)
    zb = jnp.where(n < n_seq - 1, pltpu.roll(yv[:, S5_COLS + half:], rows - 1, axis=0), 0.0)
    for k in range(n_steps):
        s = 1 << k
        sf = jnp.where(n >= s, pltpu.roll(zf, s, axis=0), 0.0)
        sb = jnp.where(n < n_seq - s, pltpu.roll(zb, rows - s, axis=0), 0.0)
        zf = zf + ar_ref[0, k:k + 1, :half] * sf + ai_ref[0, k:k + 1, :half] * pltpu.roll(sf, S5_STATE, axis=1)
        zb = zb + ar_ref[0, k:k + 1, half:] * sb + ai_ref[0, k:k + 1, half:] * pltpu.roll(sb, S5_STATE, axis=1)
    x = jnp.concatenate([zf, zb], axis=1).astype(BF16)
    y = yv[:, :S5_COLS] + jnp.dot(x, w2_ref[0], preferred_element_type=F32)
    y_ref[0] = y.astype(BF16)


def _s5(ug, w1, w2, ar, ai, n_seq):
    g, rows, _ = ug.shape
    n_steps = ar.shape[1]
    per_g = lambda shape: pl.BlockSpec((1,) + shape, lambda i: (i, 0, 0))
    return pl.pallas_call(
        functools.partial(_s5_body, n_seq=n_seq, n_steps=n_steps),
        grid=(g,),
        in_specs=[per_g((rows, S5_COLS)), per_g((S5_COLS, 2 * S5_COLS)), per_g((S5_COLS, S5_COLS)),
                  per_g((n_steps, S5_COLS)), per_g((n_steps, S5_COLS))],
        out_specs=per_g((rows, S5_COLS)),
        out_shape=jax.ShapeDtypeStruct((g, rows, S5_COLS), BF16),
        compiler_params=_params(("arbitrary",), 32),
        name="s5",
    )(ug, w1, w2, ar, ai)


def _gelu_tanh(x):
    return 0.5 * x * (1.0 + jnp.tanh(math.sqrt(2.0 / math.pi) * (x + 0.044715 * (x * x * x))))


def _merge_body(x_ref, of_ref, ob_ref, go_ref, ys_ref, u_ref, ga_ref, gb_ref,
                hn_ref, sd_ref, wglu_ref, bglu_ref, wa_ref, wb_ref, wo_ref, nf_ref, wr_ref,
                x1_ref, h2_ref, lg_ref):
    o = of_ref[...].astype(F32) + ob_ref[...].astype(F32)
    parts = []
    for h in range(A_HEADS):
        oh = o[:, h * A_HEAD_DIM:(h + 1) * A_HEAD_DIM]
        parts.append(oh * lax.rsqrt(jnp.mean(oh * oh, axis=-1, keepdims=True) + RMS_EPS))
    h_a = jnp.concatenate(parts, axis=1) * hn_ref[...] * jax.nn.silu(go_ref[...].astype(F32))

    u = u_ref[...].astype(F32)
    ys = _gelu_tanh(ys_ref[...].astype(F32) + sd_ref[...] * u)
    z = jnp.dot(ys.astype(BF16), wglu_ref[...], preferred_element_type=F32) + bglu_ref[...]
    h_b = ys * jax.nn.sigmoid(z)

    y_a = jnp.dot(h_a.astype(BF16), wa_ref[...], preferred_element_type=F32)
    y_b = jnp.dot(h_b.astype(BF16), wb_ref[...], preferred_element_type=F32)
    merged = jax.nn.sigmoid(ga_ref[...].astype(F32)) * y_a + jax.nn.sigmoid(gb_ref[...].astype(F32)) * y_b
    x1 = x_ref[...] + jnp.dot(merged.astype(BF16), wo_ref[...], preferred_element_type=F32)
    x1_ref[...] = x1

    h2 = (x1 * lax.rsqrt(jnp.mean(x1 * x1, axis=-1, keepdims=True) + RMS_EPS)) * nf_ref[...]
    h2_hi = h2.astype(BF16)
    h2_ref[...] = h2_hi
    h2_lo = (h2 - h2_hi.astype(F32)).astype(BF16)
    wr = wr_ref[...]
    wr_hi = wr.astype(BF16)
    wr_lo = (wr - wr_hi.astype(F32)).astype(BF16)
    lg = (jnp.dot(h2_hi, wr_hi, preferred_element_type=F32) + jnp.dot(h2_lo, wr_hi, preferred_element_type=F32)
          + jnp.dot(h2_hi, wr_lo, preferred_element_type=F32))
    lg_ref[...] = lg[:, :N_EXPERTS]


def _merge(x, o_fw, o_bw, hg, ys5, u, gates, hn, sd, wglu, bglu, wa, wb, wo, nf, wr_pad, tm=512):
    t = x.shape[0]
    row = lambda w, col=0: pl.BlockSpec((tm, w), lambda i: (i, col))
    full = lambda a: pl.BlockSpec(a.shape, lambda i: (0,) * a.ndim)
    return pl.pallas_call(
        _merge_body,
        grid=(t // tm,),
        in_specs=[row(D_MODEL), row(A_WIDTH), row(A_WIDTH), row(A_WIDTH, 4), row(S5_WIDTH), row(S5_WIDTH),
                  row(D_MODEL, 0), row(D_MODEL, 1),
                  full(hn), full(sd), full(wglu), full(bglu), full(wa), full(wb), full(wo), full(nf), full(wr_pad)],
        out_specs=[row(D_MODEL), row(D_MODEL), row(N_EXPERTS)],
        out_shape=[jax.ShapeDtypeStruct((t, D_MODEL), F32),
                   jax.ShapeDtypeStruct((t, D_MODEL), BF16),
                   jax.ShapeDtypeStruct((t, N_EXPERTS), F32)],
        compiler_params=_params(("arbitrary",), 48),
        name="merge",
    )(x, o_fw, o_bw, hg, ys5, u, gates, gates, hn, sd, wglu, bglu, wa, wb, wo, nf, wr_pad)


def _exclusive_prefix(m, upper, strict_lower):
    mb = m.astype(BF16)
    incl = jnp.dot(mb, upper, preferred_element_type=F32)
    row_off = jnp.sum(jnp.dot(strict_lower, mb, preferred_element_type=F32), axis=1, keepdims=True)
    return incl - m + row_off, row_off


def _select_body(lg_ref, pos_ref, gate_ref, lo_ref, aff_ref, bits_ref, *, rpe, cap):
    e_n = N_EXPERTS
    slab = lambda ref, e: ref[e * rpe:(e + 1) * rpe, :]
    mx = slab(lg_ref, 0)
    for e in range(1, e_n):
        mx = jnp.maximum(mx, slab(lg_ref, e))
    den = jnp.zeros_like(mx)
    for e in range(e_n):
        ex = jnp.exp(slab(lg_ref, e) - mx)
        aff_ref[e * rpe:(e + 1) * rpe, :] = ex
        den = den + ex
    for e in range(e_n):
        a = slab(aff_ref, e) / den
        aff_ref[e * rpe:(e + 1) * rpe, :] = a
        bits_ref[e * rpe:(e + 1) * rpe, :] = pltpu.bitcast(a, I32)

    capf = float(cap)

    def bit_step(i, cand):
        bit = jnp.left_shift(jnp.int32(1), 30 - i)
        rows = []
        for e in range(e_n):
            cur = cand[e:e + 1, :]
            trial = cur | bit
            cnt = jnp.sum(jnp.where(slab(bits_ref, e) >= trial, 1.0, 0.0), keepdims=True)
            rows.append(jnp.where(cnt >= capf, trial, cur))
        return jnp.concatenate(rows, axis=0)

    thr = lax.fori_loop(0, 31, bit_step, jnp.zeros((e_n, LANES), I32))

    upper = jnp.where(lax.broadcasted_iota(I32, (LANES, LANES), 0) <= lax.broadcasted_iota(I32, (LANES, LANES), 1),
                      1.0, 0.0).astype(BF16)
    strict_lower = jnp.where(lax.broadcasted_iota(I32, (rpe, rpe), 1) < lax.broadcasted_iota(I32, (rpe, rpe), 0),
                             1.0, 0.0).astype(BF16)
    for e in range(e_n):
        bits = slab(bits_ref, e)
        t = thr[e:e + 1, :]
        gt = jnp.where(bits > t, 1.0, 0.0)
        eq = jnp.where(bits == t, 1.0, 0.0)
        need = capf - jnp.sum(gt, keepdims=True)
        rank_eq, _ = _exclusive_prefix(eq, upper, strict_lower)
        sel = gt + eq * jnp.where(rank_eq < need, 1.0, 0.0)
        pos, row_off = _exclusive_prefix(sel, upper, strict_lower)
        chosen = sel > 0.5
        pos_ref[e * rpe:(e + 1) * rpe, :] = jnp.where(chosen, pos, -1.0).astype(I32)
        gate_ref[e * rpe:(e + 1) * rpe, :] = jnp.where(chosen, slab(aff_ref, e), 0.0)
        lo_ref[e * rpe:(e + 1) * rpe, :] = jnp.broadcast_to(row_off, (rpe, LANES)).astype(I32)


def _select(lg_em, rpe, cap):
    rows = N_EXPERTS * rpe
    shp = lambda dt: jax.ShapeDtypeStruct((rows, LANES), dt)
    return pl.pallas_call(
        functools.partial(_select_body, rpe=rpe, cap=cap),
        out_shape=[shp(I32), shp(F32), shp(I32)],
        scratch_shapes=[pltpu.VMEM((rows, LANES), F32), pltpu.VMEM((rows, LANES), I32)],
        compiler_params=pltpu.CompilerParams(vmem_limit_bytes=32 * MIB),
        name="select",
    )(lg_em)


def _block_lo(lo_ref, e, blk, rpe):
    return lo_ref[e * rpe + blk * (TOK_BLOCK // LANES)]


def _block_count(lo_ref, e, blk, rpe, n_blk, cap):
    nxt = lo_ref[e * rpe + jnp.minimum((blk + 1) * (TOK_BLOCK // LANES), rpe - 1)]
    return jnp.where(blk == n_blk - 1, cap, nxt) - _block_lo(lo_ref, e, blk, rpe)


def _store_row_tiles(dst, val, n_rows):
    for s in range(D_MODEL // LANES):
        dst[pl.ds(s, n_rows, stride=ROW_TILE), :] = val[:, s * LANES:(s + 1) * LANES]


def _load_row_tiles(src, n_rows):
    return [src[pl.ds(s, n_rows, stride=ROW_TILE), :] for s in range(D_MODEL // LANES)]


def _dispatch_body(lo_ref, pm_ref, h_ref, xe_ref, stage_ref, sem_ref, *, rpe, n_blk, cap):
    b = pl.program_id(0)
    pm = pm_ref[...]
    hb = h_ref[...]
    r_idx = lax.broadcasted_iota(I32, (WIN, TOK_BLOCK), 0)
    los = [_block_lo(lo_ref, e, b, rpe) for e in range(N_EXPERTS)]
    onehot = jnp.concatenate(
        [jnp.where(r_idx == pm[e:e + 1, :] - los[e], 1.0, 0.0) for e in range(N_EXPERTS)], axis=0).astype(BF16)
    xw = jnp.dot(onehot, hb, preferred_element_type=F32)

    def window_copy(e, first_row):
        start = pl.multiple_of(first_row * ROW_TILE, ROW_TILE)
        return pltpu.make_async_copy(stage_ref.at[e], xe_ref.at[e, pl.ds(start, WIN * ROW_TILE), :], sem_ref.at[e])

    for e in range(N_EXPERTS):
        @pl.when(b > 0)
        def _():
            window_copy(e, 0).wait()

        _store_row_tiles(stage_ref.at[e], xw[e * WIN:(e + 1) * WIN, :], WIN)
        window_copy(e, los[e]).start()
        n_win = (_block_count(lo_ref, e, b, rpe, n_blk, cap) + (WIN - 1)) // WIN

        def extra_window(k, carry):
            window_copy(e, 0).wait()
            oh = jnp.where(r_idx + k * WIN == pm[e:e + 1, :] - los[e], 1.0, 0.0).astype(BF16)
            _store_row_tiles(stage_ref.at[e], jnp.dot(oh, hb, preferred_element_type=F32), WIN)
            window_copy(e, los[e] + k * WIN).start()
            return carry

        lax.fori_loop(1, n_win, extra_window, 0)

    @pl.when(b == n_blk - 1)
    def _():
        for e in range(N_EXPERTS):
            window_copy(e, 0).wait()
            stage_ref[e] = jnp.zeros((WIN * ROW_TILE, LANES), F32)
            window_copy(e, cap).start()
        for e in range(N_EXPERTS):
            window_copy(e, cap).wait()


def _dispatch(lo, posm, h2, cap):
    t = h2.shape[0]
    rpe = t // LANES
    n_blk = t // TOK_BLOCK
    grid_spec = pltpu.PrefetchScalarGridSpec(
        num_scalar_prefetch=1,
        grid=(n_blk,),
        in_specs=[pl.BlockSpec((N_EXPERTS, TOK_BLOCK), lambda b, lo_r: (0, b)),
                  pl.BlockSpec((TOK_BLOCK, D_MODEL), lambda b, lo_r: (b, 0))],
        out_specs=pl.BlockSpec(memory_space=pl.ANY),
        scratch_shapes=[pltpu.VMEM((N_EXPERTS, WIN * ROW_TILE, LANES), F32),
                        pltpu.SemaphoreType.DMA((N_EXPERTS,))],
    )
    return pl.pallas_call(
        functools.partial(_dispatch_body, rpe=rpe, n_blk=n_blk, cap=cap),
        grid_spec=grid_spec,
        out_shape=jax.ShapeDtypeStruct((N_EXPERTS, (cap + WIN) * ROW_TILE, LANES), F32),
        compiler_params=_params(("arbitrary",), 32),
        name="dispatch",
    )(lo, posm, h2)


def _ffn_body(x_ref, wg_ref, wu_ref, wd_ref, y_ref, xb_ref, acc_ref, *, tr, n_f):
    f = pl.program_id(2)

    @pl.when(f == 0)
    def _():
        for s, piece in enumerate(_load_row_tiles(x_ref.at[0], tr)):
            xb_ref[:, s * LANES:(s + 1) * LANES] = piece.astype(BF16)
        acc_ref[...] = jnp.zeros_like(acc_ref)

    x = xb_ref[...]
    a = jnp.dot(x, wg_ref[0].astype(BF16), preferred_element_type=F32)
    u = jnp.dot(x, wu_ref[0].astype(BF16), preferred_element_type=F32)
    h = (jax.nn.silu(a) * u).astype(BF16)
    acc_ref[...] += jnp.dot(h, wd_ref[0].astype(BF16), preferred_element_type=F32)

    @pl.when(f == n_f - 1)
    def _():
        _store_row_tiles(y_ref.at[0], acc_ref[...], tr)


def _ffn(xe, w_gate, w_up, w_down, cap, tr=1024, tf=512):
    tr = min(tr, cap)
    n_r, n_f = cap // tr, EXPERT_FF // tf
    return pl.pallas_call(
        functools.partial(_ffn_body, tr=tr, n_f=n_f),
        grid=(N_EXPERTS, n_r, n_f),
        in_specs=[pl.BlockSpec((1, tr * ROW_TILE, LANES), lambda e, r, f: (e, r, 0)),
                  pl.BlockSpec((1, D_MODEL, tf), lambda e, r, f: (e, 0, f)),
                  pl.BlockSpec((1, D_MODEL, tf), lambda e, r, f: (e, 0, f)),
                  pl.BlockSpec((1, tf, D_MODEL), lambda e, r, f: (e, f, 0))],
        out_specs=pl.BlockSpec((1, tr * ROW_TILE, LANES), lambda e, r, f: (e, r, 0)),
        out_shape=jax.ShapeDtypeStruct((N_EXPERTS, cap * ROW_TILE, LANES), F32),
        scratch_shapes=[pltpu.VMEM((tr, D_MODEL), BF16), pltpu.VMEM((tr, D_MODEL), F32)],
        compiler_params=_params(("arbitrary", "arbitrary", "arbitrary"), 56),
        name="expert_ffn",
    )(xe, w_gate, w_up, w_down)


def _combine_body(lo_ref, pm_ref, gm_ref, x1_ref, nf_ref, y2_ref, out_ref,
                  win_ref, a_ref, m_ref, ovw_ref, sem_ref, ovsem_ref, *, rpe, n_blk, cap):
    b = pl.program_id(0)
    slot = b % 2

    def win_start(e, blk):
        return jnp.minimum(_block_lo(lo_ref, e, blk, rpe), cap - WIN)

    def window_copy(e, blk, sl):
        start = pl.multiple_of(win_start(e, blk) * ROW_TILE, ROW_TILE)
        return pltpu.make_async_copy(y2_ref.at[e, pl.ds(start, WIN * ROW_TILE), :], win_ref.at[sl, e],
                                     sem_ref.at[sl, e])

    @pl.when(b == 0)
    def _():
        for e in range(N_EXPERTS):
            window_copy(e, 0, 0).start()

    @pl.when(b + 1 < n_blk)
    def _():
        for e in range(N_EXPERTS):
            window_copy(e, b + 1, 1 - slot).start()

    pm = pm_ref[...]
    gm = gm_ref[...]
    lane = lax.broadcasted_iota(I32, (TOK_BLOCK, LANES), 1)
    low = lane < WIN
    pieces = []
    for p in range(N_EXPERTS // 2):
        e0, e1 = 2 * p, 2 * p + 1
        tgt = jnp.where(low, pm[:, e0:e0 + 1] - win_start(e0, b), pm[:, e1:e1 + 1] - win_start(e1, b) + WIN)
        gv = jnp.where(low, gm[:, e0:e0 + 1], gm[:, e1:e1 + 1])
        pieces.append(jnp.where(lane == tgt, gv, 0.0).astype(BF16))
    gmat = jnp.concatenate(pieces, axis=1)

    for e in range(N_EXPERTS):
        window_copy(e, b, slot).wait()
        for s, piece in enumerate(_load_row_tiles(win_ref.at[slot, e], WIN)):
            a_ref[e * WIN:(e + 1) * WIN, s * LANES:(s + 1) * LANES] = piece.astype(BF16)
    m_ref[...] = jnp.dot(gmat, a_ref[...], preferred_element_type=F32)

    lane_w = lax.broadcasted_iota(I32, (TOK_BLOCK, WIN), 1)
    for e in range(N_EXPERTS):
        lo_e = _block_lo(lo_ref, e, b, rpe)
        n_win = (_block_count(lo_ref, e, b, rpe, n_blk, cap) + (WIN - 1)) // WIN

        def extra_window(k, carry):
            first = lo_e + k * WIN
            ws = jnp.minimum(first, cap - WIN)
            cp = pltpu.make_async_copy(
                y2_ref.at[e, pl.ds(pl.multiple_of(ws * ROW_TILE, ROW_TILE), WIN * ROW_TILE), :], ovw_ref, ovsem_ref)
            cp.start()
            cp.wait()
            rows = jnp.concatenate([p.astype(BF16) for p in _load_row_tiles(ovw_ref, WIN)], axis=1)
            pe = pm[:, e:e + 1]
            gk = jnp.where((lane_w == pe - ws) & (pe >= first), gm[:, e:e + 1], 0.0).astype(BF16)
            m_ref[...] += jnp.dot(gk, rows, preferred_element_type=F32)
            return carry

        lax.fori_loop(1, n_win, extra_window, 0)

    xo = x1_ref[...] + m_ref[...]
    out_ref[...] = (xo * lax.rsqrt(jnp.mean(xo * xo, axis=-1, keepdims=True) + RMS_EPS)) * nf_ref[...]


def _combine(lo, posm_t, gate_t, x1, norm_final, y2, cap):
    t = x1.shape[0]
    rpe = t // LANES
    n_blk = t // TOK_BLOCK
    grid_spec = pltpu.PrefetchScalarGridSpec(
        num_scalar_prefetch=1,
        grid=(n_blk,),
        in_specs=[pl.BlockSpec((TOK_BLOCK, N_EXPERTS), lambda b, lo_r: (b, 0)),
                  pl.BlockSpec((TOK_BLOCK, N_EXPERTS), lambda b, lo_r: (b, 0)),
                  pl.BlockSpec((TOK_BLOCK, D_MODEL), lambda b, lo_r: (b, 0)),
                  pl.BlockSpec((1, D_MODEL), lambda b, lo_r: (0, 0)),
                  pl.BlockSpec(memory_space=pl.ANY)],
        out_specs=pl.BlockSpec((TOK_BLOCK, D_MODEL), lambda b, lo_r: (b, 0)),
        scratch_shapes=[pltpu.VMEM((2, N_EXPERTS, WIN * ROW_TILE, LANES), F32),
                        pltpu.VMEM((N_EXPERTS * WIN, D_MODEL), BF16),
                        pltpu.VMEM((TOK_BLOCK, D_MODEL), F32),
                        pltpu.VMEM((WIN * ROW_TILE, LANES), F32),
                        pltpu.SemaphoreType.DMA((2, N_EXPERTS)),
                        pltpu.SemaphoreType.DMA(())],
    )
    return pl.pallas_call(
        functools.partial(_combine_body, rpe=rpe, n_blk=n_blk, cap=cap),
        grid_spec=grid_spec,
        out_shape=jax.ShapeDtypeStruct((t, D_MODEL), F32),
        compiler_params=_params(("arbitrary",), 40),
        name="combine",
    )(lo, posm_t, gate_t, x1, norm_final, y2)


def _trunk(x, p):
    b, l, _ = x.shape
    t = b * l
    n_seq = l // S5_CHUNK
    cap = max(1, EC_CAPACITY_FACTOR * t // N_EXPERTS)
    rpe = t // LANES

    hg, u, gates = _inproj(x.reshape(t, D_MODEL), p['norm_mix'], p['w_in'])
    o_fw, o_bw = _hgrn(hg.reshape(b, l, 5 * A_WIDTH), p['lb'])

    ug = u.reshape(b, n_seq, S5_CHUNK, S5_GROUPS, S5_GROUP).transpose(3, 0, 1, 2, 4)
    w1, w2, ar, ai = p['s5'][n_seq]
    yg = _s5(ug.reshape(S5_GROUPS, b * n_seq, S5_COLS), w1, w2, ar, ai, n_seq)
    ys5 = yg.reshape(S5_GROUPS, b, n_seq, S5_CHUNK, S5_GROUP).transpose(1, 2, 3, 0, 4).reshape(t, S5_WIDTH)

    x1, h2, logits = _merge(x.reshape(t, D_MODEL), o_fw.reshape(t, A_WIDTH), o_bw.reshape(t, A_WIDTH), hg, ys5, u,
                            gates, p['hgrn_norm'], p['s5_d'], p['w_glu'], p['b_glu'], p['w_a'], p['w_b'],
                            p['w_out'], p['norm_ffn'], p['w_router'])

    posm, gate, lo = _select(logits.T.reshape(N_EXPERTS * rpe, LANES), rpe, cap)
    lo = lo[:, 0]
    posm = posm.reshape(N_EXPERTS, t)
    xe = _dispatch(lo, posm, h2, cap)
    y2 = _ffn(xe, p['w_gate'], p['w_up'], p['w_down'], cap)
    out = _combine(lo, posm.T, gate.reshape(N_EXPERTS, t).T, x1, p['norm_final'], y2, cap)
    return out.reshape(b, l, D_MODEL)


def kernel(x_prompt, x_sample, norm_mix, w_in, hgrn_gamma, hgrn_norm, s5_a_re, s5_a_im, s5_log_dt, s5_b_re, s5_b_im, s5_c_re, s5_c_im, s5_d, s5_w_glu, s5_b_glu, w_branch_a, w_branch_b, w_out, norm_ffn, w_router, w_exp_gate, w_exp_up, w_exp_down, norm_final):
    assert norm_mix.shape[0] == 1, "single-layer trunk"
    row = lambda a: a.astype(F32).reshape(1, -1)
    lb = jnp.cumsum(jax.nn.softmax(hgrn_gamma.astype(F32), axis=0), axis=0)[0]
    seq_chunks = sorted({x_prompt.shape[1] // S5_CHUNK, x_sample.shape[1] // S5_CHUNK})
    s5_ops = {n: _s5_operators(s5_a_re[0], s5_a_im[0], s5_log_dt[0], s5_b_re[0], s5_b_im[0], s5_c_re[0], s5_c_im[0],
                               n.bit_length() - 1) for n in seq_chunks}
    p = {
        'norm_mix': row(norm_mix[0]), 'w_in': w_in[0].astype(BF16), 'lb': lb, 's5': s5_ops,
        'hgrn_norm': row(hgrn_norm[0]), 's5_d': row(s5_d[0]), 'w_glu': s5_w_glu[0].astype(BF16),
        'b_glu': row(s5_b_glu[0]), 'w_a': w_branch_a[0].astype(BF16), 'w_b': w_branch_b[0].astype(BF16),
        'w_out': w_out[0].astype(BF16), 'norm_ffn': row(norm_ffn[0]),
        'w_router': jnp.pad(w_router[0].astype(F32), ((0, 0), (0, LANES - N_EXPERTS))),
        'w_gate': w_exp_gate[0], 'w_up': w_exp_up[0], 'w_down': w_exp_down[0],
        'norm_final': row(norm_final),
    }
    return (_trunk(x_prompt, p), _trunk(x_sample, p))
```

```python
import functools
import math

import jax
import jax.numpy as jnp
from jax import lax
from jax.experimental import pallas as pl
from jax.experimental.pallas import tpu as pltpu

F32 = jnp.float32
BF16 = jnp.bfloat16
I32 = jnp.int32

D_MODEL = 1024
A_HEADS = 4
A_HEAD_DIM = 128
A_WIDTH = A_HEADS * A_HEAD_DIM
A_CHUNK = 64
S5_WIDTH = 512
S5_GROUP = 16
S5_GROUPS = S5_WIDTH // S5_GROUP
S5_STATE = 64
N_EXPERTS = 16
EXPERT_FF = 2048
EC_CAPACITY_FACTOR = 2
RMS_EPS = 1e-6
IN_COLS = 5 * A_WIDTH + S5_WIDTH + 2 * D_MODEL

LANES = 128
BF16_TILE_ROWS = 16
S5_CHUNK = 16
S5_COLS = S5_CHUNK * S5_GROUP
S5_GROUPS_PER_STEP = LANES // S5_GROUP
TOK_BLOCK = 256
COMBINE_WIN = 64
DISPATCH_WIN = COMBINE_WIN + BF16_TILE_ROWS
MIB = 2 ** 20


def _params(semantics, vmem_mib):
    return pltpu.CompilerParams(dimension_semantics=semantics, vmem_limit_bytes=vmem_mib * MIB)


def _nt_dot(a, b):
    return lax.dot_general(a, b, (((1,), (1,)), ((), ())), preferred_element_type=F32)


def _tn_dot(a, b):
    return lax.dot_general(a, b, (((0,), (0,)), ((), ())), preferred_element_type=F32)


def _inproj_body(x_ref, g_ref, w_ref, hg_ref, u_ref, us_ref, gt_ref, uf_ref, *, tm):
    x = x_ref[...]
    ms = jnp.mean(x * x, axis=-1, keepdims=True)
    h = ((x * lax.rsqrt(ms + RMS_EPS)) * g_ref[...]).astype(BF16)
    n_hg = 5 * A_WIDTH // 512
    for c in range(IN_COLS // 512):
        r = jnp.dot(h, w_ref[:, c * 512:(c + 1) * 512], preferred_element_type=F32)
        if c < n_hg:
            hg_ref[:, c * 512:(c + 1) * 512] = r.astype(BF16)
        elif c == n_hg:
            u_ref[...] = r.astype(BF16)
            for cb in range(S5_WIDTH // LANES):
                uf_ref[cb] = r[:, cb * LANES:(cb + 1) * LANES]
                for sg in range(S5_CHUNK):
                    us_ref[sg, :, cb * LANES:(cb + 1) * LANES] = uf_ref[
                        cb, pl.ds(sg, tm // S5_CHUNK, stride=S5_CHUNK), :].astype(BF16)
        else:
            gt_ref[:, (c - n_hg - 1) * 512:(c - n_hg) * 512] = r.astype(BF16)


def _inproj(x, gain, w_bf16, tm=512):
    t = x.shape[0]
    return pl.pallas_call(
        functools.partial(_inproj_body, tm=tm),
        grid=(t // tm,),
        in_specs=[pl.BlockSpec((tm, D_MODEL), lambda i: (i, 0)),
                  pl.BlockSpec((1, D_MODEL), lambda i: (0, 0)),
                  pl.BlockSpec((D_MODEL, IN_COLS), lambda i: (0, 0))],
        out_specs=[pl.BlockSpec((tm, 5 * A_WIDTH), lambda i: (i, 0)),
                   pl.BlockSpec((tm, S5_WIDTH), lambda i: (i, 0)),
                   pl.BlockSpec((S5_CHUNK, tm // S5_CHUNK, S5_WIDTH), lambda i: (0, i, 0)),
                   pl.BlockSpec((tm, 2 * D_MODEL), lambda i: (i, 0))],
        out_shape=[jax.ShapeDtypeStruct((t, 5 * A_WIDTH), BF16),
                   jax.ShapeDtypeStruct((t, S5_WIDTH), BF16),
                   jax.ShapeDtypeStruct((S5_CHUNK, t // S5_CHUNK, S5_WIDTH), BF16),
                   jax.ShapeDtypeStruct((t, 2 * D_MODEL), BF16)],
        scratch_shapes=[pltpu.VMEM((S5_WIDTH // LANES, tm, LANES), F32)],
        compiler_params=_params(("arbitrary",), 48),
        name="inproj",
    )(x, gain, w_bf16)


def _split3(x):
    hi = x.astype(BF16)
    r1 = x - hi.astype(F32)
    mid = r1.astype(BF16)
    lo = (r1 - mid.astype(F32)).astype(BF16)
    return hi, mid, lo


def _hgrn_chunk(q, fr, v, lb, s_ref, tri, mid, last):
    q = q.astype(F32)
    f = lb + (1.0 - lb) * jax.nn.sigmoid(fr.astype(F32))
    lf = jnp.log(f)
    kk = 1.0 - f
    tri_b = jnp.where(tri, 1.0, 0.0).astype(BF16)
    hi, md, lo = _split3(lf)
    bcum = (jnp.dot(tri_b, hi, preferred_element_type=F32)
            + jnp.dot(tri_b, md, preferred_element_type=F32)
            + jnp.dot(tri_b, lo, preferred_element_type=F32))
    bmid = bcum[mid:mid + 1, :]
    blast = bcum[last:last + 1, :]
    q_in = (q * jnp.exp(bcum - bmid)).astype(BF16)
    k_in = (kk * jnp.exp(bmid - bcum)).astype(BF16)
    qdec = (q * jnp.exp(bcum)).astype(BF16)
    kdec = (kk * jnp.exp(blast - bcum)).astype(BF16)
    decay = jnp.exp(blast)
    outs = []
    for h in range(A_HEADS):
        sl = slice(h * A_HEAD_DIM, (h + 1) * A_HEAD_DIM)
        s = jnp.where(tri, _nt_dot(q_in[:, sl], k_in[:, sl]), 0.0).astype(BF16)
        st = s_ref[h]
        o = jnp.dot(s, v[:, sl], preferred_element_type=F32) + _nt_dot(qdec[:, sl], st.astype(BF16))
        s_ref[h] = decay[:, sl] * st + _tn_dot(v[:, sl], kdec[:, sl])
        outs.append(o)
    return jnp.concatenate(outs, axis=1).astype(BF16)


def _hgrn_body(qf_ref, ff_ref, vf_ref, qb_ref, fb_ref, vb_ref, lb_ref, of_ref, ob_ref, sf_ref, sb_ref,
               *, n_chunks):
    @pl.when(pl.program_id(1) == 0)
    def _():
        sf_ref[...] = jnp.zeros_like(sf_ref)
        sb_ref[...] = jnp.zeros_like(sb_ref)

    c = A_CHUNK
    t_idx = lax.broadcasted_iota(I32, (c, c), 0)
    s_idx = lax.broadcasted_iota(I32, (c, c), 1)
    tri_f = s_idx <= t_idx
    tri_b = s_idx >= t_idx

    def step(j, carry):
        rf = pl.ds(pl.multiple_of(j * c, c), c)
        of_ref[0, rf, :] = _hgrn_chunk(qf_ref[0, rf, :], ff_ref[0, rf, :], vf_ref[0, rf, :],
                                       lb_ref[0:1, :], sf_ref, tri_f, c // 2 - 1, c - 1)
        rb = pl.ds(pl.multiple_of((n_chunks - 1 - j) * c, c), c)
        ob_ref[0, rb, :] = _hgrn_chunk(qb_ref[0, rb, :], fb_ref[0, rb, :], vb_ref[0, rb, :],
                                       lb_ref[1:2, :], sb_ref, tri_b, c // 2, 0)
        return carry

    lax.fori_loop(0, n_chunks, step, 0)


def _hgrn(hg, lb, lb_tokens=512):
    b, l, _ = hg.shape
    nb = l // lb_tokens
    blk = (1, lb_tokens, A_WIDTH)
    fwd = lambda col: pl.BlockSpec(blk, lambda bi, i: (bi, i, col))
    bwd = lambda col: pl.BlockSpec(blk, lambda bi, i: (bi, nb - 1 - i, col))
    return pl.pallas_call(
        functools.partial(_hgrn_body, n_chunks=lb_tokens // A_CHUNK),
        grid=(b, nb),
        in_specs=[fwd(0), fwd(1), fwd(3), bwd(0), bwd(2), bwd(3),
                  pl.BlockSpec((2, A_WIDTH), lambda bi, i: (0, 0))],
        out_specs=[pl.BlockSpec(blk, lambda bi, i: (bi, i, 0)),
                   pl.BlockSpec(blk, lambda bi, i: (bi, nb - 1 - i, 0))],
        out_shape=[jax.ShapeDtypeStruct((b, l, A_WIDTH), BF16)] * 2,
        scratch_shapes=[pltpu.VMEM((A_HEADS, A_HEAD_DIM, A_HEAD_DIM), F32)] * 2,
        compiler_params=_params(("arbitrary", "arbitrary"), 32),
        name="hgrn2",
    )(hg, hg, hg, hg, hg, hg, lb)


def _s5_direction_terms(a_re, a_im, log_dt, b_re, b_im):
    a_re, a_im = a_re.astype(F32), a_im.astype(F32)
    b_re, b_im = b_re.astype(F32), b_im.astype(F32)
    dt = jnp.exp(log_dt.astype(F32))[:, None]
    j = jnp.arange(S5_CHUNK + 1, dtype=F32)[:, None, None]
    mag = jnp.exp(j * (dt * a_re))
    pw_re = mag * jnp.cos(j * (dt * a_im))
    pw_im = mag * jnp.sin(j * (dt * a_im))
    nr, ni = pw_re[1] - 1.0, pw_im[1]
    den = a_re * a_re + a_im * a_im
    f_re = (nr * a_re + ni * a_im) / den
    f_im = (ni * a_re - nr * a_im) / den
    bb_re = f_re[..., None] * b_re - f_im[..., None] * b_im
    bb_im = f_re[..., None] * b_im + f_im[..., None] * b_re
    pb_re = pw_re[..., None] * bb_re - pw_im[..., None] * bb_im
    pb_im = pw_re[..., None] * bb_im + pw_im[..., None] * bb_re
    return pw_re, pw_im, pb_re, pb_im


def _s5_operators(a_re, a_im, log_dt, b_re, b_im, c_re, c_im, n_steps):
    hp = lax.Precision.HIGHEST
    c_re, c_im = c_re.astype(F32), c_im.astype(F32)
    g = S5_GROUPS
    terms = [_s5_direction_terms(a_re[d], a_im[d], log_dt[d], b_re, b_im) for d in range(2)]
    taps = []
    for pw_re, pw_im, pb_re, pb_im in terms:
        k = (jnp.einsum('gop,jgpi->jgoi', c_re, pb_re[:S5_CHUNK], precision=hp)
             - jnp.einsum('gop,jgpi->jgoi', c_im, pb_im[:S5_CHUNK], precision=hp))
        taps.append(k)
    sig = jnp.arange(S5_CHUNK)[:, None]
    tau = jnp.arange(S5_CHUNK)[None, :]
    d = tau - sig
    kf = jnp.where((d >= 0)[:, :, None, None, None], taps[0][jnp.clip(d, 0, S5_CHUNK - 1)], 0.0)
    kb = jnp.where((d <= 0)[:, :, None, None, None], taps[1][jnp.clip(-d, 0, S5_CHUNK - 1)], 0.0)
    w_intra = (kf + kb).transpose(2, 0, 4, 1, 3).reshape(g, S5_COLS, S5_COLS)

    def w_in(pb_re, pb_im, order):
        re = pb_re[order].transpose(1, 0, 3, 2)
        im = pb_im[order].transpose(1, 0, 3, 2)
        return jnp.concatenate([re, im], axis=-1).reshape(g, S5_COLS, 2 * S5_STATE)

    def w_out(pw_re, pw_im, order):
        pr, pi = pw_re[order], pw_im[order]
        m_re = c_re[None] * pr[:, :, None, :] - c_im[None] * pi[:, :, None, :]
        m_im = c_re[None] * pi[:, :, None, :] + c_im[None] * pr[:, :, None, :]
        re = m_re.transpose(1, 3, 0, 2).reshape(g, S5_STATE, S5_COLS)
        im = -m_im.transpose(1, 3, 0, 2).reshape(g, S5_STATE, S5_COLS)
        return jnp.concatenate([re, im], axis=1)

    asc = jnp.arange(S5_CHUNK)
    w_in_f = w_in(terms[0][2], terms[0][3], S5_CHUNK - 1 - asc)
    w_in_b = w_in(terms[1][2], terms[1][3], asc)
    w_out_f = w_out(terms[0][0], terms[0][1], asc + 1)
    w_out_b = w_out(terms[1][0], terms[1][1], S5_CHUNK - asc)
    w1 = jnp.concatenate([w_intra, w_in_f, w_in_b], axis=-1).astype(BF16)
    w2 = jnp.concatenate([w_out_f, w_out_b], axis=1).astype(BF16)

    ars, ais = [], []
    cur = [(t[0][S5_CHUNK], t[1][S5_CHUNK]) for t in terms]
    for _ in range(n_steps):
        ars.append(jnp.concatenate([cur[0][0], cur[0][0], cur[1][0], cur[1][0]], axis=-1))
        ais.append(jnp.concatenate([-cur[0][1], cur[0][1], -cur[1][1], cur[1][1]], axis=-1))
        cur = [(r * r - i * i, 2.0 * r * i) for r, i in cur]
    return w1, w2, jnp.stack(ars, axis=1), jnp.stack(ais, axis=1)


def _s5_group(u, w1, w2, ar, ai, n, n_seq, n_steps):
    rows = u.shape[0]
    half = 2 * S5_STATE
    yv = jnp.dot(u, w1, preferred_element_type=F32)
    zf = jnp.where(n >= 1, pltpu.roll(yv[:, S5_COLS:S5_COLS + half], 1, axis=0), 0.0)
    zb = jnp.where(n < n_seq - 1, pltpu.roll(yv[:, S5_COLS + half:], rows - 1, axis=0), 0.0)
    for k in range(n_steps):
        s = 1 << k
        sf = jnp.where(n >= s, pltpu.roll(zf, s, axis=0), 0.0)
        sb = jnp.where(n < n_seq - s, pltpu.roll(zb, rows - s, axis=0), 0.0)
        zf = zf + ar[k:k + 1, :half] * sf + ai[k:k + 1, :half] * pltpu.roll(sf, S5_STATE, axis=1)
        zb = zb + ar[k:k + 1, half:] * sb + ai[k:k + 1, half:] * pltpu.roll(sb, S5_STATE, axis=1)
    x = jnp.concatenate([zf, zb], axis=1).astype(BF16)
    return yv[:, :S5_COLS] + jnp.dot(x, w2, preferred_element_type=F32)


def _s5_body(u_ref, perm_ref, w1_ref, w2_ref, ar_ref, ai_ref, y_ref, z_ref, *, n_seq, n_steps):
    gps = S5_GROUPS_PER_STEP
    rows = u_ref.shape[1]
    n = lax.broadcasted_iota(I32, (rows, 2 * S5_STATE), 0) & (n_seq - 1)
    perm = perm_ref[...]
    for hf in range(S5_CHUNK // gps):
        x = jnp.concatenate([u_ref[hf * gps + j] for j in range(gps)], axis=1)
        z_ref[:, hf * gps * LANES:(hf + 1) * gps * LANES] = jnp.dot(
            x, perm, preferred_element_type=F32).astype(BF16)
    for g in range(gps):
        u = jnp.concatenate([z_ref[:, (hf * gps + g) * LANES:(hf * gps + g + 1) * LANES]
                             for hf in range(S5_CHUNK // gps)], axis=1)
        y = _s5_group(u, w1_ref[g], w2_ref[g], ar_ref[g], ai_ref[g], n, n_seq, n_steps).astype(BF16)
        for hf in range(S5_CHUNK // gps):
            z_ref[:, (hf * gps + g) * LANES:(hf * gps + g + 1) * LANES] = y[:, hf * LANES:(hf + 1) * LANES]
    for hf in range(S5_CHUNK // gps):
        x = jnp.dot(z_ref[:, hf * gps * LANES:(hf + 1) * gps * LANES], perm,
                    preferred_element_type=F32).astype(BF16)
        for j in range(gps):
            y_ref[hf * gps + j] = x[:, j * LANES:(j + 1) * LANES]


def _s5_lane_permutation():
    idx = jnp.arange(S5_GROUPS_PER_STEP * LANES)
    a, b, c = idx // LANES, (idx % LANES) // S5_GROUP, idx % S5_GROUP
    dst = b * LANES + a * S5_GROUP + c
    return (dst[:, None] == idx[None, :]).astype(BF16)


def _s5(us, perm, w1, w2, ar, ai, n_seq):
    _, rows, _ = us.shape
    n_steps = n_seq.bit_length() - 1
    gps = S5_GROUPS_PER_STEP
    tok = pl.BlockSpec((S5_CHUNK, rows, LANES), lambda i: (0, 0, i))
    per_g = lambda a: pl.BlockSpec((gps,) + a.shape[1:], lambda i: (i, 0, 0))
    return pl.pallas_call(
        functools.partial(_s5_body, n_seq=n_seq, n_steps=n_steps),
        grid=(S5_GROUPS // gps,),
        in_specs=[tok, pl.BlockSpec(perm.shape, lambda i: (0, 0)), per_g(w1), per_g(w2), per_g(ar), per_g(ai)],
        out_specs=tok,
        out_shape=jax.ShapeDtypeStruct(us.shape, BF16),
        scratch_shapes=[pltpu.VMEM((rows, S5_CHUNK * LANES), BF16)],
        compiler_params=_params(("arbitrary",), 48),
        name="s5",
    )(us, perm, w1, w2, ar, ai)


def _gelu_tanh(x):
    return 0.5 * x * (1.0 + jnp.tanh(math.sqrt(2.0 / math.pi) * (x + 0.044715 * (x * x * x))))


def _merge_body(x_ref, of_ref, ob_ref, go_ref, ys_ref, u_ref, ga_ref, gb_ref,
                hn_ref, sd_ref, wglu_ref, bglu_ref, wa_ref, wb_ref, wo_ref, nf_ref, wr_ref,
                x1_ref, h2_ref, lg_ref, yn_ref, *, tm):
    for sg in range(S5_CHUNK):
        for cb in range(S5_WIDTH // LANES):
            yn_ref[cb, pl.ds(sg, tm // S5_CHUNK, stride=S5_CHUNK), :] = ys_ref[
                sg, :, cb * LANES:(cb + 1) * LANES].astype(F32)
    o = of_ref[...].astype(F32) + ob_ref[...].astype(F32)
    parts = []
    for h in range(A_HEADS):
        oh = o[:, h * A_HEAD_DIM:(h + 1) * A_HEAD_DIM]
        parts.append(oh * lax.rsqrt(jnp.mean(oh * oh, axis=-1, keepdims=True) + RMS_EPS))
    h_a = jnp.concatenate(parts, axis=1) * hn_ref[...] * jax.nn.silu(go_ref[...].astype(F32))

    u = u_ref[...].astype(F32)
    y_s5 = jnp.concatenate([yn_ref[cb] for cb in range(S5_WIDTH // LANES)], axis=1)
    ys = _gelu_tanh(y_s5 + sd_ref[...] * u)
    z = jnp.dot(ys.astype(BF16), wglu_ref[...], preferred_element_type=F32) + bglu_ref[...]
    h_b = ys * jax.nn.sigmoid(z)

    y_a = jnp.dot(h_a.astype(BF16), wa_ref[...], preferred_element_type=F32)
    y_b = jnp.dot(h_b.astype(BF16), wb_ref[...], preferred_element_type=F32)
    merged = jax.nn.sigmoid(ga_ref[...].astype(F32)) * y_a + jax.nn.sigmoid(gb_ref[...].astype(F32)) * y_b
    x1 = x_ref[...] + jnp.dot(merged.astype(BF16), wo_ref[...], preferred_element_type=F32)
    x1_ref[...] = x1

    h2 = (x1 * lax.rsqrt(jnp.mean(x1 * x1, axis=-1, keepdims=True) + RMS_EPS)) * nf_ref[...]
    h2_hi = h2.astype(BF16)
    h2_ref[...] = h2_hi
    h2_lo = (h2 - h2_hi.astype(F32)).astype(BF16)
    wr = wr_ref[...]
    wr_hi = wr.astype(BF16)
    wr_lo = (wr - wr_hi.astype(F32)).astype(BF16)
    lg = (jnp.dot(h2_hi, wr_hi, preferred_element_type=F32) + jnp.dot(h2_lo, wr_hi, preferred_element_type=F32)
          + jnp.dot(h2_hi, wr_lo, preferred_element_type=F32))
    lg_ref[...] = lg[:, :N_EXPERTS]


def _merge(x, o_fw, o_bw, hg, ys5, u, gates, hn, sd, wglu, bglu, wa, wb, wo, nf, wr_pad, tm=512):
    t = x.shape[0]
    row = lambda w, col=0: pl.BlockSpec((tm, w), lambda i: (i, col))
    full = lambda a: pl.BlockSpec(a.shape, lambda i: (0,) * a.ndim)
    return pl.pallas_call(
        functools.partial(_merge_body, tm=tm),
        grid=(t // tm,),
        in_specs=[row(D_MODEL), row(A_WIDTH), row(A_WIDTH), row(A_WIDTH, 4),
                  pl.BlockSpec((S5_CHUNK, tm // S5_CHUNK, S5_WIDTH), lambda i: (0, i, 0)), row(S5_WIDTH),
                  row(D_MODEL, 0), row(D_MODEL, 1),
                  full(hn), full(sd), full(wglu), full(bglu), full(wa), full(wb), full(wo), full(nf), full(wr_pad)],
        out_specs=[row(D_MODEL), row(D_MODEL), row(N_EXPERTS)],
        out_shape=[jax.ShapeDtypeStruct((t, D_MODEL), F32),
                   jax.ShapeDtypeStruct((t, D_MODEL), BF16),
                   jax.ShapeDtypeStruct((t, N_EXPERTS), F32)],
        scratch_shapes=[pltpu.VMEM((S5_WIDTH // LANES, tm, LANES), F32)],
        compiler_params=_params(("arbitrary",), 48),
        name="merge",
    )(x, o_fw, o_bw, hg, ys5, u, gates, gates, hn, sd, wglu, bglu, wa, wb, wo, nf, wr_pad)


def _exclusive_prefix(m, upper, strict_lower):
    mb = m.astype(BF16)
    incl = jnp.dot(mb, upper, preferred_element_type=F32)
    row_off = jnp.sum(jnp.dot(strict_lower, mb, preferred_element_type=F32), axis=1, keepdims=True)
    return incl - m + row_off, row_off


def _select_body(lg_ref, pos_ref, gate_ref, lo_ref, aff_ref, *, rpe, cap):
    e_n = N_EXPERTS
    slab = lambda ref, e: ref[e * rpe:(e + 1) * rpe, :]
    mx = slab(lg_ref, 0)
    for e in range(1, e_n):
        mx = jnp.maximum(mx, slab(lg_ref, e))
    den = jnp.zeros_like(mx)
    for e in range(e_n):
        ex = jnp.exp(slab(lg_ref, e) - mx)
        aff_ref[e * rpe:(e + 1) * rpe, :] = ex
        den = den + ex
    for e in range(e_n):
        aff_ref[e * rpe:(e + 1) * rpe, :] = slab(aff_ref, e) / den

    capf = float(cap)

    def bit_step(i, cand):
        bit = jnp.left_shift(jnp.int32(1), 30 - i)
        rows = []
        for e in range(e_n):
            cur = cand[e:e + 1, :]
            trial = cur | bit
            cnt = jnp.sum(jnp.where(slab(aff_ref, e) >= pltpu.bitcast(trial, F32), 1.0, 0.0), keepdims=True)
            rows.append(jnp.where(cnt >= capf, trial, cur))
        return jnp.concatenate(rows, axis=0)

    thr = pltpu.bitcast(lax.fori_loop(0, 31, bit_step, jnp.zeros((e_n, LANES), I32)), F32)

    upper = jnp.where(lax.broadcasted_iota(I32, (LANES, LANES), 0) <= lax.broadcasted_iota(I32, (LANES, LANES), 1),
                      1.0, 0.0).astype(BF16)
    strict_lower = jnp.where(lax.broadcasted_iota(I32, (rpe, rpe), 1) < lax.broadcasted_iota(I32, (rpe, rpe), 0),
                             1.0, 0.0).astype(BF16)
    for e in range(e_n):
        aff = slab(aff_ref, e)
        t = thr[e:e + 1, :]
        gt = jnp.where(aff > t, 1.0, 0.0)
        eq = jnp.where(aff == t, 1.0, 0.0)
        need = capf - jnp.sum(gt, keepdims=True)
        rank_eq, _ = _exclusive_prefix(eq, upper, strict_lower)
        sel = gt + eq * jnp.where(rank_eq < need, 1.0, 0.0)
        pos, row_off = _exclusive_prefix(sel, upper, strict_lower)
        chosen = sel > 0.5
        pos_ref[e * rpe:(e + 1) * rpe, :] = jnp.where(chosen, pos, -1.0).astype(I32)
        gate_ref[e * rpe:(e + 1) * rpe, :] = jnp.where(chosen, aff, 0.0)
        lo_ref[e * rpe:(e + 1) * rpe, :] = jnp.broadcast_to(row_off, (rpe, LANES)).astype(I32)


def _select(lg_em, rpe, cap):
    rows = N_EXPERTS * rpe
    shp = lambda dt: jax.ShapeDtypeStruct((rows, LANES), dt)
    return pl.pallas_call(
        functools.partial(_select_body, rpe=rpe, cap=cap),
        out_shape=[shp(I32), shp(F32), shp(I32)],
        scratch_shapes=[pltpu.VMEM((rows, LANES), F32)],
        compiler_params=pltpu.CompilerParams(vmem_limit_bytes=32 * MIB),
        name="select",
    )(lg_em)


def _block_lo(lo_ref, e, blk, rpe):
    return lo_ref[e * rpe + blk * (TOK_BLOCK // LANES)]


def _block_count(lo_ref, e, blk, rpe, n_blk, cap):
    nxt = lo_ref[e * rpe + jnp.minimum((blk + 1) * (TOK_BLOCK // LANES), rpe - 1)]
    return jnp.where(blk == n_blk - 1, cap, nxt) - _block_lo(lo_ref, e, blk, rpe)


def _floor_tile(row):
    return row - (row & (BF16_TILE_ROWS - 1))


def _dispatch_body(lo_ref, pm_ref, h_ref, xe_ref, stage_ref, carry_ref, sem_ref, *, rpe, n_blk, cap):
    b = pl.program_id(0)

    @pl.when(b == 0)
    def _():
        carry_ref[...] = jnp.zeros_like(carry_ref)

    pm = pm_ref[...]
    hb = h_ref[...]
    r_idx = lax.broadcasted_iota(I32, (DISPATCH_WIN, TOK_BLOCK), 0)
    head_row = lax.broadcasted_iota(I32, (BF16_TILE_ROWS, D_MODEL), 0)
    los = [_block_lo(lo_ref, e, b, rpe) for e in range(N_EXPERTS)]
    bases = [_floor_tile(lo) for lo in los]
    onehot = jnp.concatenate(
        [jnp.where(r_idx == pm[e:e + 1, :] - bases[e], 1.0, 0.0) for e in range(N_EXPERTS)], axis=0).astype(BF16)
    xw = jnp.dot(onehot, hb, preferred_element_type=F32)

    def window_copy(e, first_row):
        start = pl.multiple_of(first_row, BF16_TILE_ROWS)
        return pltpu.make_async_copy(stage_ref.at[e], xe_ref.at[e, pl.ds(start, DISPATCH_WIN), :], sem_ref.at[e])

    for e in range(N_EXPERTS):
        @pl.when(b > 0)
        def _():
            window_copy(e, 0).wait()

        rows = xw[e * DISPATCH_WIN:(e + 1) * DISPATCH_WIN, :]
        head = jnp.where(head_row < los[e] - bases[e], carry_ref[e].astype(F32), rows[:BF16_TILE_ROWS, :])
        stage_ref[e, :BF16_TILE_ROWS, :] = head.astype(BF16)
        stage_ref[e, BF16_TILE_ROWS:, :] = rows[BF16_TILE_ROWS:, :].astype(BF16)
        window_copy(e, bases[e]).start()
        end = los[e] + _block_count(lo_ref, e, b, rpe, n_blk, cap)
        n_win = lax.div(end - bases[e] + (DISPATCH_WIN - 1), DISPATCH_WIN)

        def extra_window(k, carry):
            window_copy(e, 0).wait()
            oh = jnp.where(r_idx + k * DISPATCH_WIN == pm[e:e + 1, :] - bases[e], 1.0, 0.0).astype(BF16)
            stage_ref[e] = jnp.dot(oh, hb, preferred_element_type=F32).astype(BF16)
            window_copy(e, bases[e] + k * DISPATCH_WIN).start()
            return carry

        lax.fori_loop(1, n_win, extra_window, 0)
        last_start = bases[e] + (jnp.maximum(n_win, 1) - 1) * DISPATCH_WIN
        tile = jnp.minimum(_floor_tile(end) - last_start, DISPATCH_WIN - BF16_TILE_ROWS)
        carry_ref[e] = stage_ref[e, pl.ds(pl.multiple_of(tile, BF16_TILE_ROWS), BF16_TILE_ROWS), :]

    @pl.when(b == n_blk - 1)
    def _():
        for e in range(N_EXPERTS):
            window_copy(e, 0).wait()
            stage_ref[e] = jnp.zeros((DISPATCH_WIN, D_MODEL), BF16)
            window_copy(e, cap).start()
        for e in range(N_EXPERTS):
            window_copy(e, cap).wait()


def _dispatch(lo, posm, h2, cap):
    t = h2.shape[0]
    rpe = t // LANES
    n_blk = t // TOK_BLOCK
    grid_spec = pltpu.PrefetchScalarGridSpec(
        num_scalar_prefetch=1,
        grid=(n_blk,),
        in_specs=[pl.BlockSpec((N_EXPERTS, TOK_BLOCK), lambda b, lo_r: (0, b)),
                  pl.BlockSpec((TOK_BLOCK, D_MODEL), lambda b, lo_r: (b, 0))],
        out_specs=pl.BlockSpec(memory_space=pl.ANY),
        scratch_shapes=[pltpu.VMEM((N_EXPERTS, DISPATCH_WIN, D_MODEL), BF16),
                        pltpu.VMEM((N_EXPERTS, BF16_TILE_ROWS, D_MODEL), BF16),
                        pltpu.SemaphoreType.DMA((N_EXPERTS,))],
    )
    return pl.pallas_call(
        functools.partial(_dispatch_body, rpe=rpe, n_blk=n_blk, cap=cap),
        grid_spec=grid_spec,
        out_shape=jax.ShapeDtypeStruct((N_EXPERTS, cap + DISPATCH_WIN, D_MODEL), BF16),
        compiler_params=_params(("arbitrary",), 32),
        name="dispatch",
    )(lo, posm, h2)


def _ffn_body(x_ref, wg_ref, wu_ref, wd_ref, y_ref, acc_ref, *, n_f, n_sub, sub):
    f = pl.program_id(1)

    @pl.when(f == 0)
    def _():
        acc_ref[...] = jnp.zeros_like(acc_ref)

    wg = wg_ref[0].astype(BF16)
    wu = wu_ref[0].astype(BF16)
    wd = wd_ref[0].astype(BF16)

    def rows_step(i, carry):
        r = pl.ds(pl.multiple_of(i * sub, sub), sub)
        x = x_ref[0, r, :]
        a = jnp.dot(x, wg, preferred_element_type=F32)
        u = jnp.dot(x, wu, preferred_element_type=F32)
        h = (jax.nn.silu(a) * u).astype(BF16)
        acc_ref[r, :] += jnp.dot(h, wd, preferred_element_type=F32)
        return carry

    lax.fori_loop(0, n_sub, rows_step, 0)

    @pl.when(f == n_f - 1)
    def _():
        y_ref[0] = acc_ref[...].astype(BF16)


def _ffn(xe, w_gate, w_up, w_down, cap, tf=512, sub=512):
    sub = min(sub, cap)
    n_f = EXPERT_FF // tf
    return pl.pallas_call(
        functools.partial(_ffn_body, n_f=n_f, n_sub=cap // sub, sub=sub),
        grid=(N_EXPERTS, n_f),
        in_specs=[pl.BlockSpec((1, cap, D_MODEL), lambda e, f: (e, 0, 0)),
                  pl.BlockSpec((1, D_MODEL, tf), lambda e, f: (e, 0, f)),
                  pl.BlockSpec((1, D_MODEL, tf), lambda e, f: (e, 0, f)),
                  pl.BlockSpec((1, tf, D_MODEL), lambda e, f: (e, f, 0))],
        out_specs=pl.BlockSpec((1, cap, D_MODEL), lambda e, f: (e, 0, 0)),
        out_shape=jax.ShapeDtypeStruct((N_EXPERTS, cap, D_MODEL), BF16),
        scratch_shapes=[pltpu.VMEM((cap, D_MODEL), F32)],
        compiler_params=_params(("arbitrary", "arbitrary"), 56),
        name="expert_ffn",
    )(xe, w_gate, w_up, w_down)


def _combine_body(lo_ref, pm_ref, gm_ref, x1_ref, nf_ref, y_ref, out_ref,
                  a_ref, m_ref, ovw_ref, sem_ref, ovsem_ref, *, rpe, n_blk, cap):
    WIN = COMBINE_WIN
    b = pl.program_id(0)
    slot = b % 2

    def win_start(e, blk):
        return jnp.minimum(_floor_tile(_block_lo(lo_ref, e, blk, rpe)), cap - WIN)

    def window_copy(e, blk, sl):
        start = pl.multiple_of(win_start(e, blk), BF16_TILE_ROWS)
        return pltpu.make_async_copy(y_ref.at[e, pl.ds(start, WIN), :], a_ref.at[sl, pl.ds(e * WIN, WIN), :],
                                     sem_ref.at[sl, e])

    @pl.when(b == 0)
    def _():
        for e in range(N_EXPERTS):
            window_copy(e, 0, 0).start()

    @pl.when(b + 1 < n_blk)
    def _():
        for e in range(N_EXPERTS):
            window_copy(e, b + 1, 1 - slot).start()

    pm = pm_ref[...]
    gm = gm_ref[...]
    lane = lax.broadcasted_iota(I32, (TOK_BLOCK, LANES), 1)
    low = lane < WIN
    pieces = []
    for p in range(N_EXPERTS // 2):
        e0, e1 = 2 * p, 2 * p + 1
        tgt = jnp.where(low, pm[:, e0:e0 + 1] - win_start(e0, b), pm[:, e1:e1 + 1] - win_start(e1, b) + WIN)
        gv = jnp.where(low, gm[:, e0:e0 + 1], gm[:, e1:e1 + 1])
        pieces.append(jnp.where(lane == tgt, gv, 0.0).astype(BF16))
    gmat = jnp.concatenate(pieces, axis=1)

    for e in range(N_EXPERTS):
        window_copy(e, b, slot).wait()
    m_ref[...] = jnp.dot(gmat, a_ref[slot], preferred_element_type=F32)

    lane_w = lax.broadcasted_iota(I32, (TOK_BLOCK, WIN), 1)
    for e in range(N_EXPERTS):
        ws0 = win_start(e, b)
        end = _block_lo(lo_ref, e, b, rpe) + _block_count(lo_ref, e, b, rpe, n_blk, cap)
        n_win = lax.div(jnp.maximum(end - ws0, 0) + (WIN - 1), WIN)

        def extra_window(k, carry):
            first = ws0 + k * WIN
            ws = jnp.minimum(first, cap - WIN)
            cp = pltpu.make_async_copy(y_ref.at[e, pl.ds(pl.multiple_of(ws, BF16_TILE_ROWS), WIN), :], ovw_ref,
                                       ovsem_ref)
            cp.start()
            cp.wait()
            pe = pm[:, e:e + 1]
            gk = jnp.where((lane_w == pe - ws) & (pe >= first), gm[:, e:e + 1], 0.0).astype(BF16)
            m_ref[...] += jnp.dot(gk, ovw_ref[...], preferred_element_type=F32)
            return carry

        lax.fori_loop(1, n_win, extra_window, 0)

    xo = x1_ref[...] + m_ref[...]
    out_ref[...] = (xo * lax.rsqrt(jnp.mean(xo * xo, axis=-1, keepdims=True) + RMS_EPS)) * nf_ref[...]


def _combine(lo, posm_t, gate_t, x1, norm_final, y2, cap):
    t = x1.shape[0]
    rpe = t // LANES
    n_blk = t // TOK_BLOCK
    grid_spec = pltpu.PrefetchScalarGridSpec(
        num_scalar_prefetch=1,
        grid=(n_blk,),
        in_specs=[pl.BlockSpec((TOK_BLOCK, N_EXPERTS), lambda b, lo_r: (b, 0)),
                  pl.BlockSpec((TOK_BLOCK, N_EXPERTS), lambda b, lo_r: (b, 0)),
                  pl.BlockSpec((TOK_BLOCK, D_MODEL), lambda b, lo_r: (b, 0)),
                  pl.BlockSpec((1, D_MODEL), lambda b, lo_r: (0, 0)),
                  pl.BlockSpec(memory_space=pl.ANY)],
        out_specs=pl.BlockSpec((TOK_BLOCK, D_MODEL), lambda b, lo_r: (b, 0)),
        scratch_shapes=[pltpu.VMEM((2, N_EXPERTS * COMBINE_WIN, D_MODEL), BF16),
                        pltpu.VMEM((TOK_BLOCK, D_MODEL), F32),
                        pltpu.VMEM((COMBINE_WIN, D_MODEL), BF16),
                        pltpu.SemaphoreType.DMA((2, N_EXPERTS)),
                        pltpu.SemaphoreType.DMA(())],
    )
    return pl.pallas_call(
        functools.partial(_combine_body, rpe=rpe, n_blk=n_blk, cap=cap),
        grid_spec=grid_spec,
        out_shape=jax.ShapeDtypeStruct((t, D_MODEL), F32),
        compiler_params=_params(("arbitrary",), 40),
        name="combine",
    )(lo, posm_t, gate_t, x1, norm_final, y2)


def _trunk(x, p):
    b, l, _ = x.shape
    t = b * l
    n_seq = l // S5_CHUNK
    cap = max(1, EC_CAPACITY_FACTOR * t // N_EXPERTS)
    rpe = t // LANES

    hg, u, u_sm, gates = _inproj(x.reshape(t, D_MODEL), p['norm_mix'], p['w_in'])
    o_fw, o_bw = _hgrn(hg.reshape(b, l, 5 * A_WIDTH), p['lb'])
    ys5 = _s5(u_sm, p['s5_perm'], *p['s5'], n_seq)

    x1, h2, logits = _merge(x.reshape(t, D_MODEL), o_fw.reshape(t, A_WIDTH), o_bw.reshape(t, A_WIDTH), hg, ys5, u,
                            gates, p['hgrn_norm'], p['s5_d'], p['w_glu'], p['b_glu'], p['w_a'], p['w_b'],
                            p['w_out'], p['norm_ffn'], p['w_router'])

    posm, gate, lo = _select(logits.T.reshape(N_EXPERTS * rpe, LANES), rpe, cap)
    lo = lo[:, 0]
    posm = posm.reshape(N_EXPERTS, t)
    xe = _dispatch(lo, posm, h2, cap)
    y2 = _ffn(xe, p['w_gate'], p['w_up'], p['w_down'], cap)
    out = _combine(lo, posm.T, gate.reshape(N_EXPERTS, t).T, x1, p['norm_final'], y2, cap)
    return out.reshape(b, l, D_MODEL)


def kernel(x_prompt, x_sample, norm_mix, w_in, hgrn_gamma, hgrn_norm, s5_a_re, s5_a_im, s5_log_dt, s5_b_re, s5_b_im, s5_c_re, s5_c_im, s5_d, s5_w_glu, s5_b_glu, w_branch_a, w_branch_b, w_out, norm_ffn, w_router, w_exp_gate, w_exp_up, w_exp_down, norm_final):
    assert norm_mix.shape[0] == 1, "single-layer trunk"
    row = lambda a: a.astype(F32).reshape(1, -1)
    lb = jnp.cumsum(jax.nn.softmax(hgrn_gamma.astype(F32), axis=0), axis=0)[0]
    max_chunks = max(x_prompt.shape[1], x_sample.shape[1]) // S5_CHUNK
    s5_ops = _s5_operators(s5_a_re[0], s5_a_im[0], s5_log_dt[0], s5_b_re[0], s5_b_im[0], s5_c_re[0], s5_c_im[0],
                           max_chunks.bit_length() - 1)
    p = {
        'norm_mix': row(norm_mix[0]), 'w_in': w_in[0].astype(BF16), 'lb': lb, 's5': s5_ops,
        's5_perm': _s5_lane_permutation(),
        'hgrn_norm': row(hgrn_norm[0]), 's5_d': row(s5_d[0]), 'w_glu': s5_w_glu[0].astype(BF16),
        'b_glu': row(s5_b_glu[0]), 'w_a': w_branch_a[0].astype(BF16), 'w_b': w_branch_b[0].astype(BF16),
        'w_out': w_out[0].astype(BF16), 'norm_ffn': row(norm_ffn[0]),
        'w_router': jnp.pad(w_router[0].astype(F32), ((0, 0), (0, LANES - N_EXPERTS))),
        'w_gate': w_exp_gate[0], 'w_up': w_exp_up[0], 'w_down': w_exp_down[0],
        'norm_final': row(norm_final),
    }
    return (_trunk(x_prompt, p), _trunk(x_sample, p))
```

```python
import functools
import math

import jax
import jax.numpy as jnp
from jax import lax
from jax.experimental import pallas as pl
from jax.experimental.pallas import tpu as pltpu

F32 = jnp.float32
BF16 = jnp.bfloat16
I32 = jnp.int32

D_MODEL = 1024
A_HEADS = 4
A_HEAD_DIM = 128
A_WIDTH = A_HEADS * A_HEAD_DIM
A_CHUNK = 64
S5_WIDTH = 512
S5_GROUP = 16
S5_GROUPS = S5_WIDTH // S5_GROUP
S5_STATE = 64
N_EXPERTS = 16
EXPERT_FF = 2048
EC_CAPACITY_FACTOR = 2
RMS_EPS = 1e-6
IN_COLS = 5 * A_WIDTH + S5_WIDTH + 2 * D_MODEL

LANES = 128
BF16_TILE_ROWS = 16
S5_CHUNK = 16
S5_COLS = S5_CHUNK * S5_GROUP
S5_GROUPS_PER_STEP = LANES // S5_GROUP
TOK_BLOCK = 256
COMBINE_WIN = 64
DISPATCH_WIN = COMBINE_WIN + BF16_TILE_ROWS
MIB = 2 ** 20


def _params(semantics, vmem_mib):
    return pltpu.CompilerParams(dimension_semantics=semantics, vmem_limit_bytes=vmem_mib * MIB)


def _inproj_body(x_ref, g_ref, w_ref, hg_ref, u_ref, us_ref, gt_ref, uf_ref, *, tm):
    x = x_ref[...]
    ms = jnp.mean(x * x, axis=-1, keepdims=True)
    h = ((x * lax.rsqrt(ms + RMS_EPS)) * g_ref[...]).astype(BF16)
    n_hg = 5 * A_WIDTH // 512
    for c in range(IN_COLS // 512):
        r = jnp.dot(h, w_ref[:, c * 512:(c + 1) * 512], preferred_element_type=F32)
        if c < n_hg:
            hg_ref[:, c * 512:(c + 1) * 512] = r.astype(BF16)
        elif c == n_hg:
            u_ref[...] = r.astype(BF16)
            for cb in range(S5_WIDTH // LANES):
                uf_ref[cb] = r[:, cb * LANES:(cb + 1) * LANES]
                for sg in range(S5_CHUNK):
                    us_ref[sg, :, cb * LANES:(cb + 1) * LANES] = uf_ref[
                        cb, pl.ds(sg, tm // S5_CHUNK, stride=S5_CHUNK), :].astype(BF16)
        else:
            gt_ref[:, (c - n_hg - 1) * 512:(c - n_hg) * 512] = r.astype(BF16)


def _inproj(x, gain, w_bf16, tm=512):
    t = x.shape[0]
    return pl.pallas_call(
        functools.partial(_inproj_body, tm=tm),
        grid=(t // tm,),
        in_specs=[pl.BlockSpec((tm, D_MODEL), lambda i: (i, 0)),
                  pl.BlockSpec((1, D_MODEL), lambda i: (0, 0)),
                  pl.BlockSpec((D_MODEL, IN_COLS), lambda i: (0, 0))],
        out_specs=[pl.BlockSpec((tm, 5 * A_WIDTH), lambda i: (i, 0)),
                   pl.BlockSpec((tm, S5_WIDTH), lambda i: (i, 0)),
                   pl.BlockSpec((S5_CHUNK, tm // S5_CHUNK, S5_WIDTH), lambda i: (0, i, 0)),
                   pl.BlockSpec((tm, 2 * D_MODEL), lambda i: (i, 0))],
        out_shape=[jax.ShapeDtypeStruct((t, 5 * A_WIDTH), BF16),
                   jax.ShapeDtypeStruct((t, S5_WIDTH), BF16),
                   jax.ShapeDtypeStruct((S5_CHUNK, t // S5_CHUNK, S5_WIDTH), BF16),
                   jax.ShapeDtypeStruct((t, 2 * D_MODEL), BF16)],
        scratch_shapes=[pltpu.VMEM((S5_WIDTH // LANES, tm, LANES), F32)],
        compiler_params=_params(("arbitrary",), 48),
        name="inproj",
    )(x, gain, w_bf16)


def _hgrn_pair(q, fr, v, lb, s_ref, forward):
    c, c2 = A_CHUNK, 2 * A_CHUNK
    t_idx = lax.broadcasted_iota(I32, (c2, c2), 0)
    s_idx = lax.broadcasted_iota(I32, (c2, c2), 1)
    one = lambda cond: jnp.where(cond, 1.0, 0.0)
    s_loc = s_idx & (c - 1)
    same_chunk = (t_idx < c) == (s_idx < c)
    tri = same_chunk & ((s_idx <= t_idx) if forward else (s_idx >= t_idx))
    m_run = one(tri)
    m_mid = one(same_chunk & ((s_loc <= c // 2 - 1) if forward else (s_loc >= c // 2)))
    m_all = one(same_chunk)
    order = (0, 1) if forward else (1, 0)

    q = q.astype(F32)
    f = lb + (1.0 - lb) * jax.nn.sigmoid(fr.astype(F32))
    lf = jnp.log2(f)
    kk = 1.0 - f
    hi = lf.astype(BF16)
    hl = jnp.concatenate([hi, (lf - hi.astype(F32)).astype(BF16)], axis=0)

    def sums(m):
        mb = m.astype(BF16)
        return jnp.dot(jnp.concatenate([mb, mb], axis=1), hl, preferred_element_type=F32)

    q_in = (q * jnp.exp2(sums(m_run - m_mid))).astype(BF16)
    k_in = kk * jnp.exp2(sums(m_mid - m_run))
    qdec = (q * jnp.exp2(sums(m_run))).astype(BF16)
    kdec = kk * jnp.exp2(sums(m_all - m_run))
    total = [sums(one(jnp.broadcast_to((s_idx < c) == (ch == 0), (c2, c2)))) for ch in (0, 1)]
    zero = jnp.zeros((), BF16)
    outs = []
    for h in range(A_HEADS):
        sl = slice(h * A_HEAD_DIM, (h + 1) * A_HEAD_DIM)
        vh = v[:, sl]
        k_t = k_in[:, sl].T.astype(BF16)
        kd_t = kdec[:, sl].T.astype(BF16)
        s = jnp.where(tri, jnp.dot(q_in[:, sl], k_t, preferred_element_type=F32), 0.0).astype(BF16)
        o_intra = jnp.dot(s, vh, preferred_element_type=F32)
        state = s_ref[h]
        o_inter = [None, None]
        for ch in order:
            rows = slice(ch * c, (ch + 1) * c)
            o_inter[ch] = jnp.dot(qdec[rows, sl], state.astype(BF16), preferred_element_type=F32)
            own_rows = (t_idx < c) if ch == 0 else (t_idx >= c)
            v_ch = jnp.where(own_rows, vh, zero)
            decay = jnp.exp2(total[ch][:, sl].T)
            state = decay * state + jnp.dot(kd_t, v_ch, preferred_element_type=F32)
        s_ref[h] = state
        outs.append(o_intra + jnp.concatenate(o_inter, axis=0))
    return jnp.concatenate(outs, axis=1).astype(BF16)


def _hgrn_body(qf_ref, ff_ref, vf_ref, qb_ref, fb_ref, vb_ref, lb_ref, of_ref, ob_ref, sf_ref, sb_ref,
               *, n_pairs):
    @pl.when(pl.program_id(1) == 0)
    def _():
        sf_ref[...] = jnp.zeros_like(sf_ref)
        sb_ref[...] = jnp.zeros_like(sb_ref)

    c2 = 2 * A_CHUNK

    def step(j, carry):
        rf = pl.ds(pl.multiple_of(j * c2, c2), c2)
        of_ref[0, rf, :] = _hgrn_pair(qf_ref[0, rf, :], ff_ref[0, rf, :], vf_ref[0, rf, :],
                                      lb_ref[0:1, :], sf_ref, True)
        rb = pl.ds(pl.multiple_of((n_pairs - 1 - j) * c2, c2), c2)
        ob_ref[0, rb, :] = _hgrn_pair(qb_ref[0, rb, :], fb_ref[0, rb, :], vb_ref[0, rb, :],
                                      lb_ref[1:2, :], sb_ref, False)
        return carry

    lax.fori_loop(0, n_pairs, step, 0, unroll=True)


def _hgrn(hg, lb, lb_tokens=512):
    b, l, _ = hg.shape
    nb = l // lb_tokens
    blk = (1, lb_tokens, A_WIDTH)
    fwd = lambda col: pl.BlockSpec(blk, lambda bi, i: (bi, i, col))
    bwd = lambda col: pl.BlockSpec(blk, lambda bi, i: (bi, nb - 1 - i, col))
    return pl.pallas_call(
        functools.partial(_hgrn_body, n_pairs=lb_tokens // (2 * A_CHUNK)),
        grid=(b, nb),
        in_specs=[fwd(0), fwd(1), fwd(3), bwd(0), bwd(2), bwd(3),
                  pl.BlockSpec((2, A_WIDTH), lambda bi, i: (0, 0))],
        out_specs=[pl.BlockSpec(blk, lambda bi, i: (bi, i, 0)),
                   pl.BlockSpec(blk, lambda bi, i: (bi, nb - 1 - i, 0))],
        out_shape=[jax.ShapeDtypeStruct((b, l, A_WIDTH), BF16)] * 2,
        scratch_shapes=[pltpu.VMEM((A_HEADS, A_HEAD_DIM, A_HEAD_DIM), F32)] * 2,
        compiler_params=_params(("arbitrary", "arbitrary"), 32),
        name="hgrn2",
    )(hg, hg, hg, hg, hg, hg, lb)


def _s5_direction_terms(a_re, a_im, log_dt, b_re, b_im):
    a_re, a_im = a_re.astype(F32), a_im.astype(F32)
    b_re, b_im = b_re.astype(F32), b_im.astype(F32)
    dt = jnp.exp(log_dt.astype(F32))[:, None]
    j = jnp.arange(S5_CHUNK + 1, dtype=F32)[:, None, None]
    mag = jnp.exp(j * (dt * a_re))
    pw_re = mag * jnp.cos(j * (dt * a_im))
    pw_im = mag * jnp.sin(j * (dt * a_im))
    nr, ni = pw_re[1] - 1.0, pw_im[1]
    den = a_re * a_re + a_im * a_im
    f_re = (nr * a_re + ni * a_im) / den
    f_im = (ni * a_re - nr * a_im) / den
    bb_re = f_re[..., None] * b_re - f_im[..., None] * b_im
    bb_im = f_re[..., None] * b_im + f_im[..., None] * b_re
    pb_re = pw_re[..., None] * bb_re - pw_im[..., None] * bb_im
    pb_im = pw_re[..., None] * bb_im + pw_im[..., None] * bb_re
    return pw_re, pw_im, pb_re, pb_im


def _s5_operators(a_re, a_im, log_dt, b_re, b_im, c_re, c_im, n_steps):
    hp = lax.Precision.HIGHEST
    c_re, c_im = c_re.astype(F32), c_im.astype(F32)
    g = S5_GROUPS
    terms = [_s5_direction_terms(a_re[d], a_im[d], log_dt[d], b_re, b_im) for d in range(2)]
    taps = []
    for pw_re, pw_im, pb_re, pb_im in terms:
        k = (jnp.einsum('gop,jgpi->jgoi', c_re, pb_re[:S5_CHUNK], precision=hp)
             - jnp.einsum('gop,jgpi->jgoi', c_im, pb_im[:S5_CHUNK], precision=hp))
        taps.append(k)
    sig = jnp.arange(S5_CHUNK)[:, None]
    tau = jnp.arange(S5_CHUNK)[None, :]
    d = tau - sig
    kf = jnp.where((d >= 0)[:, :, None, None, None], taps[0][jnp.clip(d, 0, S5_CHUNK - 1)], 0.0)
    kb = jnp.where((d <= 0)[:, :, None, None, None], taps[1][jnp.clip(-d, 0, S5_CHUNK - 1)], 0.0)
    w_intra = (kf + kb).transpose(2, 0, 4, 1, 3).reshape(g, S5_COLS, S5_COLS)

    def w_in(pb_re, pb_im, order):
        re = pb_re[order].transpose(1, 0, 3, 2)
        im = pb_im[order].transpose(1, 0, 3, 2)
        return jnp.concatenate([re, im], axis=-1).reshape(g, S5_COLS, 2 * S5_STATE)

    def w_out(pw_re, pw_im, order):
        pr, pi = pw_re[order], pw_im[order]
        m_re = c_re[None] * pr[:, :, None, :] - c_im[None] * pi[:, :, None, :]
        m_im = c_re[None] * pi[:, :, None, :] + c_im[None] * pr[:, :, None, :]
        re = m_re.transpose(1, 3, 0, 2).reshape(g, S5_STATE, S5_COLS)
        im = -m_im.transpose(1, 3, 0, 2).reshape(g, S5_STATE, S5_COLS)
        return jnp.concatenate([re, im], axis=1)

    asc = jnp.arange(S5_CHUNK)
    w_in_f = w_in(terms[0][2], terms[0][3], S5_CHUNK - 1 - asc)
    w_in_b = w_in(terms[1][2], terms[1][3], asc)
    w_out_f = w_out(terms[0][0], terms[0][1], asc + 1)
    w_out_b = w_out(terms[1][0], terms[1][1], S5_CHUNK - asc)
    w1 = jnp.concatenate([w_intra, w_in_f, w_in_b], axis=-1).astype(BF16)
    w2 = jnp.concatenate([w_out_f, w_out_b], axis=1).astype(BF16)

    ars, ais = [], []
    cur = [(t[0][S5_CHUNK], t[1][S5_CHUNK]) for t in terms]
    for _ in range(n_steps):
        ars.append(jnp.concatenate([cur[0][0], cur[0][0], cur[1][0], cur[1][0]], axis=-1))
        ais.append(jnp.concatenate([-cur[0][1], cur[0][1], -cur[1][1], cur[1][1]], axis=-1))
        cur = [(r * r - i * i, 2.0 * r * i) for r, i in cur]
    return w1, w2, jnp.stack(ars, axis=1), jnp.stack(ais, axis=1)


def _s5_group(u, w1, w2, ar, ai, n, n_seq, n_steps):
    rows = u.shape[0]
    half = 2 * S5_STATE
    yv = jnp.dot(u, w1, preferred_element_type=F32)
    zf = jnp.where(n >= 1, pltpu.roll(yv[:, S5_COLS:S5_COLS + half], 1, axis=0), 0.0)
    zb = jnp.where(n < n_seq - 1, pltpu.roll(yv[:, S5_COLS + half:], rows - 1, axis=0), 0.0)
    for k in range(n_steps):
        s = 1 << k
        sf = jnp.where(n >= s, pltpu.roll(zf, s, axis=0), 0.0)
        sb = jnp.where(n < n_seq - s, pltpu.roll(zb, rows - s, axis=0), 0.0)
        zf = zf + ar[k:k + 1, :half] * sf + ai[k:k + 1, :half] * pltpu.roll(sf, S5_STATE, axis=1)
        zb = zb + ar[k:k + 1, half:] * sb + ai[k:k + 1, half:] * pltpu.roll(sb, S5_STATE, axis=1)
    x = jnp.concatenate([zf, zb], axis=1).astype(BF16)
    return yv[:, :S5_COLS] + jnp.dot(x, w2, preferred_element_type=F32)


def _s5_body(u_ref, perm_ref, w1_ref, w2_ref, ar_ref, ai_ref, y_ref, z_ref, *, n_seq, n_steps):
    gps = S5_GROUPS_PER_STEP
    rows = u_ref.shape[1]
    n = lax.broadcasted_iota(I32, (rows, 2 * S5_STATE), 0) & (n_seq - 1)
    perm = perm_ref[...]
    for hf in range(S5_CHUNK // gps):
        x = jnp.concatenate([u_ref[hf * gps + j] for j in range(gps)], axis=1)
        z_ref[:, hf * gps * LANES:(hf + 1) * gps * LANES] = jnp.dot(
            x, perm, preferred_element_type=F32).astype(BF16)
    for g in range(gps):
        u = jnp.concatenate([z_ref[:, (hf * gps + g) * LANES:(hf * gps + g + 1) * LANES]
                             for hf in range(S5_CHUNK // gps)], axis=1)
        y = _s5_group(u, w1_ref[g], w2_ref[g], ar_ref[g], ai_ref[g], n, n_seq, n_steps).astype(BF16)
        for hf in range(S5_CHUNK // gps):
            z_ref[:, (hf * gps + g) * LANES:(hf * gps + g + 1) * LANES] = y[:, hf * LANES:(hf + 1) * LANES]
    for hf in range(S5_CHUNK // gps):
        x = jnp.dot(z_ref[:, hf * gps * LANES:(hf + 1) * gps * LANES], perm,
                    preferred_element_type=F32).astype(BF16)
        for j in range(gps):
            y_ref[hf * gps + j] = x[:, j * LANES:(j + 1) * LANES]


def _s5_lane_permutation():
    idx = jnp.arange(S5_GROUPS_PER_STEP * LANES)
    a, b, c = idx // LANES, (idx % LANES) // S5_GROUP, idx % S5_GROUP
    dst = b * LANES + a * S5_GROUP + c
    return (dst[:, None] == idx[None, :]).astype(BF16)


def _s5(us, perm, w1, w2, ar, ai, n_seq):
    _, rows, _ = us.shape
    n_steps = n_seq.bit_length() - 1
    gps = S5_GROUPS_PER_STEP
    tok = pl.BlockSpec((S5_CHUNK, rows, LANES), lambda i: (0, 0, i))
    per_g = lambda a: pl.BlockSpec((gps,) + a.shape[1:], lambda i: (i, 0, 0))
    return pl.pallas_call(
        functools.partial(_s5_body, n_seq=n_seq, n_steps=n_steps),
        grid=(S5_GROUPS // gps,),
        in_specs=[tok, pl.BlockSpec(perm.shape, lambda i: (0, 0)), per_g(w1), per_g(w2), per_g(ar), per_g(ai)],
        out_specs=tok,
        out_shape=jax.ShapeDtypeStruct(us.shape, BF16),
        scratch_shapes=[pltpu.VMEM((rows, S5_CHUNK * LANES), BF16)],
        compiler_params=_params(("arbitrary",), 48),
        name="s5",
    )(us, perm, w1, w2, ar, ai)


def _gelu_tanh(x):
    return 0.5 * x * (1.0 + jnp.tanh(math.sqrt(2.0 / math.pi) * (x + 0.044715 * (x * x * x))))


def _merge_body(x_ref, of_ref, ob_ref, go_ref, ys_ref, u_ref, ga_ref, gb_ref,
                hn_ref, sd_ref, wglu_ref, bglu_ref, wa_ref, wb_ref, wo_ref, nf_ref, wr_ref,
                x1_ref, h2_ref, lg_ref, yn_ref, *, tm):
    for sg in range(S5_CHUNK):
        for cb in range(S5_WIDTH // LANES):
            yn_ref[cb, pl.ds(sg, tm // S5_CHUNK, stride=S5_CHUNK), :] = ys_ref[
                sg, :, cb * LANES:(cb + 1) * LANES].astype(F32)
    o = of_ref[...].astype(F32) + ob_ref[...].astype(F32)
    parts = []
    for h in range(A_HEADS):
        oh = o[:, h * A_HEAD_DIM:(h + 1) * A_HEAD_DIM]
        parts.append(oh * lax.rsqrt(jnp.mean(oh * oh, axis=-1, keepdims=True) + RMS_EPS))
    h_a = jnp.concatenate(parts, axis=1) * hn_ref[...] * jax.nn.silu(go_ref[...].astype(F32))

    u = u_ref[...].astype(F32)
    y_s5 = jnp.concatenate([yn_ref[cb] for cb in range(S5_WIDTH // LANES)], axis=1)
    ys = _gelu_tanh(y_s5 + sd_ref[...] * u)
    z = jnp.dot(ys.astype(BF16), wglu_ref[...], preferred_element_type=F32) + bglu_ref[...]
    h_b = ys * jax.nn.sigmoid(z)

    y_a = jnp.dot(h_a.astype(BF16), wa_ref[...], preferred_element_type=F32)
    y_b = jnp.dot(h_b.astype(BF16), wb_ref[...], preferred_element_type=F32)
    merged = jax.nn.sigmoid(ga_ref[...].astype(F32)) * y_a + jax.nn.sigmoid(gb_ref[...].astype(F32)) * y_b
    x1 = x_ref[...] + jnp.dot(merged.astype(BF16), wo_ref[...], preferred_element_type=F32)
    x1_ref[...] = x1

    h2 = (x1 * lax.rsqrt(jnp.mean(x1 * x1, axis=-1, keepdims=True) + RMS_EPS)) * nf_ref[...]
    h2_hi = h2.astype(BF16)
    h2_ref[...] = h2_hi
    h2_lo = (h2 - h2_hi.astype(F32)).astype(BF16)
    wr = wr_ref[...]
    wr_hi = wr.astype(BF16)
    wr_lo = (wr - wr_hi.astype(F32)).astype(BF16)
    lg = (jnp.dot(h2_hi, wr_hi, preferred_element_type=F32) + jnp.dot(h2_lo, wr_hi, preferred_element_type=F32)
          + jnp.dot(h2_hi, wr_lo, preferred_element_type=F32))
    lg_ref[...] = lg[:, :N_EXPERTS]


def _merge(x, o_fw, o_bw, hg, ys5, u, gates, hn, sd, wglu, bglu, wa, wb, wo, nf, wr_pad, tm=512):
    t = x.shape[0]
    row = lambda w, col=0: pl.BlockSpec((tm, w), lambda i: (i, col))
    full = lambda a: pl.BlockSpec(a.shape, lambda i: (0,) * a.ndim)
    return pl.pallas_call(
        functools.partial(_merge_body, tm=tm),
        grid=(t // tm,),
        in_specs=[row(D_MODEL), row(A_WIDTH), row(A_WIDTH), row(A_WIDTH, 4),
                  pl.BlockSpec((S5_CHUNK, tm // S5_CHUNK, S5_WIDTH), lambda i: (0, i, 0)), row(S5_WIDTH),
                  row(D_MODEL, 0), row(D_MODEL, 1),
                  full(hn), full(sd), full(wglu), full(bglu), full(wa), full(wb), full(wo), full(nf), full(wr_pad)],
        out_specs=[row(D_MODEL), row(D_MODEL), row(N_EXPERTS)],
        out_shape=[jax.ShapeDtypeStruct((t, D_MODEL), F32),
                   jax.ShapeDtypeStruct((t, D_MODEL), BF16),
                   jax.ShapeDtypeStruct((t, N_EXPERTS), F32)],
        scratch_shapes=[pltpu.VMEM((S5_WIDTH // LANES, tm, LANES), F32)],
        compiler_params=_params(("arbitrary",), 48),
        name="merge",
    )(x, o_fw, o_bw, hg, ys5, u, gates, gates, hn, sd, wglu, bglu, wa, wb, wo, nf, wr_pad)


def _exclusive_prefix(m, upper, strict_lower):
    mb = m.astype(BF16)
    incl = jnp.dot(mb, upper, preferred_element_type=F32)
    row_off = jnp.sum(jnp.dot(strict_lower, mb, preferred_element_type=F32), axis=1, keepdims=True)
    return incl - m + row_off, row_off


def _select_body(lg_ref, pos_ref, gate_ref, lo_ref, aff_ref, *, rpe, cap):
    e_n = N_EXPERTS
    slab = lambda ref, e: ref[e * rpe:(e + 1) * rpe, :]
    mx = slab(lg_ref, 0)
    for e in range(1, e_n):
        mx = jnp.maximum(mx, slab(lg_ref, e))
    den = jnp.zeros_like(mx)
    for e in range(e_n):
        ex = jnp.exp(slab(lg_ref, e) - mx)
        aff_ref[e * rpe:(e + 1) * rpe, :] = ex
        den = den + ex
    for e in range(e_n):
        aff_ref[e * rpe:(e + 1) * rpe, :] = slab(aff_ref, e) / den

    capf = float(cap)

    def bit_step(i, cand):
        bit = jnp.left_shift(jnp.int32(1), 30 - i)
        rows = []
        for e in range(e_n):
            cur = cand[e:e + 1, :]
            trial = cur | bit
            cnt = jnp.sum(jnp.where(slab(aff_ref, e) >= pltpu.bitcast(trial, F32), 1.0, 0.0), keepdims=True)
            rows.append(jnp.where(cnt >= capf, trial, cur))
        return jnp.concatenate(rows, axis=0)

    thr = pltpu.bitcast(lax.fori_loop(0, 31, bit_step, jnp.zeros((e_n, LANES), I32)), F32)

    upper = jnp.where(lax.broadcasted_iota(I32, (LANES, LANES), 0) <= lax.broadcasted_iota(I32, (LANES, LANES), 1),
                      1.0, 0.0).astype(BF16)
    strict_lower = jnp.where(lax.broadcasted_iota(I32, (rpe, rpe), 1) < lax.broadcasted_iota(I32, (rpe, rpe), 0),
                             1.0, 0.0).astype(BF16)
    for e in range(e_n):
        aff = slab(aff_ref, e)
        t = thr[e:e + 1, :]
        gt = jnp.where(aff > t, 1.0, 0.0)
        eq = jnp.where(aff == t, 1.0, 0.0)
        need = capf - jnp.sum(gt, keepdims=True)
        rank_eq, _ = _exclusive_prefix(eq, upper, strict_lower)
        sel = gt + eq * jnp.where(rank_eq < need, 1.0, 0.0)
        pos, row_off = _exclusive_prefix(sel, upper, strict_lower)
        chosen = sel > 0.5
        pos_ref[e * rpe:(e + 1) * rpe, :] = jnp.where(chosen, pos, -1.0).astype(I32)
        gate_ref[e * rpe:(e + 1) * rpe, :] = jnp.where(chosen, aff, 0.0)
        lo_ref[e * rpe:(e + 1) * rpe, :] = jnp.broadcast_to(row_off, (rpe, LANES)).astype(I32)


def _select(lg_em, rpe, cap):
    rows = N_EXPERTS * rpe
    shp = lambda dt: jax.ShapeDtypeStruct((rows, LANES), dt)
    return pl.pallas_call(
        functools.partial(_select_body, rpe=rpe, cap=cap),
        out_shape=[shp(I32), shp(F32), shp(I32)],
        scratch_shapes=[pltpu.VMEM((rows, LANES), F32)],
        compiler_params=pltpu.CompilerParams(vmem_limit_bytes=32 * MIB),
        name="select",
    )(lg_em)


def _block_lo(lo_ref, e, blk, rpe):
    return lo_ref[e * rpe + blk * (TOK_BLOCK // LANES)]


def _block_count(lo_ref, e, blk, rpe, n_blk, cap):
    nxt = lo_ref[e * rpe + jnp.minimum((blk + 1) * (TOK_BLOCK // LANES), rpe - 1)]
    return jnp.where(blk == n_blk - 1, cap, nxt) - _block_lo(lo_ref, e, blk, rpe)


def _floor_tile(row):
    return row - (row & (BF16_TILE_ROWS - 1))


def _dispatch_body(lo_ref, pm_ref, h_ref, xe_ref, stage_ref, carry_ref, sem_ref, *, rpe, n_blk, cap):
    b = pl.program_id(0)

    @pl.when(b == 0)
    def _():
        carry_ref[...] = jnp.zeros_like(carry_ref)

    pm = pm_ref[...]
    hb = h_ref[...]
    r_idx = lax.broadcasted_iota(I32, (DISPATCH_WIN, TOK_BLOCK), 0)
    head_row = lax.broadcasted_iota(I32, (BF16_TILE_ROWS, D_MODEL), 0)
    los = [_block_lo(lo_ref, e, b, rpe) for e in range(N_EXPERTS)]
    bases = [_floor_tile(lo) for lo in los]
    onehot = jnp.concatenate(
        [jnp.where(r_idx == pm[e:e + 1, :] - bases[e], 1.0, 0.0) for e in range(N_EXPERTS)], axis=0).astype(BF16)
    xw = jnp.dot(onehot, hb, preferred_element_type=F32)

    def window_copy(e, first_row):
        start = pl.multiple_of(first_row, BF16_TILE_ROWS)
        return pltpu.make_async_copy(stage_ref.at[e], xe_ref.at[e, pl.ds(start, DISPATCH_WIN), :], sem_ref.at[e])

    for e in range(N_EXPERTS):
        @pl.when(b > 0)
        def _():
            window_copy(e, 0).wait()

        rows = xw[e * DISPATCH_WIN:(e + 1) * DISPATCH_WIN, :]
        head = jnp.where(head_row < los[e] - bases[e], carry_ref[e].astype(F32), rows[:BF16_TILE_ROWS, :])
        stage_ref[e, :BF16_TILE_ROWS, :] = head.astype(BF16)
        stage_ref[e, BF16_TILE_ROWS:, :] = rows[BF16_TILE_ROWS:, :].astype(BF16)
        window_copy(e, bases[e]).start()

    ends = [los[e] + _block_count(lo_ref, e, b, rpe, n_blk, cap) for e in range(N_EXPERTS)]
    n_wins = [lax.div(ends[e] - bases[e] + (DISPATCH_WIN - 1), DISPATCH_WIN) for e in range(N_EXPERTS)]

    @pl.when(functools.reduce(jnp.maximum, n_wins) > 1)
    def _():
        for e in range(N_EXPERTS):
            def extra_window(k, carry):
                window_copy(e, 0).wait()
                oh = jnp.where(r_idx + k * DISPATCH_WIN == pm[e:e + 1, :] - bases[e], 1.0, 0.0).astype(BF16)
                stage_ref[e] = jnp.dot(oh, hb, preferred_element_type=F32).astype(BF16)
                window_copy(e, bases[e] + k * DISPATCH_WIN).start()
                return carry

            lax.fori_loop(1, n_wins[e], extra_window, 0)

    for e in range(N_EXPERTS):
        last_start = bases[e] + (jnp.maximum(n_wins[e], 1) - 1) * DISPATCH_WIN
        tile = jnp.minimum(_floor_tile(ends[e]) - last_start, DISPATCH_WIN - BF16_TILE_ROWS)
        carry_ref[e] = stage_ref[e, pl.ds(pl.multiple_of(tile, BF16_TILE_ROWS), BF16_TILE_ROWS), :]

    @pl.when(b == n_blk - 1)
    def _():
        for e in range(N_EXPERTS):
            window_copy(e, 0).wait()
            stage_ref[e] = jnp.zeros((DISPATCH_WIN, D_MODEL), BF16)
            window_copy(e, cap).start()
        for e in range(N_EXPERTS):
            window_copy(e, cap).wait()


def _dispatch(lo, posm, h2, cap):
    t = h2.shape[0]
    rpe = t // LANES
    n_blk = t // TOK_BLOCK
    grid_spec = pltpu.PrefetchScalarGridSpec(
        num_scalar_prefetch=1,
        grid=(n_blk,),
        in_specs=[pl.BlockSpec((N_EXPERTS, TOK_BLOCK), lambda b, lo_r: (0, b)),
                  pl.BlockSpec((TOK_BLOCK, D_MODEL), lambda b, lo_r: (b, 0))],
        out_specs=pl.BlockSpec(memory_space=pl.ANY),
        scratch_shapes=[pltpu.VMEM((N_EXPERTS, DISPATCH_WIN, D_MODEL), BF16),
                        pltpu.VMEM((N_EXPERTS, BF16_TILE_ROWS, D_MODEL), BF16),
                        pltpu.SemaphoreType.DMA((N_EXPERTS,))],
    )
    return pl.pallas_call(
        functools.partial(_dispatch_body, rpe=rpe, n_blk=n_blk, cap=cap),
        grid_spec=grid_spec,
        out_shape=jax.ShapeDtypeStruct((N_EXPERTS, cap + DISPATCH_WIN, D_MODEL), BF16),
        compiler_params=_params(("arbitrary",), 32),
        name="dispatch",
    )(lo, posm, h2)


def _ffn_body(x_ref, wg_ref, wu_ref, wd_ref, y_ref, acc_ref, *, n_f, n_sub, sub):
    f = pl.program_id(1)

    @pl.when(f == 0)
    def _():
        acc_ref[...] = jnp.zeros_like(acc_ref)

    wg = wg_ref[0].astype(BF16)
    wu = wu_ref[0].astype(BF16)
    wd = wd_ref[0].astype(BF16)

    def rows_step(i, carry):
        r = pl.ds(pl.multiple_of(i * sub, sub), sub)
        x = x_ref[0, r, :]
        a = jnp.dot(x, wg, preferred_element_type=F32)
        u = jnp.dot(x, wu, preferred_element_type=F32)
        h = (jax.nn.silu(a) * u).astype(BF16)
        acc_ref[r, :] += jnp.dot(h, wd, preferred_element_type=F32)
        return carry

    lax.fori_loop(0, n_sub, rows_step, 0)

    @pl.when(f == n_f - 1)
    def _():
        y_ref[0] = acc_ref[...].astype(BF16)


def _ffn(xe, w_gate, w_up, w_down, cap, tf=512, sub=512):
    sub = min(sub, cap)
    n_f = EXPERT_FF // tf
    return pl.pallas_call(
        functools.partial(_ffn_body, n_f=n_f, n_sub=cap // sub, sub=sub),
        grid=(N_EXPERTS, n_f),
        in_specs=[pl.BlockSpec((1, cap, D_MODEL), lambda e, f: (e, 0, 0)),
                  pl.BlockSpec((1, D_MODEL, tf), lambda e, f: (e, 0, f)),
                  pl.BlockSpec((1, D_MODEL, tf), lambda e, f: (e, 0, f)),
                  pl.BlockSpec((1, tf, D_MODEL), lambda e, f: (e, f, 0))],
        out_specs=pl.BlockSpec((1, cap, D_MODEL), lambda e, f: (e, 0, 0)),
        out_shape=jax.ShapeDtypeStruct((N_EXPERTS, cap, D_MODEL), BF16),
        scratch_shapes=[pltpu.VMEM((cap, D_MODEL), F32)],
        compiler_params=_params(("arbitrary", "arbitrary"), 56),
        name="expert_ffn",
    )(xe, w_gate, w_up, w_down)


def _combine_body(lo_ref, pm_ref, gm_ref, x1_ref, nf_ref, y_ref, out_ref,
                  a_ref, m_ref, ovw_ref, sem_ref, ovsem_ref, *, rpe, n_blk, cap):
    WIN = COMBINE_WIN
    b = pl.program_id(0)
    slot = b % 2

    def win_start(e, blk):
        return jnp.minimum(_floor_tile(_block_lo(lo_ref, e, blk, rpe)), cap - WIN)

    def window_copy(e, blk, sl):
        start = pl.multiple_of(win_start(e, blk), BF16_TILE_ROWS)
        return pltpu.make_async_copy(y_ref.at[e, pl.ds(start, WIN), :], a_ref.at[sl, pl.ds(e * WIN, WIN), :],
                                     sem_ref.at[sl, e])

    @pl.when(b == 0)
    def _():
        for e in range(N_EXPERTS):
            window_copy(e, 0, 0).start()

    @pl.when(b + 1 < n_blk)
    def _():
        for e in range(N_EXPERTS):
            window_copy(e, b + 1, 1 - slot).start()

    pm = pm_ref[...]
    gm = gm_ref[...]
    ws0 = [win_start(e, b) for e in range(N_EXPERTS)]
    e_col = lax.broadcasted_iota(I32, (1, N_EXPERTS), 1)
    ws_row = jnp.zeros((1, N_EXPERTS), I32)
    for e in range(N_EXPERTS):
        ws_row = jnp.where(e_col == e, ws0[e], ws_row)
    rel = jnp.clip(pm - ws_row, -1, WIN).astype(F32).astype(BF16)
    spread = jnp.where(lax.broadcasted_iota(I32, (N_EXPERTS, N_EXPERTS * WIN), 1) // WIN
                       == lax.broadcasted_iota(I32, (N_EXPERTS, N_EXPERTS * WIN), 0), 1.0, 0.0).astype(BF16)
    tgt = jnp.dot(rel, spread, preferred_element_type=F32)
    gv = jnp.dot(gm.astype(BF16), spread, preferred_element_type=F32)
    lane_r = (lax.broadcasted_iota(I32, (TOK_BLOCK, N_EXPERTS * WIN), 1) & (WIN - 1)).astype(F32)
    gmat = jnp.where(tgt == lane_r, gv, 0.0).astype(BF16)

    for e in range(N_EXPERTS):
        window_copy(e, b, slot).wait()
    m_ref[...] = jnp.dot(gmat, a_ref[slot], preferred_element_type=F32)

    ends = [_block_lo(lo_ref, e, b, rpe) + _block_count(lo_ref, e, b, rpe, n_blk, cap) for e in range(N_EXPERTS)]
    n_wins = [lax.div(jnp.maximum(ends[e] - ws0[e], 0) + (WIN - 1), WIN) for e in range(N_EXPERTS)]

    @pl.when(functools.reduce(jnp.maximum, n_wins) > 1)
    def _():
        lane_w = lax.broadcasted_iota(I32, (TOK_BLOCK, WIN), 1)
        for e in range(N_EXPERTS):
            def extra_window(k, carry):
                first = ws0[e] + k * WIN
                ws = jnp.minimum(first, cap - WIN)
                cp = pltpu.make_async_copy(y_ref.at[e, pl.ds(pl.multiple_of(ws, BF16_TILE_ROWS), WIN), :], ovw_ref,
                                           ovsem_ref)
                cp.start()
                cp.wait()
                pe = pm[:, e:e + 1]
                gk = jnp.where((lane_w == pe - ws) & (pe >= first), gm[:, e:e + 1], 0.0).astype(BF16)
                m_ref[...] += jnp.dot(gk, ovw_ref[...], preferred_element_type=F32)
                return carry

            lax.fori_loop(1, n_wins[e], extra_window, 0)

    xo = x1_ref[...] + m_ref[...]
    out_ref[...] = (xo * lax.rsqrt(jnp.mean(xo * xo, axis=-1, keepdims=True) + RMS_EPS)) * nf_ref[...]


def _combine(lo, posm_t, gate_t, x1, norm_final, y2, cap):
    t = x1.shape[0]
    rpe = t // LANES
    n_blk = t // TOK_BLOCK
    grid_spec = pltpu.PrefetchScalarGridSpec(
        num_scalar_prefetch=1,
        grid=(n_blk,),
        in_specs=[pl.BlockSpec((TOK_BLOCK, N_EXPERTS), lambda b, lo_r: (b, 0)),
                  pl.BlockSpec((TOK_BLOCK, N_EXPERTS), lambda b, lo_r: (b, 0)),
                  pl.BlockSpec((TOK_BLOCK, D_MODEL), lambda b, lo_r: (b, 0)),
                  pl.BlockSpec((1, D_MODEL), lambda b, lo_r: (0, 0)),
                  pl.BlockSpec(memory_space=pl.ANY)],
        out_specs=pl.BlockSpec((TOK_BLOCK, D_MODEL), lambda b, lo_r: (b, 0)),
        scratch_shapes=[pltpu.VMEM((2, N_EXPERTS * COMBINE_WIN, D_MODEL), BF16),
                        pltpu.VMEM((TOK_BLOCK, D_MODEL), F32),
                        pltpu.VMEM((COMBINE_WIN, D_MODEL), BF16),
                        pltpu.SemaphoreType.DMA((2, N_EXPERTS)),
                        pltpu.SemaphoreType.DMA(())],
    )
    return pl.pallas_call(
        functools.partial(_combine_body, rpe=rpe, n_blk=n_blk, cap=cap),
        grid_spec=grid_spec,
        out_shape=jax.ShapeDtypeStruct((t, D_MODEL), F32),
        compiler_params=_params(("arbitrary",), 40),
        name="combine",
    )(lo, posm_t, gate_t, x1, norm_final, y2)


def _trunk(x, p):
    b, l, _ = x.shape
    t = b * l
    n_seq = l // S5_CHUNK
    cap = max(1, EC_CAPACITY_FACTOR * t // N_EXPERTS)
    rpe = t // LANES

    hg, u, u_sm, gates = _inproj(x.reshape(t, D_MODEL), p['norm_mix'], p['w_in'])
    o_fw, o_bw = _hgrn(hg.reshape(b, l, 5 * A_WIDTH), p['lb'])
    ys5 = _s5(u_sm, p['s5_perm'], *p['s5'], n_seq)

    x1, h2, logits = _merge(x.reshape(t, D_MODEL), o_fw.reshape(t, A_WIDTH), o_bw.reshape(t, A_WIDTH), hg, ys5, u,
                            gates, p['hgrn_norm'], p['s5_d'], p['w_glu'], p['b_glu'], p['w_a'], p['w_b'],
                            p['w_out'], p['norm_ffn'], p['w_router'])

    posm, gate, lo = _select(logits.T.reshape(N_EXPERTS * rpe, LANES), rpe, cap)
    lo = lo[:, 0]
    posm = posm.reshape(N_EXPERTS, t)
    xe = _dispatch(lo, posm, h2, cap)
    y2 = _ffn(xe, p['w_gate'], p['w_up'], p['w_down'], cap)
    out = _combine(lo, posm.T, gate.reshape(N_EXPERTS, t).T, x1, p['norm_final'], y2, cap)
    return out.reshape(b, l, D_MODEL)


def kernel(x_prompt, x_sample, norm_mix, w_in, hgrn_gamma, hgrn_norm, s5_a_re, s5_a_im, s5_log_dt, s5_b_re, s5_b_im, s5_c_re, s5_c_im, s5_d, s5_w_glu, s5_b_glu, w_branch_a, w_branch_b, w_out, norm_ffn, w_router, w_exp_gate, w_exp_up, w_exp_down, norm_final):
    assert norm_mix.shape[0] == 1, "single-layer trunk"
    row = lambda a: a.astype(F32).reshape(1, -1)
    lb = jnp.cumsum(jax.nn.softmax(hgrn_gamma.astype(F32), axis=0), axis=0)[0]
    max_chunks = max(x_prompt.shape[1], x_sample.shape[1]) // S5_CHUNK
    s5_ops = _s5_operators(s5_a_re[0], s5_a_im[0], s5_log_dt[0], s5_b_re[0], s5_b_im[0], s5_c_re[0], s5_c_im[0],
                           max_chunks.bit_length() - 1)
    p = {
        'norm_mix': row(norm_mix[0]), 'w_in': w_in[0].astype(BF16), 'lb': lb, 's5': s5_ops,
        's5_perm': _s5_lane_permutation(),
        'hgrn_norm': row(hgrn_norm[0]), 's5_d': row(s5_d[0]), 'w_glu': s5_w_glu[0].astype(BF16),
        'b_glu': row(s5_b_glu[0]), 'w_a': w_branch_a[0].astype(BF16), 'w_b': w_branch_b[0].astype(BF16),
        'w_out': w_out[0].astype(BF16), 'norm_ffn': row(norm_ffn[0]),
        'w_router': jnp.pad(w_router[0].astype(F32), ((0, 0), (0, LANES - N_EXPERTS))),
        'w_gate': w_exp_gate[0], 'w_up': w_exp_up[0], 'w_down': w_exp_down[0],
        'norm_final': row(norm_final),
    }
    return (_trunk(x_prompt, p), _trunk(x_sample, p))
```

```python
import functools
import math

import numpy as np
import jax
import jax.numpy as jnp
from jax import lax
from jax.experimental import pallas as pl
from jax.experimental.pallas import tpu as pltpu

F32 = jnp.float32
BF16 = jnp.bfloat16
I32 = jnp.int32

D_MODEL = 1024
A_HEADS = 4
A_HEAD_DIM = 128
A_WIDTH = A_HEADS * A_HEAD_DIM
A_CHUNK = 64
S5_WIDTH = 512
S5_GROUP = 16
S5_GROUPS = S5_WIDTH // S5_GROUP
S5_STATE = 64
N_EXPERTS = 16
EXPERT_FF = 2048
EC_CAPACITY_FACTOR = 2
RMS_EPS = 1e-6
IN_COLS = 5 * A_WIDTH + S5_WIDTH + 2 * D_MODEL

LANES = 128
BF16_TILE_ROWS = 16
S5_CHUNK = 16
S5_COLS = S5_CHUNK * S5_GROUP
S5_GROUPS_PER_STEP = LANES // S5_GROUP
TOK_BLOCK = 256
COMBINE_WIN = 64
DISPATCH_WIN = COMBINE_WIN + BF16_TILE_ROWS
MIB = 2 ** 20


def _params(semantics, vmem_mib):
    return pltpu.CompilerParams(dimension_semantics=semantics, vmem_limit_bytes=vmem_mib * MIB)


def _inproj_body(x_ref, g_ref, w_ref, hg_ref, u_ref, us_ref, gt_ref, uf_ref, *, tm):
    x = x_ref[...]
    ms = jnp.mean(x * x, axis=-1, keepdims=True)
    h = ((x * lax.rsqrt(ms + RMS_EPS)) * g_ref[...]).astype(BF16)
    n_hg = 5 * A_WIDTH // 512
    for c in range(IN_COLS // 512):
        r = jnp.dot(h, w_ref[:, c * 512:(c + 1) * 512], preferred_element_type=F32)
        if c < n_hg:
            hg_ref[:, c * 512:(c + 1) * 512] = r.astype(BF16)
        elif c == n_hg:
            u_ref[...] = r.astype(BF16)
            for cb in range(S5_WIDTH // LANES):
                uf_ref[cb] = r[:, cb * LANES:(cb + 1) * LANES]
                for sg in range(S5_CHUNK):
                    us_ref[sg, :, cb * LANES:(cb + 1) * LANES] = uf_ref[
                        cb, pl.ds(sg, tm // S5_CHUNK, stride=S5_CHUNK), :].astype(BF16)
        else:
            gt_ref[:, (c - n_hg - 1) * 512:(c - n_hg) * 512] = r.astype(BF16)


def _inproj(x, gain, w_bf16, tm=512):
    t = x.shape[0]
    return pl.pallas_call(
        functools.partial(_inproj_body, tm=tm),
        grid=(t // tm,),
        in_specs=[pl.BlockSpec((tm, D_MODEL), lambda i: (i, 0)),
                  pl.BlockSpec((1, D_MODEL), lambda i: (0, 0)),
                  pl.BlockSpec((D_MODEL, IN_COLS), lambda i: (0, 0))],
        out_specs=[pl.BlockSpec((tm, 5 * A_WIDTH), lambda i: (i, 0)),
                   pl.BlockSpec((tm, S5_WIDTH), lambda i: (i, 0)),
                   pl.BlockSpec((S5_CHUNK, tm // S5_CHUNK, S5_WIDTH), lambda i: (0, i, 0)),
                   pl.BlockSpec((tm, 2 * D_MODEL), lambda i: (i, 0))],
        out_shape=[jax.ShapeDtypeStruct((t, 5 * A_WIDTH), BF16),
                   jax.ShapeDtypeStruct((t, S5_WIDTH), BF16),
                   jax.ShapeDtypeStruct((S5_CHUNK, t // S5_CHUNK, S5_WIDTH), BF16),
                   jax.ShapeDtypeStruct((t, 2 * D_MODEL), BF16)],
        scratch_shapes=[pltpu.VMEM((S5_WIDTH // LANES, tm, LANES), F32)],
        compiler_params=_params(("arbitrary",), 48),
        name="inproj",
    )(x, gain, w_bf16)


def _hgrn_pair(q, fr, v, lb, s_ref, forward):
    c, c2 = A_CHUNK, 2 * A_CHUNK
    t_idx = lax.broadcasted_iota(I32, (c2, c2), 0)
    s_idx = lax.broadcasted_iota(I32, (c2, c2), 1)
    one = lambda cond: jnp.where(cond, 1.0, 0.0)
    s_loc = s_idx & (c - 1)
    same_chunk = (t_idx < c) == (s_idx < c)
    tri = same_chunk & ((s_idx <= t_idx) if forward else (s_idx >= t_idx))
    m_run = one(tri)
    m_mid = one(same_chunk & ((s_loc <= c // 2 - 1) if forward else (s_loc >= c // 2)))
    m_all = one(same_chunk)
    order = (0, 1) if forward else (1, 0)

    q = q.astype(F32)
    f = lb + (1.0 - lb) * jax.nn.sigmoid(fr.astype(F32))
    lf = jnp.log2(f)
    kk = 1.0 - f
    hi = lf.astype(BF16)
    hl = jnp.concatenate([hi, (lf - hi.astype(F32)).astype(BF16)], axis=0)

    def sums(m):
        mb = m.astype(BF16)
        return jnp.dot(jnp.concatenate([mb, mb], axis=1), hl, preferred_element_type=F32)

    q_in = (q * jnp.exp2(sums(m_run - m_mid))).astype(BF16)
    k_in = kk * jnp.exp2(sums(m_mid - m_run))
    qdec = (q * jnp.exp2(sums(m_run))).astype(BF16)
    kdec = kk * jnp.exp2(sums(m_all - m_run))
    total = [sums(one(jnp.broadcast_to((s_idx < c) == (ch == 0), (c2, c2)))) for ch in (0, 1)]
    zero = jnp.zeros((), BF16)
    outs = []
    for h in range(A_HEADS):
        sl = slice(h * A_HEAD_DIM, (h + 1) * A_HEAD_DIM)
        vh = v[:, sl]
        k_t = k_in[:, sl].T.astype(BF16)
        kd_t = kdec[:, sl].T.astype(BF16)
        s = jnp.where(tri, jnp.dot(q_in[:, sl], k_t, preferred_element_type=F32), 0.0).astype(BF16)
        o_intra = jnp.dot(s, vh, preferred_element_type=F32)
        state = s_ref[h]
        o_inter = [None, None]
        for ch in order:
            rows = slice(ch * c, (ch + 1) * c)
            o_inter[ch] = jnp.dot(qdec[rows, sl], state.astype(BF16), preferred_element_type=F32)
            own_rows = (t_idx < c) if ch == 0 else (t_idx >= c)
            v_ch = jnp.where(own_rows, vh, zero)
            decay = jnp.exp2(total[ch][:, sl].T)
            state = decay * state + jnp.dot(kd_t, v_ch, preferred_element_type=F32)
        s_ref[h] = state
        outs.append(o_intra + jnp.concatenate(o_inter, axis=0))
    return jnp.concatenate(outs, axis=1).astype(BF16)


def _hgrn_body(qf_ref, ff_ref, vf_ref, qb_ref, fb_ref, vb_ref, lb_ref, of_ref, ob_ref, sf_ref, sb_ref,
               *, n_pairs):
    @pl.when(pl.program_id(1) == 0)
    def _():
        sf_ref[...] = jnp.zeros_like(sf_ref)
        sb_ref[...] = jnp.zeros_like(sb_ref)

    c2 = 2 * A_CHUNK

    def step(j, carry):
        rf = pl.ds(pl.multiple_of(j * c2, c2), c2)
        of_ref[0, rf, :] = _hgrn_pair(qf_ref[0, rf, :], ff_ref[0, rf, :], vf_ref[0, rf, :],
                                      lb_ref[0:1, :], sf_ref, True)
        rb = pl.ds(pl.multiple_of((n_pairs - 1 - j) * c2, c2), c2)
        ob_ref[0, rb, :] = _hgrn_pair(qb_ref[0, rb, :], fb_ref[0, rb, :], vb_ref[0, rb, :],
                                      lb_ref[1:2, :], sb_ref, False)
        return carry

    lax.fori_loop(0, n_pairs, step, 0, unroll=True)


def _hgrn(hg, lb, lb_tokens=512):
    b, l, _ = hg.shape
    nb = l // lb_tokens
    blk = (1, lb_tokens, A_WIDTH)
    fwd = lambda col: pl.BlockSpec(blk, lambda bi, i: (bi, i, col))
    bwd = lambda col: pl.BlockSpec(blk, lambda bi, i: (bi, nb - 1 - i, col))
    return pl.pallas_call(
        functools.partial(_hgrn_body, n_pairs=lb_tokens // (2 * A_CHUNK)),
        grid=(b, nb),
        in_specs=[fwd(0), fwd(1), fwd(3), bwd(0), bwd(2), bwd(3),
                  pl.BlockSpec((2, A_WIDTH), lambda bi, i: (0, 0))],
        out_specs=[pl.BlockSpec(blk, lambda bi, i: (bi, i, 0)),
                   pl.BlockSpec(blk, lambda bi, i: (bi, nb - 1 - i, 0))],
        out_shape=[jax.ShapeDtypeStruct((b, l, A_WIDTH), BF16)] * 2,
        scratch_shapes=[pltpu.VMEM((A_HEADS, A_HEAD_DIM, A_HEAD_DIM), F32)] * 2,
        compiler_params=_params(("arbitrary", "arbitrary"), 32),
        name="hgrn2",
    )(hg, hg, hg, hg, hg, hg, lb)


def _s5_direction_terms(a_re, a_im, log_dt, b_re, b_im):
    a_re, a_im = a_re.astype(F32), a_im.astype(F32)
    b_re, b_im = b_re.astype(F32), b_im.astype(F32)
    dt = jnp.exp(log_dt.astype(F32))[:, None]
    j = jnp.arange(S5_CHUNK + 1, dtype=F32)[:, None, None]
    mag = jnp.exp(j * (dt * a_re))
    pw_re = mag * jnp.cos(j * (dt * a_im))
    pw_im = mag * jnp.sin(j * (dt * a_im))
    nr, ni = pw_re[1] - 1.0, pw_im[1]
    den = a_re * a_re + a_im * a_im
    f_re = (nr * a_re + ni * a_im) / den
    f_im = (ni * a_re - nr * a_im) / den
    bb_re = f_re[..., None] * b_re - f_im[..., None] * b_im
    bb_im = f_re[..., None] * b_im + f_im[..., None] * b_re
    pb_re = pw_re[..., None] * bb_re - pw_im[..., None] * bb_im
    pb_im = pw_re[..., None] * bb_im + pw_im[..., None] * bb_re
    return pw_re, pw_im, pb_re, pb_im


def _s5_operators(a_re, a_im, log_dt, b_re, b_im, c_re, c_im, n_steps):
    hp = lax.Precision.HIGHEST
    c_re, c_im = c_re.astype(F32), c_im.astype(F32)
    g = S5_GROUPS
    terms = [_s5_direction_terms(a_re[d], a_im[d], log_dt[d], b_re, b_im) for d in range(2)]
    taps = []
    for pw_re, pw_im, pb_re, pb_im in terms:
        k = (jnp.einsum('gop,jgpi->jgoi', c_re, pb_re[:S5_CHUNK], precision=hp)
             - jnp.einsum('gop,jgpi->jgoi', c_im, pb_im[:S5_CHUNK], precision=hp))
        taps.append(k)
    lag = np.arange(S5_CHUNK)
    place_f = (lag[None, None, :] - lag[None, :, None] == lag[:, None, None]).astype(np.float32)
    place_b = place_f.transpose(0, 2, 1)
    w_intra = (jnp.einsum('jst,jgoi->gsito', place_f, taps[0], precision=hp)
               + jnp.einsum('jst,jgoi->gsito', place_b, taps[1], precision=hp)
               ).reshape(g, S5_COLS, S5_COLS)

    def w_in(pb_re, pb_im, order):
        re = pb_re[order].transpose(1, 0, 3, 2)
        im = pb_im[order].transpose(1, 0, 3, 2)
        return jnp.concatenate([re, im], axis=-1).reshape(g, S5_COLS, 2 * S5_STATE)

    def w_out(pw_re, pw_im, order):
        pr, pi = pw_re[order], pw_im[order]
        m_re = c_re[None] * pr[:, :, None, :] - c_im[None] * pi[:, :, None, :]
        m_im = c_re[None] * pi[:, :, None, :] + c_im[None] * pr[:, :, None, :]
        re = m_re.transpose(1, 3, 0, 2).reshape(g, S5_STATE, S5_COLS)
        im = -m_im.transpose(1, 3, 0, 2).reshape(g, S5_STATE, S5_COLS)
        return jnp.concatenate([re, im], axis=1)

    asc = jnp.arange(S5_CHUNK)
    w_in_f = w_in(terms[0][2], terms[0][3], S5_CHUNK - 1 - asc)
    w_in_b = w_in(terms[1][2], terms[1][3], asc)
    w_out_f = w_out(terms[0][0], terms[0][1], asc + 1)
    w_out_b = w_out(terms[1][0], terms[1][1], S5_CHUNK - asc)
    w1 = jnp.concatenate([w_intra, w_in_f, w_in_b], axis=-1).astype(BF16)
    w2 = jnp.concatenate([w_out_f, w_out_b], axis=1).astype(BF16)

    ars, ais = [], []
    cur = [(t[0][S5_CHUNK], t[1][S5_CHUNK]) for t in terms]
    for _ in range(n_steps):
        ars.append(jnp.concatenate([cur[0][0], cur[0][0], cur[1][0], cur[1][0]], axis=-1))
        ais.append(jnp.concatenate([-cur[0][1], cur[0][1], -cur[1][1], cur[1][1]], axis=-1))
        cur = [(r * r - i * i, 2.0 * r * i) for r, i in cur]
    return w1, w2, jnp.stack(ars, axis=1), jnp.stack(ais, axis=1)


def _s5_group(u, w1, w2, ar, ai, n, n_seq, n_steps):
    rows = u.shape[0]
    half = 2 * S5_STATE
    yv = jnp.dot(u, w1, preferred_element_type=F32)
    zf = jnp.where(n >= 1, pltpu.roll(yv[:, S5_COLS:S5_COLS + half], 1, axis=0), 0.0)
    zb = jnp.where(n < n_seq - 1, pltpu.roll(yv[:, S5_COLS + half:], rows - 1, axis=0), 0.0)
    for k in range(n_steps):
        s = 1 << k
        sf = jnp.where(n >= s, pltpu.roll(zf, s, axis=0), 0.0)
        sb = jnp.where(n < n_seq - s, pltpu.roll(zb, rows - s, axis=0), 0.0)
        zf = zf + ar[k:k + 1, :half] * sf + ai[k:k + 1, :half] * pltpu.roll(sf, S5_STATE, axis=1)
        zb = zb + ar[k:k + 1, half:] * sb + ai[k:k + 1, half:] * pltpu.roll(sb, S5_STATE, axis=1)
    x = jnp.concatenate([zf, zb], axis=1).astype(BF16)
    return yv[:, :S5_COLS] + jnp.dot(x, w2, preferred_element_type=F32)


def _s5_body(u_ref, perm_ref, w1_ref, w2_ref, ar_ref, ai_ref, y_ref, z_ref, *, n_seq, n_steps):
    gps = S5_GROUPS_PER_STEP
    rows = u_ref.shape[1]
    n = lax.broadcasted_iota(I32, (rows, 2 * S5_STATE), 0) & (n_seq - 1)
    perm = perm_ref[...]
    for hf in range(S5_CHUNK // gps):
        x = jnp.concatenate([u_ref[hf * gps + j] for j in range(gps)], axis=1)
        z_ref[:, hf * gps * LANES:(hf + 1) * gps * LANES] = jnp.dot(
            x, perm, preferred_element_type=F32).astype(BF16)
    for g in range(gps):
        u = jnp.concatenate([z_ref[:, (hf * gps + g) * LANES:(hf * gps + g + 1) * LANES]
                             for hf in range(S5_CHUNK // gps)], axis=1)
        y = _s5_group(u, w1_ref[g], w2_ref[g], ar_ref[g], ai_ref[g], n, n_seq, n_steps).astype(BF16)
        for hf in range(S5_CHUNK // gps):
            z_ref[:, (hf * gps + g) * LANES:(hf * gps + g + 1) * LANES] = y[:, hf * LANES:(hf + 1) * LANES]
    for hf in range(S5_CHUNK // gps):
        x = jnp.dot(z_ref[:, hf * gps * LANES:(hf + 1) * gps * LANES], perm,
                    preferred_element_type=F32).astype(BF16)
        for j in range(gps):
            y_ref[hf * gps + j] = x[:, j * LANES:(j + 1) * LANES]


def _s5_lane_permutation():
    idx = jnp.arange(S5_GROUPS_PER_STEP * LANES)
    a, b, c = idx // LANES, (idx % LANES) // S5_GROUP, idx % S5_GROUP
    dst = b * LANES + a * S5_GROUP + c
    return (dst[:, None] == idx[None, :]).astype(BF16)


def _s5(us, perm, w1, w2, ar, ai, n_seq):
    _, rows, _ = us.shape
    assert n_seq & (n_seq - 1) == 0, n_seq
    n_steps = n_seq.bit_length() - 1
    gps = S5_GROUPS_PER_STEP
    tok = pl.BlockSpec((S5_CHUNK, rows, LANES), lambda i: (0, 0, i))
    per_g = lambda a: pl.BlockSpec((gps,) + a.shape[1:], lambda i: (i, 0, 0))
    return pl.pallas_call(
        functools.partial(_s5_body, n_seq=n_seq, n_steps=n_steps),
        grid=(S5_GROUPS // gps,),
        in_specs=[tok, pl.BlockSpec(perm.shape, lambda i: (0, 0)), per_g(w1), per_g(w2), per_g(ar), per_g(ai)],
        out_specs=tok,
        out_shape=jax.ShapeDtypeStruct(us.shape, BF16),
        scratch_shapes=[pltpu.VMEM((rows, S5_CHUNK * LANES), BF16)],
        compiler_params=_params(("arbitrary",), 48),
        name="s5",
    )(us, perm, w1, w2, ar, ai)


def _gelu_tanh(x):
    return 0.5 * x * (1.0 + jnp.tanh(math.sqrt(2.0 / math.pi) * (x + 0.044715 * (x * x * x))))


def _merge_body(x_ref, of_ref, ob_ref, go_ref, ys_ref, u_ref, ga_ref, gb_ref,
                hn_ref, sd_ref, wglu_ref, bglu_ref, wa_ref, wb_ref, wo_ref, nf_ref, wr_ref,
                x1_ref, h2_ref, lg_ref, yn_ref, *, tm):
    for sg in range(S5_CHUNK):
        for cb in range(S5_WIDTH // LANES):
            yn_ref[cb, pl.ds(sg, tm // S5_CHUNK, stride=S5_CHUNK), :] = ys_ref[
                sg, :, cb * LANES:(cb + 1) * LANES].astype(F32)
    o = of_ref[...].astype(F32) + ob_ref[...].astype(F32)
    parts = []
    for h in range(A_HEADS):
        oh = o[:, h * A_HEAD_DIM:(h + 1) * A_HEAD_DIM]
        parts.append(oh * lax.rsqrt(jnp.mean(oh * oh, axis=-1, keepdims=True) + RMS_EPS))
    h_a = jnp.concatenate(parts, axis=1) * hn_ref[...] * jax.nn.silu(go_ref[...].astype(F32))

    u = u_ref[...].astype(F32)
    y_s5 = jnp.concatenate([yn_ref[cb] for cb in range(S5_WIDTH // LANES)], axis=1)
    ys = _gelu_tanh(y_s5 + sd_ref[...] * u)
    z = jnp.dot(ys.astype(BF16), wglu_ref[...], preferred_element_type=F32) + bglu_ref[...]
    h_b = ys * jax.nn.sigmoid(z)

    y_a = jnp.dot(h_a.astype(BF16), wa_ref[...], preferred_element_type=F32)
    y_b = jnp.dot(h_b.astype(BF16), wb_ref[...], preferred_element_type=F32)
    merged = jax.nn.sigmoid(ga_ref[...].astype(F32)) * y_a + jax.nn.sigmoid(gb_ref[...].astype(F32)) * y_b
    x1 = x_ref[...] + jnp.dot(merged.astype(BF16), wo_ref[...], preferred_element_type=F32)
    x1_ref[...] = x1

    h2 = (x1 * lax.rsqrt(jnp.mean(x1 * x1, axis=-1, keepdims=True) + RMS_EPS)) * nf_ref[...]
    h2_hi = h2.astype(BF16)
    h2_ref[...] = h2_hi
    h2_lo = (h2 - h2_hi.astype(F32)).astype(BF16)
    r_hi = jnp.dot(h2_hi, wr_ref[...], preferred_element_type=F32)
    r_lo = jnp.dot(h2_lo, wr_ref[...], preferred_element_type=F32)
    lg_ref[...] = r_hi[:, :N_EXPERTS] + r_hi[:, N_EXPERTS:2 * N_EXPERTS] + r_lo[:, :N_EXPERTS]


def _split_router(w_router):
    w = w_router.astype(F32)
    hi = w.astype(BF16)
    lo = (w - hi.astype(F32)).astype(BF16)
    pad = jnp.zeros((w.shape[0], LANES - 2 * N_EXPERTS), BF16)
    return jnp.concatenate([hi, lo, pad], axis=1)


def _merge(x, o_fw, o_bw, hg, ys5, u, gates, hn, sd, wglu, bglu, wa, wb, wo, nf, wr_pad, tm=512):
    t = x.shape[0]
    row = lambda w, col=0: pl.BlockSpec((tm, w), lambda i: (i, col))
    full = lambda a: pl.BlockSpec(a.shape, lambda i: (0,) * a.ndim)
    return pl.pallas_call(
        functools.partial(_merge_body, tm=tm),
        grid=(t // tm,),
        in_specs=[row(D_MODEL), row(A_WIDTH), row(A_WIDTH), row(A_WIDTH, 4),
                  pl.BlockSpec((S5_CHUNK, tm // S5_CHUNK, S5_WIDTH), lambda i: (0, i, 0)), row(S5_WIDTH),
                  row(D_MODEL, 0), row(D_MODEL, 1),
                  full(hn), full(sd), full(wglu), full(bglu), full(wa), full(wb), full(wo), full(nf), full(wr_pad)],
        out_specs=[row(D_MODEL), row(D_MODEL), row(N_EXPERTS)],
        out_shape=[jax.ShapeDtypeStruct((t, D_MODEL), F32),
                   jax.ShapeDtypeStruct((t, D_MODEL), BF16),
                   jax.ShapeDtypeStruct((t, N_EXPERTS), F32)],
        scratch_shapes=[pltpu.VMEM((S5_WIDTH // LANES, tm, LANES), F32)],
        compiler_params=_params(("arbitrary",), 48),
        name="merge",
    )(x, o_fw, o_bw, hg, ys5, u, gates, gates, hn, sd, wglu, bglu, wa, wb, wo, nf, wr_pad)


def _exclusive_prefix(m, upper, strict_lower):
    mb = m.astype(BF16)
    incl = jnp.dot(mb, upper, preferred_element_type=F32)
    row_off = jnp.sum(jnp.dot(strict_lower, mb, preferred_element_type=F32), axis=1, keepdims=True)
    return incl - m + row_off, row_off


def _select_body(lg_ref, pos_ref, gate_ref, lo_ref, aff_ref, *, rpe, cap):
    e_n = N_EXPERTS
    slab = lambda ref, e: ref[e * rpe:(e + 1) * rpe, :]
    mx = slab(lg_ref, 0)
    for e in range(1, e_n):
        mx = jnp.maximum(mx, slab(lg_ref, e))
    den = jnp.zeros_like(mx)
    for e in range(e_n):
        ex = jnp.exp(slab(lg_ref, e) - mx)
        aff_ref[e * rpe:(e + 1) * rpe, :] = ex
        den = den + ex
    for e in range(e_n):
        aff_ref[e * rpe:(e + 1) * rpe, :] = slab(aff_ref, e) / den

    capf = float(cap)

    def bit_step(i, cand):
        bit = jnp.left_shift(jnp.int32(1), 30 - i)
        rows = []
        for e in range(e_n):
            cur = cand[e:e + 1, :]
            trial = cur | bit
            cnt = jnp.sum(jnp.where(slab(aff_ref, e) >= pltpu.bitcast(trial, F32), 1.0, 0.0), keepdims=True)
            rows.append(jnp.where(cnt >= capf, trial, cur))
        return jnp.concatenate(rows, axis=0)

    thr = pltpu.bitcast(lax.fori_loop(0, 31, bit_step, jnp.zeros((e_n, LANES), I32)), F32)

    upper = jnp.where(lax.broadcasted_iota(I32, (LANES, LANES), 0) <= lax.broadcasted_iota(I32, (LANES, LANES), 1),
                      1.0, 0.0).astype(BF16)
    strict_lower = jnp.where(lax.broadcasted_iota(I32, (rpe, rpe), 1) < lax.broadcasted_iota(I32, (rpe, rpe), 0),
                             1.0, 0.0).astype(BF16)
    for e in range(e_n):
        aff = slab(aff_ref, e)
        t = thr[e:e + 1, :]
        gt = jnp.where(aff > t, 1.0, 0.0)
        eq = jnp.where(aff == t, 1.0, 0.0)
        need = capf - jnp.sum(gt, keepdims=True)
        rank_eq, _ = _exclusive_prefix(eq, upper, strict_lower)
        sel = gt + eq * jnp.where(rank_eq < need, 1.0, 0.0)
        pos, row_off = _exclusive_prefix(sel, upper, strict_lower)
        chosen = sel > 0.5
        pos_ref[e * rpe:(e + 1) * rpe, :] = jnp.where(chosen, pos, -1.0).astype(I32)
        gate_ref[e * rpe:(e + 1) * rpe, :] = jnp.where(chosen, aff, 0.0)
        lo_ref[e * rpe:(e + 1) * rpe, :] = jnp.broadcast_to(row_off, (rpe, LANES)).astype(I32)


def _select(lg_em, rpe, cap):
    rows = N_EXPERTS * rpe
    shp = lambda dt: jax.ShapeDtypeStruct((rows, LANES), dt)
    return pl.pallas_call(
        functools.partial(_select_body, rpe=rpe, cap=cap),
        out_shape=[shp(I32), shp(F32), shp(I32)],
        scratch_shapes=[pltpu.VMEM((rows, LANES), F32)],
        compiler_params=pltpu.CompilerParams(vmem_limit_bytes=32 * MIB),
        name="select",
    )(lg_em)


def _block_lo(lo_ref, e, blk, rpe):
    return lo_ref[e * rpe + blk * (TOK_BLOCK // LANES)]


def _block_count(lo_ref, e, blk, rpe, n_blk, cap):
    nxt = lo_ref[e * rpe + jnp.minimum((blk + 1) * (TOK_BLOCK // LANES), rpe - 1)]
    return jnp.where(blk == n_blk - 1, cap, nxt) - _block_lo(lo_ref, e, blk, rpe)


def _floor_tile(row):
    return row - (row & (BF16_TILE_ROWS - 1))


def _dispatch_body(lo_ref, pm_ref, h_ref, xe_ref, stage_ref, carry_ref, sem_ref, *, rpe, n_blk, cap):
    b = pl.program_id(0)

    @pl.when(b == 0)
    def _():
        carry_ref[...] = jnp.zeros_like(carry_ref)

    pm = pm_ref[...]
    hb = h_ref[...]
    r_idx = lax.broadcasted_iota(I32, (DISPATCH_WIN, TOK_BLOCK), 0)
    head_row = lax.broadcasted_iota(I32, (BF16_TILE_ROWS, D_MODEL), 0)
    los = [_block_lo(lo_ref, e, b, rpe) for e in range(N_EXPERTS)]
    bases = [_floor_tile(lo) for lo in los]
    onehot = jnp.concatenate(
        [jnp.where(r_idx == pm[e:e + 1, :] - bases[e], 1.0, 0.0) for e in range(N_EXPERTS)], axis=0).astype(BF16)
    xw = jnp.dot(onehot, hb, preferred_element_type=F32)

    def window_copy(e, first_row):
        start = pl.multiple_of(first_row, BF16_TILE_ROWS)
        return pltpu.make_async_copy(stage_ref.at[e], xe_ref.at[e, pl.ds(start, DISPATCH_WIN), :], sem_ref.at[e])

    for e in range(N_EXPERTS):
        @pl.when(b > 0)
        def _():
            window_copy(e, 0).wait()

        rows = xw[e * DISPATCH_WIN:(e + 1) * DISPATCH_WIN, :]
        head = jnp.where(head_row < los[e] - bases[e], carry_ref[e].astype(F32), rows[:BF16_TILE_ROWS, :])
        stage_ref[e, :BF16_TILE_ROWS, :] = head.astype(BF16)
        stage_ref[e, BF16_TILE_ROWS:, :] = rows[BF16_TILE_ROWS:, :].astype(BF16)
        window_copy(e, bases[e]).start()

    ends = [los[e] + _block_count(lo_ref, e, b, rpe, n_blk, cap) for e in range(N_EXPERTS)]
    n_wins = [lax.div(ends[e] - bases[e] + (DISPATCH_WIN - 1), DISPATCH_WIN) for e in range(N_EXPERTS)]

    @pl.when(functools.reduce(jnp.maximum, n_wins) > 1)
    def _():
        for e in range(N_EXPERTS):
            def extra_window(k, carry):
                window_copy(e, 0).wait()
                oh = jnp.where(r_idx + k * DISPATCH_WIN == pm[e:e + 1, :] - bases[e], 1.0, 0.0).astype(BF16)
                stage_ref[e] = jnp.dot(oh, hb, preferred_element_type=F32).astype(BF16)
                window_copy(e, bases[e] + k * DISPATCH_WIN).start()
                return carry

            lax.fori_loop(1, n_wins[e], extra_window, 0)

    for e in range(N_EXPERTS):
        last_start = bases[e] + (jnp.maximum(n_wins[e], 1) - 1) * DISPATCH_WIN
        tile = jnp.minimum(_floor_tile(ends[e]) - last_start, DISPATCH_WIN - BF16_TILE_ROWS)
        carry_ref[e] = stage_ref[e, pl.ds(pl.multiple_of(tile, BF16_TILE_ROWS), BF16_TILE_ROWS), :]

    @pl.when(b == n_blk - 1)
    def _():
        for e in range(N_EXPERTS):
            window_copy(e, 0).wait()
            stage_ref[e] = jnp.zeros((DISPATCH_WIN, D_MODEL), BF16)
            window_copy(e, cap).start()
        for e in range(N_EXPERTS):
            window_copy(e, cap).wait()


def _dispatch(lo, posm, h2, cap):
    t = h2.shape[0]
    rpe = t // LANES
    n_blk = t // TOK_BLOCK
    grid_spec = pltpu.PrefetchScalarGridSpec(
        num_scalar_prefetch=1,
        grid=(n_blk,),
        in_specs=[pl.BlockSpec((N_EXPERTS, TOK_BLOCK), lambda b, lo_r: (0, b)),
                  pl.BlockSpec((TOK_BLOCK, D_MODEL), lambda b, lo_r: (b, 0))],
        out_specs=pl.BlockSpec(memory_space=pl.ANY),
        scratch_shapes=[pltpu.VMEM((N_EXPERTS, DISPATCH_WIN, D_MODEL), BF16),
                        pltpu.VMEM((N_EXPERTS, BF16_TILE_ROWS, D_MODEL), BF16),
                        pltpu.SemaphoreType.DMA((N_EXPERTS,))],
    )
    return pl.pallas_call(
        functools.partial(_dispatch_body, rpe=rpe, n_blk=n_blk, cap=cap),
        grid_spec=grid_spec,
        out_shape=jax.ShapeDtypeStruct((N_EXPERTS, cap + DISPATCH_WIN, D_MODEL), BF16),
        compiler_params=_params(("arbitrary",), 32),
        name="dispatch",
    )(lo, posm, h2)


def _ffn_body(x_ref, wg_ref, wu_ref, wd_ref, y_ref, acc_ref, *, n_f, n_sub, sub):
    f = pl.program_id(1)
    wg = wg_ref[0].astype(BF16)
    wu = wu_ref[0].astype(BF16)
    wd = wd_ref[0].astype(BF16)

    def sweep(first, last):
        def rows_step(i, carry):
            r = pl.ds(pl.multiple_of(i * sub, sub), sub)
            x = x_ref[0, r, :]
            a = jnp.dot(x, wg, preferred_element_type=F32)
            u = jnp.dot(x, wu, preferred_element_type=F32)
            h = (jax.nn.silu(a) * u).astype(BF16)
            d = jnp.dot(h, wd, preferred_element_type=F32)
            if not first:
                d = acc_ref[r, :] + d
            if last:
                y_ref[0, r, :] = d.astype(BF16)
            else:
                acc_ref[r, :] = d
            return carry

        lax.fori_loop(0, n_sub, rows_step, 0)

    if n_f == 1:
        sweep(True, True)
    else:
        pl.when(f == 0)(lambda: sweep(True, False))
        pl.when(jnp.logical_and(f > 0, f < n_f - 1))(lambda: sweep(False, False))
        pl.when(f == n_f - 1)(lambda: sweep(False, True))


def _ffn(xe, w_gate, w_up, w_down, cap, tf=512, sub=1024):
    sub = min(sub, cap)
    n_f = EXPERT_FF // tf
    return pl.pallas_call(
        functools.partial(_ffn_body, n_f=n_f, n_sub=cap // sub, sub=sub),
        grid=(N_EXPERTS, n_f),
        in_specs=[pl.BlockSpec((1, cap, D_MODEL), lambda e, f: (e, 0, 0)),
                  pl.BlockSpec((1, D_MODEL, tf), lambda e, f: (e, 0, f)),
                  pl.BlockSpec((1, D_MODEL, tf), lambda e, f: (e, 0, f)),
                  pl.BlockSpec((1, tf, D_MODEL), lambda e, f: (e, f, 0))],
        out_specs=pl.BlockSpec((1, cap, D_MODEL), lambda e, f: (e, 0, 0)),
        out_shape=jax.ShapeDtypeStruct((N_EXPERTS, cap, D_MODEL), BF16),
        scratch_shapes=[pltpu.VMEM((cap, D_MODEL), F32)],
        compiler_params=_params(("arbitrary", "arbitrary"), 56),
        name="expert_ffn",
    )(xe, w_gate, w_up, w_down)


def _combine_body(lo_ref, pm_ref, gm_ref, x1_ref, nf_ref, y_ref, out_ref,
                  a_ref, m_ref, ovw_ref, sem_ref, ovsem_ref, *, rpe, n_blk, cap):
    WIN = COMBINE_WIN
    b = pl.program_id(0)
    slot = b % 2

    def win_start(e, blk):
        return jnp.minimum(_floor_tile(_block_lo(lo_ref, e, blk, rpe)), cap - WIN)

    def window_copy(e, blk, sl):
        start = pl.multiple_of(win_start(e, blk), BF16_TILE_ROWS)
        return pltpu.make_async_copy(y_ref.at[e, pl.ds(start, WIN), :], a_ref.at[sl, pl.ds(e * WIN, WIN), :],
                                     sem_ref.at[sl, e])

    @pl.when(b == 0)
    def _():
        for e in range(N_EXPERTS):
            window_copy(e, 0, 0).start()

    @pl.when(b + 1 < n_blk)
    def _():
        for e in range(N_EXPERTS):
            window_copy(e, b + 1, 1 - slot).start()

    pm = pm_ref[...]
    gm = gm_ref[...]
    ws0 = [win_start(e, b) for e in range(N_EXPERTS)]
    e_col = lax.broadcasted_iota(I32, (1, N_EXPERTS), 1)
    ws_row = jnp.zeros((1, N_EXPERTS), I32)
    for e in range(N_EXPERTS):
        ws_row = jnp.where(e_col == e, ws0[e], ws_row)
    rel = jnp.clip(pm - ws_row, -1, WIN).astype(F32).astype(BF16)
    spread = jnp.where(lax.broadcasted_iota(I32, (N_EXPERTS, N_EXPERTS * WIN), 1) // WIN
                       == lax.broadcasted_iota(I32, (N_EXPERTS, N_EXPERTS * WIN), 0), 1.0, 0.0).astype(BF16)
    tgt = jnp.dot(rel, spread, preferred_element_type=F32)
    gv = jnp.dot(gm.astype(BF16), spread, preferred_element_type=F32)
    lane_r = (lax.broadcasted_iota(I32, (TOK_BLOCK, N_EXPERTS * WIN), 1) & (WIN - 1)).astype(F32)
    gmat = jnp.where(tgt == lane_r, gv, 0.0).astype(BF16)

    for e in range(N_EXPERTS):
        window_copy(e, b, slot).wait()
    m_ref[...] = jnp.dot(gmat, a_ref[slot], preferred_element_type=F32)

    ends = [_block_lo(lo_ref, e, b, rpe) + _block_count(lo_ref, e, b, rpe, n_blk, cap) for e in range(N_EXPERTS)]
    n_wins = [lax.div(jnp.maximum(ends[e] - ws0[e], 0) + (WIN - 1), WIN) for e in range(N_EXPERTS)]

    @pl.when(functools.reduce(jnp.maximum, n_wins) > 1)
    def _():
        lane_w = lax.broadcasted_iota(I32, (TOK_BLOCK, WIN), 1)
        for e in range(N_EXPERTS):
            def extra_window(k, carry):
                first = ws0[e] + k * WIN
                ws = jnp.minimum(first, cap - WIN)
                cp = pltpu.make_async_copy(y_ref.at[e, pl.ds(pl.multiple_of(ws, BF16_TILE_ROWS), WIN), :], ovw_ref,
                                           ovsem_ref)
                cp.start()
                cp.wait()
                pe = pm[:, e:e + 1]
                gk = jnp.where((lane_w == pe - ws) & (pe >= first), gm[:, e:e + 1], 0.0).astype(BF16)
                m_ref[...] += jnp.dot(gk, ovw_ref[...], preferred_element_type=F32)
                return carry

            lax.fori_loop(1, n_wins[e], extra_window, 0)

    xo = x1_ref[...] + m_ref[...]
    out_ref[...] = (xo * lax.rsqrt(jnp.mean(xo * xo, axis=-1, keepdims=True) + RMS_EPS)) * nf_ref[...]


def _combine(lo, posm_t, gate_t, x1, norm_final, y2, cap):
    t = x1.shape[0]
    rpe = t // LANES
    n_blk = t // TOK_BLOCK
    grid_spec = pltpu.PrefetchScalarGridSpec(
        num_scalar_prefetch=1,
        grid=(n_blk,),
        in_specs=[pl.BlockSpec((TOK_BLOCK, N_EXPERTS), lambda b, lo_r: (b, 0)),
                  pl.BlockSpec((TOK_BLOCK, N_EXPERTS), lambda b, lo_r: (b, 0)),
                  pl.BlockSpec((TOK_BLOCK, D_MODEL), lambda b, lo_r: (b, 0)),
                  pl.BlockSpec((1, D_MODEL), lambda b, lo_r: (0, 0)),
                  pl.BlockSpec(memory_space=pl.ANY)],
        out_specs=pl.BlockSpec((TOK_BLOCK, D_MODEL), lambda b, lo_r: (b, 0)),
        scratch_shapes=[pltpu.VMEM((2, N_EXPERTS * COMBINE_WIN, D_MODEL), BF16),
                        pltpu.VMEM((TOK_BLOCK, D_MODEL), F32),
                        pltpu.VMEM((COMBINE_WIN, D_MODEL), BF16),
                        pltpu.SemaphoreType.DMA((2, N_EXPERTS)),
                        pltpu.SemaphoreType.DMA(())],
    )
    return pl.pallas_call(
        functools.partial(_combine_body, rpe=rpe, n_blk=n_blk, cap=cap),
        grid_spec=grid_spec,
        out_shape=jax.ShapeDtypeStruct((t, D_MODEL), F32),
        compiler_params=_params(("arbitrary",), 40),
        name="combine",
    )(lo, posm_t, gate_t, x1, norm_final, y2)


def _trunk(x, p):
    b, l, _ = x.shape
    t = b * l
    n_seq = l // S5_CHUNK
    cap = max(1, EC_CAPACITY_FACTOR * t // N_EXPERTS)
    rpe = t // LANES

    hg, u, u_sm, gates = _inproj(x.reshape(t, D_MODEL), p['norm_mix'], p['w_in'])
    o_fw, o_bw = _hgrn(hg.reshape(b, l, 5 * A_WIDTH), p['lb'])
    ys5 = _s5(u_sm, p['s5_perm'], *p['s5'], n_seq)

    x1, h2, logits = _merge(x.reshape(t, D_MODEL), o_fw.reshape(t, A_WIDTH), o_bw.reshape(t, A_WIDTH), hg, ys5, u,
                            gates, p['hgrn_norm'], p['s5_d'], p['w_glu'], p['b_glu'], p['w_a'], p['w_b'],
                            p['w_out'], p['norm_ffn'], p['w_router'])

    posm, gate, lo = _select(logits.T.reshape(N_EXPERTS * rpe, LANES), rpe, cap)
    lo = lo[:, 0]
    posm = posm.reshape(N_EXPERTS, t)
    xe = _dispatch(lo, posm, h2, cap)
    y2 = _ffn(xe, p['w_gate'], p['w_up'], p['w_down'], cap)
    out = _combine(lo, posm.T, gate.reshape(N_EXPERTS, t).T, x1, p['norm_final'], y2, cap)
    return out.reshape(b, l, D_MODEL)


def kernel(x_prompt, x_sample, norm_mix, w_in, hgrn_gamma, hgrn_norm, s5_a_re, s5_a_im, s5_log_dt, s5_b_re, s5_b_im, s5_c_re, s5_c_im, s5_d, s5_w_glu, s5_b_glu, w_branch_a, w_branch_b, w_out, norm_ffn, w_router, w_exp_gate, w_exp_up, w_exp_down, norm_final):
    assert norm_mix.shape[0] == 1, "single-layer trunk"
    row = lambda a: a.astype(F32).reshape(1, -1)
    lb = jnp.cumsum(jax.nn.softmax(hgrn_gamma.astype(F32), axis=0), axis=0)[0]
    max_chunks = max(x_prompt.shape[1], x_sample.shape[1]) // S5_CHUNK
    s5_ops = _s5_operators(s5_a_re[0], s5_a_im[0], s5_log_dt[0], s5_b_re[0], s5_b_im[0], s5_c_re[0], s5_c_im[0],
                           max_chunks.bit_length() - 1)
    p = {
        'norm_mix': row(norm_mix[0]), 'w_in': w_in[0].astype(BF16), 'lb': lb, 's5': s5_ops,
        's5_perm': _s5_lane_permutation(),
        'hgrn_norm': row(hgrn_norm[0]), 's5_d': row(s5_d[0]), 'w_glu': s5_w_glu[0].astype(BF16),
        'b_glu': row(s5_b_glu[0]), 'w_a': w_branch_a[0].astype(BF16), 'w_b': w_branch_b[0].astype(BF16),
        'w_out': w_out[0].astype(BF16), 'norm_ffn': row(norm_ffn[0]),
        'w_router': _split_router(w_router[0]),
        'w_gate': w_exp_gate[0], 'w_up': w_exp_up[0], 'w_down': w_exp_down[0],
        'norm_final': row(norm_final),
    }
    return (_trunk(x_prompt, p), _trunk(x_sample, p))
```

```python
import functools
import math

import jax
import jax.numpy as jnp
from jax import lax
from jax.experimental import pallas as pl
from jax.experimental.pallas import tpu as pltpu

F32 = jnp.float32
BF16 = jnp.bfloat16
I32 = jnp.int32

D_MODEL = 1024
A_HEADS = 4
A_HEAD_DIM = 128
A_WIDTH = A_HEADS * A_HEAD_DIM
A_CHUNK = 64
S5_WIDTH = 512
S5_GROUP = 16
S5_GROUPS = S5_WIDTH // S5_GROUP
S5_STATE = 64
N_EXPERTS = 16
EXPERT_FF = 2048
EC_CAPACITY_FACTOR = 2
RMS_EPS = 1e-6
IN_COLS = 5 * A_WIDTH + S5_WIDTH + 2 * D_MODEL

LANES = 128
BF16_TILE_ROWS = 16
S5_CHUNK = 16
S5_COLS = S5_CHUNK * S5_GROUP
S5_GROUPS_PER_STEP = LANES // S5_GROUP
TOK_BLOCK = 256
COMBINE_WIN = 64
DISPATCH_WIN = 64
MIB = 2 ** 20


def _params(semantics, vmem_mib):
    return pltpu.CompilerParams(dimension_semantics=semantics, vmem_limit_bytes=vmem_mib * MIB)


def _inproj_body(x_ref, g_ref, w_ref, hg_ref, u_ref, us_ref, gt_ref, uf_ref, *, tm):
    x = x_ref[...]
    ms = jnp.mean(x * x, axis=-1, keepdims=True)
    h = ((x * lax.rsqrt(ms + RMS_EPS)) * g_ref[...]).astype(BF16)
    n_hg = 5 * A_WIDTH // 512
    for c in range(IN_COLS // 512):
        r = jnp.dot(h, w_ref[:, c * 512:(c + 1) * 512], preferred_element_type=F32)
        if c < n_hg:
            hg_ref[:, c * 512:(c + 1) * 512] = r.astype(BF16)
        elif c == n_hg:
            u_ref[...] = r.astype(BF16)
            for cb in range(S5_WIDTH // LANES):
                uf_ref[cb] = r[:, cb * LANES:(cb + 1) * LANES]
                for sg in range(S5_CHUNK):
                    us_ref[sg, :, cb * LANES:(cb + 1) * LANES] = uf_ref[
                        cb, pl.ds(sg, tm // S5_CHUNK, stride=S5_CHUNK), :].astype(BF16)
        else:
            gt_ref[:, (c - n_hg - 1) * 512:(c - n_hg) * 512] = r.astype(BF16)


def _inproj(x, gain, w_bf16, tm=512):
    t = x.shape[0]
    return pl.pallas_call(
        functools.partial(_inproj_body, tm=tm),
        grid=(t // tm,),
        in_specs=[pl.BlockSpec((tm, D_MODEL), lambda i: (i, 0)),
                  pl.BlockSpec((1, D_MODEL), lambda i: (0, 0)),
                  pl.BlockSpec((D_MODEL, IN_COLS), lambda i: (0, 0))],
        out_specs=[pl.BlockSpec((tm, 5 * A_WIDTH), lambda i: (i, 0)),
                   pl.BlockSpec((tm, S5_WIDTH), lambda i: (i, 0)),
                   pl.BlockSpec((S5_CHUNK, tm // S5_CHUNK, S5_WIDTH), lambda i: (0, i, 0)),
                   pl.BlockSpec((tm, 2 * D_MODEL), lambda i: (i, 0))],
        out_shape=[jax.ShapeDtypeStruct((t, 5 * A_WIDTH), BF16),
                   jax.ShapeDtypeStruct((t, S5_WIDTH), BF16),
                   jax.ShapeDtypeStruct((S5_CHUNK, t // S5_CHUNK, S5_WIDTH), BF16),
                   jax.ShapeDtypeStruct((t, 2 * D_MODEL), BF16)],
        scratch_shapes=[pltpu.VMEM((S5_WIDTH // LANES, tm, LANES), F32)],
        compiler_params=_params(("arbitrary",), 48),
        name="inproj",
    )(x, gain, w_bf16)


def _hgrn_pair(q, fr, v, lb, s_ref, forward):
    c, c2 = A_CHUNK, 2 * A_CHUNK
    t_idx = lax.broadcasted_iota(I32, (c2, c2), 0)
    s_idx = lax.broadcasted_iota(I32, (c2, c2), 1)
    one = lambda cond: jnp.where(cond, 1.0, 0.0)
    s_loc = s_idx & (c - 1)
    same_chunk = (t_idx < c) == (s_idx < c)
    tri = same_chunk & ((s_idx <= t_idx) if forward else (s_idx >= t_idx))
    m_run = one(tri)
    m_mid = one(same_chunk & ((s_loc <= c // 2 - 1) if forward else (s_loc >= c // 2)))
    m_all = one(same_chunk)
    order = (0, 1) if forward else (1, 0)

    q = q.astype(F32)
    f = lb + (1.0 - lb) * jax.nn.sigmoid(fr.astype(F32))
    lf = jnp.log2(f)
    kk = 1.0 - f
    hi = lf.astype(BF16)
    hl = jnp.concatenate([hi, (lf - hi.astype(F32)).astype(BF16)], axis=0)

    def sums(m):
        mb = m.astype(BF16)
        return jnp.dot(jnp.concatenate([mb, mb], axis=1), hl, preferred_element_type=F32)

    e_mid = jnp.exp2(sums(m_run - m_mid))
    q_in = (q * e_mid).astype(BF16)
    k_in = kk * (1.0 / e_mid)
    qdec = (q * jnp.exp2(sums(m_run))).astype(BF16)
    kdec = kk * jnp.exp2(sums(m_all - m_run))
    total = [sums(one(jnp.broadcast_to((s_idx < c) == (ch == 0), (c2, c2)))) for ch in (0, 1)]
    zero = jnp.zeros((), BF16)
    outs = []
    for h in range(A_HEADS):
        sl = slice(h * A_HEAD_DIM, (h + 1) * A_HEAD_DIM)
        vh = v[:, sl]
        k_t = k_in[:, sl].T.astype(BF16)
        kd_t = kdec[:, sl].T.astype(BF16)
        s = jnp.where(tri, jnp.dot(q_in[:, sl], k_t, preferred_element_type=F32), 0.0).astype(BF16)
        o_intra = jnp.dot(s, vh, preferred_element_type=F32)
        state = s_ref[h]
        o_inter = [None, None]
        for ch in order:
            rows = slice(ch * c, (ch + 1) * c)
            o_inter[ch] = jnp.dot(qdec[rows, sl], state.astype(BF16), preferred_element_type=F32)
            own_rows = (t_idx < c) if ch == 0 else (t_idx >= c)
            v_ch = jnp.where(own_rows, vh, zero)
            decay = jnp.exp2(total[ch][:, sl].T)
            state = decay * state + jnp.dot(kd_t, v_ch, preferred_element_type=F32)
        s_ref[h] = state
        outs.append(o_intra + jnp.concatenate(o_inter, axis=0))
    return jnp.concatenate(outs, axis=1).astype(BF16)


def _hgrn_body(qf_ref, ff_ref, vf_ref, qb_ref, fb_ref, vb_ref, lb_ref, of_ref, ob_ref, sf_ref, sb_ref,
               *, n_pairs):
    @pl.when(pl.program_id(1) == 0)
    def _():
        sf_ref[...] = jnp.zeros_like(sf_ref)
        sb_ref[...] = jnp.zeros_like(sb_ref)

    c2 = 2 * A_CHUNK

    def step(j, carry):
        rf = pl.ds(pl.multiple_of(j * c2, c2), c2)
        of_ref[0, rf, :] = _hgrn_pair(qf_ref[0, rf, :], ff_ref[0, rf, :], vf_ref[0, rf, :],
                                      lb_ref[0:1, :], sf_ref, True)
        rb = pl.ds(pl.multiple_of((n_pairs - 1 - j) * c2, c2), c2)
        ob_ref[0, rb, :] = _hgrn_pair(qb_ref[0, rb, :], fb_ref[0, rb, :], vb_ref[0, rb, :],
                                      lb_ref[1:2, :], sb_ref, False)
        return carry

    lax.fori_loop(0, n_pairs, step, 0, unroll=True)


def _hgrn(hg, lb, lb_tokens=512):
    b, l, _ = hg.shape
    nb = l // lb_tokens
    blk = (1, lb_tokens, A_WIDTH)
    fwd = lambda col: pl.BlockSpec(blk, lambda bi, i: (bi, i, col))
    bwd = lambda col: pl.BlockSpec(blk, lambda bi, i: (bi, nb - 1 - i, col))
    return pl.pallas_call(
        functools.partial(_hgrn_body, n_pairs=lb_tokens // (2 * A_CHUNK)),
        grid=(b, nb),
        in_specs=[fwd(0), fwd(1), fwd(3), bwd(0), bwd(2), bwd(3),
                  pl.BlockSpec((2, A_WIDTH), lambda bi, i: (0, 0))],
        out_specs=[pl.BlockSpec(blk, lambda bi, i: (bi, i, 0)),
                   pl.BlockSpec(blk, lambda bi, i: (bi, nb - 1 - i, 0))],
        out_shape=[jax.ShapeDtypeStruct((b, l, A_WIDTH), BF16)] * 2,
        scratch_shapes=[pltpu.VMEM((A_HEADS, A_HEAD_DIM, A_HEAD_DIM), F32)] * 2,
        compiler_params=_params(("arbitrary", "arbitrary"), 32),
        name="hgrn2",
    )(hg, hg, hg, hg, hg, hg, lb)


def _s5_direction_terms(a_re, a_im, log_dt, b_re, b_im):
    a_re, a_im = a_re.astype(F32), a_im.astype(F32)
    b_re, b_im = b_re.astype(F32), b_im.astype(F32)
    dt = jnp.exp(log_dt.astype(F32))[:, None]
    j = jnp.arange(S5_CHUNK + 1, dtype=F32)[:, None, None]
    mag = jnp.exp(j * (dt * a_re))
    pw_re = mag * jnp.cos(j * (dt * a_im))
    pw_im = mag * jnp.sin(j * (dt * a_im))
    nr, ni = pw_re[1] - 1.0, pw_im[1]
    den = a_re * a_re + a_im * a_im
    f_re = (nr * a_re + ni * a_im) / den
    f_im = (ni * a_re - nr * a_im) / den
    bb_re = f_re[..., None] * b_re - f_im[..., None] * b_im
    bb_im = f_re[..., None] * b_im + f_im[..., None] * b_re
    pb_re = pw_re[..., None] * bb_re - pw_im[..., None] * bb_im
    pb_im = pw_re[..., None] * bb_im + pw_im[..., None] * bb_re
    return pw_re, pw_im, pb_re, pb_im


def _s5_operators(a_re, a_im, log_dt, b_re, b_im, c_re, c_im, n_steps):
    hp = lax.Precision.HIGHEST
    c_re, c_im = c_re.astype(F32), c_im.astype(F32)
    g = S5_GROUPS
    terms = [_s5_direction_terms(a_re[d], a_im[d], log_dt[d], b_re, b_im) for d in range(2)]
    taps = []
    for pw_re, pw_im, pb_re, pb_im in terms:
        k = (jnp.einsum('gop,jgpi->jgoi', c_re, pb_re[:S5_CHUNK], precision=hp)
             - jnp.einsum('gop,jgpi->jgoi', c_im, pb_im[:S5_CHUNK], precision=hp))
        taps.append(k)
    c = S5_CHUNK
    strip = jnp.concatenate([taps[1][:0:-1], (taps[0][0] + taps[1][0])[None], taps[0][1:]], axis=0)
    strip = strip.transpose(1, 3, 0, 2).reshape(g, S5_GROUP, (2 * c - 1) * S5_GROUP)
    w_intra = jnp.concatenate(
        [strip[:, :, (c - 1 - s) * S5_GROUP:(c - 1 - s) * S5_GROUP + S5_COLS] for s in range(c)], axis=1)

    def w_in(pb_re, pb_im, order):
        re = pb_re[order].transpose(1, 0, 3, 2)
        im = pb_im[order].transpose(1, 0, 3, 2)
        return jnp.concatenate([re, im], axis=-1).reshape(g, S5_COLS, 2 * S5_STATE)

    def w_out(pw_re, pw_im, order):
        pr, pi = pw_re[order], pw_im[order]
        m_re = c_re[None] * pr[:, :, None, :] - c_im[None] * pi[:, :, None, :]
        m_im = c_re[None] * pi[:, :, None, :] + c_im[None] * pr[:, :, None, :]
        re = m_re.transpose(1, 3, 0, 2).reshape(g, S5_STATE, S5_COLS)
        im = -m_im.transpose(1, 3, 0, 2).reshape(g, S5_STATE, S5_COLS)
        return jnp.concatenate([re, im], axis=1)

    asc = jnp.arange(S5_CHUNK)
    w_in_f = w_in(terms[0][2], terms[0][3], S5_CHUNK - 1 - asc)
    w_in_b = w_in(terms[1][2], terms[1][3], asc)
    w_out_f = w_out(terms[0][0], terms[0][1], asc + 1)
    w_out_b = w_out(terms[1][0], terms[1][1], S5_CHUNK - asc)
    w1 = jnp.concatenate([w_intra, w_in_f, w_in_b], axis=-1).astype(BF16)
    w2 = jnp.concatenate([w_out_f, w_out_b], axis=1).astype(BF16)

    ars, ais = [], []
    cur = [(t[0][S5_CHUNK], t[1][S5_CHUNK]) for t in terms]
    for _ in range(n_steps):
        ars.append(jnp.concatenate([cur[0][0], cur[0][0], cur[1][0], cur[1][0]], axis=-1))
        ais.append(jnp.concatenate([-cur[0][1], cur[0][1], -cur[1][1], cur[1][1]], axis=-1))
        cur = [(r * r - i * i, 2.0 * r * i) for r, i in cur]
    return w1, w2, jnp.stack(ars, axis=1), jnp.stack(ais, axis=1)


def _s5_group(u, w1, w2, ar, ai, n, n_seq, n_steps):
    rows = u.shape[0]
    half = 2 * S5_STATE
    yv = jnp.dot(u, w1, preferred_element_type=F32)
    zf = jnp.where(n >= 1, pltpu.roll(yv[:, S5_COLS:S5_COLS + half], 1, axis=0), 0.0)
    zb = jnp.where(n < n_seq - 1, pltpu.roll(yv[:, S5_COLS + half:], rows - 1, axis=0), 0.0)
    for k in range(n_steps):
        s = 1 << k
        sf = jnp.where(n >= s, pltpu.roll(zf, s, axis=0), 0.0)
        sb = jnp.where(n < n_seq - s, pltpu.roll(zb, rows - s, axis=0), 0.0)
        zf = zf + ar[k:k + 1, :half] * sf + ai[k:k + 1, :half] * pltpu.roll(sf, S5_STATE, axis=1)
        zb = zb + ar[k:k + 1, half:] * sb + ai[k:k + 1, half:] * pltpu.roll(sb, S5_STATE, axis=1)
    x = jnp.concatenate([zf, zb], axis=1).astype(BF16)
    return yv[:, :S5_COLS] + jnp.dot(x, w2, preferred_element_type=F32)


def _s5_body(u_ref, perm_ref, w1_ref, w2_ref, ar_ref, ai_ref, y_ref, z_ref, *, n_seq, n_steps):
    gps = S5_GROUPS_PER_STEP
    rows = u_ref.shape[1]
    n = lax.broadcasted_iota(I32, (rows, 2 * S5_STATE), 0) & (n_seq - 1)
    perm = perm_ref[...]
    for hf in range(S5_CHUNK // gps):
        x = jnp.concatenate([u_ref[hf * gps + j] for j in range(gps)], axis=1)
        z_ref[:, hf * gps * LANES:(hf + 1) * gps * LANES] = jnp.dot(
            x, perm, preferred_element_type=F32).astype(BF16)
    for g in range(gps):
        u = jnp.concatenate([z_ref[:, (hf * gps + g) * LANES:(hf * gps + g + 1) * LANES]
                             for hf in range(S5_CHUNK // gps)], axis=1)
        y = _s5_group(u, w1_ref[g], w2_ref[g], ar_ref[g], ai_ref[g], n, n_seq, n_steps).astype(BF16)
        for hf in range(S5_CHUNK // gps):
            z_ref[:, (hf * gps + g) * LANES:(hf * gps + g + 1) * LANES] = y[:, hf * LANES:(hf + 1) * LANES]
    for hf in range(S5_CHUNK // gps):
        x = jnp.dot(z_ref[:, hf * gps * LANES:(hf + 1) * gps * LANES], perm,
                    preferred_element_type=F32).astype(BF16)
        for j in range(gps):
            y_ref[hf * gps + j] = x[:, j * LANES:(j + 1) * LANES]


def _s5_lane_permutation():
    idx = jnp.arange(S5_GROUPS_PER_STEP * LANES)
    a, b, c = idx // LANES, (idx % LANES) // S5_GROUP, idx % S5_GROUP
    dst = b * LANES + a * S5_GROUP + c
    return (dst[:, None] == idx[None, :]).astype(BF16)


def _s5(us, perm, w1, w2, ar, ai, n_seq):
    _, rows, _ = us.shape
    assert n_seq & (n_seq - 1) == 0, n_seq
    n_steps = n_seq.bit_length() - 1
    gps = S5_GROUPS_PER_STEP
    tok = pl.BlockSpec((S5_CHUNK, rows, LANES), lambda i: (0, 0, i))
    per_g = lambda a: pl.BlockSpec((gps,) + a.shape[1:], lambda i: (i, 0, 0))
    return pl.pallas_call(
        functools.partial(_s5_body, n_seq=n_seq, n_steps=n_steps),
        grid=(S5_GROUPS // gps,),
        in_specs=[tok, pl.BlockSpec(perm.shape, lambda i: (0, 0)), per_g(w1), per_g(w2), per_g(ar), per_g(ai)],
        out_specs=tok,
        out_shape=jax.ShapeDtypeStruct(us.shape, BF16),
        scratch_shapes=[pltpu.VMEM((rows, S5_CHUNK * LANES), BF16)],
        compiler_params=_params(("arbitrary",), 48),
        name="s5",
    )(us, perm, w1, w2, ar, ai)


def _gelu_tanh(x):
    return 0.5 * x * (1.0 + jnp.tanh(math.sqrt(2.0 / math.pi) * (x + 0.044715 * (x * x * x))))


def _merge_body(x_ref, of_ref, ob_ref, go_ref, ys_ref, u_ref, ga_ref, gb_ref,
                hn_ref, sd_ref, wglu_ref, bglu_ref, wa_ref, wb_ref, wo_ref, nf_ref, wr_ref,
                x1_ref, h2_ref, lg_ref, yn_ref, *, tm):
    for sg in range(S5_CHUNK):
        for cb in range(S5_WIDTH // LANES):
            yn_ref[cb, pl.ds(sg, tm // S5_CHUNK, stride=S5_CHUNK), :] = ys_ref[
                sg, :, cb * LANES:(cb + 1) * LANES].astype(F32)
    o = of_ref[...].astype(F32) + ob_ref[...].astype(F32)
    parts = []
    for h in range(A_HEADS):
        oh = o[:, h * A_HEAD_DIM:(h + 1) * A_HEAD_DIM]
        parts.append(oh * lax.rsqrt(jnp.mean(oh * oh, axis=-1, keepdims=True) + RMS_EPS))
    h_a = jnp.concatenate(parts, axis=1) * hn_ref[...] * jax.nn.silu(go_ref[...].astype(F32))

    u = u_ref[...].astype(F32)
    y_s5 = jnp.concatenate([yn_ref[cb] for cb in range(S5_WIDTH // LANES)], axis=1)
    ys = _gelu_tanh(y_s5 + sd_ref[...] * u)
    z = jnp.dot(ys.astype(BF16), wglu_ref[...], preferred_element_type=F32) + bglu_ref[...]
    h_b = ys * jax.nn.sigmoid(z)

    y_a = jnp.dot(h_a.astype(BF16), wa_ref[...], preferred_element_type=F32)
    y_b = jnp.dot(h_b.astype(BF16), wb_ref[...], preferred_element_type=F32)
    merged = jax.nn.sigmoid(ga_ref[...].astype(F32)) * y_a + jax.nn.sigmoid(gb_ref[...].astype(F32)) * y_b
    x1 = x_ref[...] + jnp.dot(merged.astype(BF16), wo_ref[...], preferred_element_type=F32)
    x1_ref[...] = x1

    h2 = (x1 * lax.rsqrt(jnp.mean(x1 * x1, axis=-1, keepdims=True) + RMS_EPS)) * nf_ref[...]
    h2_hi = h2.astype(BF16)
    h2_ref[...] = h2_hi
    h2_lo = (h2 - h2_hi.astype(F32)).astype(BF16)
    r_hi = jnp.dot(h2_hi, wr_ref[...], preferred_element_type=F32)
    r_lo = jnp.dot(h2_lo, wr_ref[...], preferred_element_type=F32)
    lg_ref[...] = r_hi[:, :N_EXPERTS] + r_hi[:, N_EXPERTS:2 * N_EXPERTS] + r_lo[:, :N_EXPERTS]


def _split_router(w_router):
    w = w_router.astype(F32)
    hi = w.astype(BF16)
    lo = (w - hi.astype(F32)).astype(BF16)
    pad = jnp.zeros((w.shape[0], LANES - 2 * N_EXPERTS), BF16)
    return jnp.concatenate([hi, lo, pad], axis=1)


def _merge(x, o_fw, o_bw, hg, ys5, u, gates, hn, sd, wglu, bglu, wa, wb, wo, nf, wr_pad, tm=512):
    t = x.shape[0]
    row = lambda w, col=0: pl.BlockSpec((tm, w), lambda i: (i, col))
    full = lambda a: pl.BlockSpec(a.shape, lambda i: (0,) * a.ndim)
    return pl.pallas_call(
        functools.partial(_merge_body, tm=tm),
        grid=(t // tm,),
        in_specs=[row(D_MODEL), row(A_WIDTH), row(A_WIDTH), row(A_WIDTH, 4),
                  pl.BlockSpec((S5_CHUNK, tm // S5_CHUNK, S5_WIDTH), lambda i: (0, i, 0)), row(S5_WIDTH),
                  row(D_MODEL, 0), row(D_MODEL, 1),
                  full(hn), full(sd), full(wglu), full(bglu), full(wa), full(wb), full(wo), full(nf), full(wr_pad)],
        out_specs=[row(D_MODEL), row(D_MODEL), row(N_EXPERTS)],
        out_shape=[jax.ShapeDtypeStruct((t, D_MODEL), F32),
                   jax.ShapeDtypeStruct((t, D_MODEL), BF16),
                   jax.ShapeDtypeStruct((t, N_EXPERTS), F32)],
        scratch_shapes=[pltpu.VMEM((S5_WIDTH // LANES, tm, LANES), F32)],
        compiler_params=_params(("arbitrary",), 48),
        name="merge",
    )(x, o_fw, o_bw, hg, ys5, u, gates, gates, hn, sd, wglu, bglu, wa, wb, wo, nf, wr_pad)


def _exclusive_prefix(m, upper, strict_lower):
    mb = m.astype(BF16)
    incl = jnp.dot(mb, upper, preferred_element_type=F32)
    row_off = jnp.sum(jnp.dot(strict_lower, mb, preferred_element_type=F32), axis=1, keepdims=True)
    return incl - m + row_off, row_off


def _select_body(lg_ref, pos_ref, gate_ref, lo_ref, aff_ref, *, rpe, cap):
    e_n = N_EXPERTS
    slab = lambda ref, e: ref[e * rpe:(e + 1) * rpe, :]
    mx = slab(lg_ref, 0)
    for e in range(1, e_n):
        mx = jnp.maximum(mx, slab(lg_ref, e))
    den = jnp.zeros_like(mx)
    for e in range(e_n):
        ex = jnp.exp(slab(lg_ref, e) - mx)
        aff_ref[e * rpe:(e + 1) * rpe, :] = ex
        den = den + ex
    for e in range(e_n):
        aff_ref[e * rpe:(e + 1) * rpe, :] = slab(aff_ref, e) / den

    capf = float(cap)

    def bit_step(i, cand):
        bit = jnp.left_shift(jnp.int32(1), 30 - i)
        rows = []
        for e in range(e_n):
            cur = cand[e:e + 1, :]
            trial = cur | bit
            cnt = jnp.sum(jnp.where(slab(aff_ref, e) >= pltpu.bitcast(trial, F32), 1.0, 0.0), keepdims=True)
            rows.append(jnp.where(cnt >= capf, trial, cur))
        return jnp.concatenate(rows, axis=0)

    thr = pltpu.bitcast(lax.fori_loop(0, 31, bit_step, jnp.zeros((e_n, LANES), I32)), F32)

    upper = jnp.where(lax.broadcasted_iota(I32, (LANES, LANES), 0) <= lax.broadcasted_iota(I32, (LANES, LANES), 1),
                      1.0, 0.0).astype(BF16)
    strict_lower = jnp.where(lax.broadcasted_iota(I32, (rpe, rpe), 1) < lax.broadcasted_iota(I32, (rpe, rpe), 0),
                             1.0, 0.0).astype(BF16)
    for e in range(e_n):
        aff = slab(aff_ref, e)
        t = thr[e:e + 1, :]
        gt = jnp.where(aff > t, 1.0, 0.0)
        eq = jnp.where(aff == t, 1.0, 0.0)
        need = capf - jnp.sum(gt, keepdims=True)
        rank_eq, _ = _exclusive_prefix(eq, upper, strict_lower)
        sel = gt + eq * jnp.where(rank_eq < need, 1.0, 0.0)
        pos, row_off = _exclusive_prefix(sel, upper, strict_lower)
        chosen = sel > 0.5
        pos_ref[e * rpe:(e + 1) * rpe, :] = jnp.where(chosen, pos, -1.0).astype(I32)
        gate_ref[e * rpe:(e + 1) * rpe, :] = jnp.where(chosen, aff, 0.0)
        lo_ref[e * rpe:(e + 1) * rpe, :] = jnp.broadcast_to(row_off, (rpe, LANES)).astype(I32)


def _select(lg_em, rpe, cap):
    rows = N_EXPERTS * rpe
    shp = lambda dt: jax.ShapeDtypeStruct((rows, LANES), dt)
    return pl.pallas_call(
        functools.partial(_select_body, rpe=rpe, cap=cap),
        out_shape=[shp(I32), shp(F32), shp(I32)],
        scratch_shapes=[pltpu.VMEM((rows, LANES), F32)],
        compiler_params=pltpu.CompilerParams(vmem_limit_bytes=32 * MIB),
        name="select",
    )(lg_em)


def _block_lo(lo_ref, e, blk, rpe):
    return lo_ref[e * rpe + blk * (TOK_BLOCK // LANES)]


def _block_count(lo_ref, e, blk, rpe, n_blk, cap):
    nxt = lo_ref[e * rpe + jnp.minimum((blk + 1) * (TOK_BLOCK // LANES), rpe - 1)]
    return jnp.where(blk == n_blk - 1, cap, nxt) - _block_lo(lo_ref, e, blk, rpe)


def _floor_tile(row):
    return row - (row & (BF16_TILE_ROWS - 1))


def _dispatch_body(lo_ref, pm_ref, h_ref, xe_ref, stage_ref, carry_ref, sem_ref, *, rpe, n_blk, cap):
    b = pl.program_id(0)

    @pl.when(b == 0)
    def _():
        carry_ref[...] = jnp.zeros_like(carry_ref)

    pm = pm_ref[...]
    hb = h_ref[...]
    r_idx = lax.broadcasted_iota(I32, (DISPATCH_WIN, TOK_BLOCK), 0)
    head_row = lax.broadcasted_iota(I32, (BF16_TILE_ROWS, D_MODEL), 0)
    los = [_block_lo(lo_ref, e, b, rpe) for e in range(N_EXPERTS)]
    bases = [_floor_tile(lo) for lo in los]
    onehot = jnp.concatenate(
        [jnp.where(r_idx == pm[e:e + 1, :] - bases[e], 1.0, 0.0) for e in range(N_EXPERTS)], axis=0).astype(BF16)
    xw = jnp.dot(onehot, hb, preferred_element_type=F32)

    def window_copy(e, first_row):
        start = pl.multiple_of(first_row, BF16_TILE_ROWS)
        return pltpu.make_async_copy(stage_ref.at[e], xe_ref.at[e, pl.ds(start, DISPATCH_WIN), :], sem_ref.at[e])

    for e in range(N_EXPERTS):
        @pl.when(b > 0)
        def _():
            window_copy(e, 0).wait()

        rows = xw[e * DISPATCH_WIN:(e + 1) * DISPATCH_WIN, :]
        head = jnp.where(head_row < los[e] - bases[e], carry_ref[e].astype(F32), rows[:BF16_TILE_ROWS, :])
        stage_ref[e, :BF16_TILE_ROWS, :] = head.astype(BF16)
        stage_ref[e, BF16_TILE_ROWS:, :] = rows[BF16_TILE_ROWS:, :].astype(BF16)
        window_copy(e, bases[e]).start()

    ends = [los[e] + _block_count(lo_ref, e, b, rpe, n_blk, cap) for e in range(N_EXPERTS)]
    n_wins = [lax.div(ends[e] - bases[e] + (DISPATCH_WIN - 1), DISPATCH_WIN) for e in range(N_EXPERTS)]

    @pl.when(functools.reduce(jnp.maximum, n_wins) > 1)
    def _():
        for e in range(N_EXPERTS):
            def extra_window(k, carry):
                window_copy(e, 0).wait()
                oh = jnp.where(r_idx + k * DISPATCH_WIN == pm[e:e + 1, :] - bases[e], 1.0, 0.0).astype(BF16)
                stage_ref[e] = jnp.dot(oh, hb, preferred_element_type=F32).astype(BF16)
                window_copy(e, bases[e] + k * DISPATCH_WIN).start()
                return carry

            lax.fori_loop(1, n_wins[e], extra_window, 0)

    for e in range(N_EXPERTS):
        last_start = bases[e] + (jnp.maximum(n_wins[e], 1) - 1) * DISPATCH_WIN
        tile = jnp.minimum(_floor_tile(ends[e]) - last_start, DISPATCH_WIN - BF16_TILE_ROWS)
        carry_ref[e] = stage_ref[e, pl.ds(pl.multiple_of(tile, BF16_TILE_ROWS), BF16_TILE_ROWS), :]

    @pl.when(b == n_blk - 1)
    def _():
        for e in range(N_EXPERTS):
            window_copy(e, 0).wait()
            stage_ref[e] = jnp.zeros((DISPATCH_WIN, D_MODEL), BF16)
            window_copy(e, cap).start()
        for e in range(N_EXPERTS):
            window_copy(e, cap).wait()


def _dispatch(lo, posm, h2, cap):
    t = h2.shape[0]
    rpe = t // LANES
    n_blk = t // TOK_BLOCK
    grid_spec = pltpu.PrefetchScalarGridSpec(
        num_scalar_prefetch=1,
        grid=(n_blk,),
        in_specs=[pl.BlockSpec((N_EXPERTS, TOK_BLOCK), lambda b, lo_r: (0, b)),
                  pl.BlockSpec((TOK_BLOCK, D_MODEL), lambda b, lo_r: (b, 0))],
        out_specs=pl.BlockSpec(memory_space=pl.ANY),
        scratch_shapes=[pltpu.VMEM((N_EXPERTS, DISPATCH_WIN, D_MODEL), BF16),
                        pltpu.VMEM((N_EXPERTS, BF16_TILE_ROWS, D_MODEL), BF16),
                        pltpu.SemaphoreType.DMA((N_EXPERTS,))],
    )
    return pl.pallas_call(
        functools.partial(_dispatch_body, rpe=rpe, n_blk=n_blk, cap=cap),
        grid_spec=grid_spec,
        out_shape=jax.ShapeDtypeStruct((N_EXPERTS, cap + DISPATCH_WIN, D_MODEL), BF16),
        compiler_params=_params(("arbitrary",), 32),
        name="dispatch",
    )(lo, posm, h2)


def _ffn_body(x_ref, wg_ref, wu_ref, wd_ref, y_ref, acc_ref, *, n_f, n_sub, sub):
    f = pl.program_id(1)
    wg = wg_ref[0].astype(BF16)
    wu = wu_ref[0].astype(BF16)
    wd = wd_ref[0].astype(BF16)

    def sweep(first, last):
        def rows_step(i, carry):
            r = pl.ds(pl.multiple_of(i * sub, sub), sub)
            x = x_ref[0, r, :]
            a = jnp.dot(x, wg, preferred_element_type=F32)
            u = jnp.dot(x, wu, preferred_element_type=F32)
            h = (jax.nn.silu(a) * u).astype(BF16)
            d = jnp.dot(h, wd, preferred_element_type=F32)
            if not first:
                d = acc_ref[r, :] + d
            if last:
                y_ref[0, r, :] = d.astype(BF16)
            else:
                acc_ref[r, :] = d
            return carry

        lax.fori_loop(0, n_sub, rows_step, 0)

    if n_f == 1:
        sweep(True, True)
    else:
        pl.when(f == 0)(lambda: sweep(True, False))
        pl.when(jnp.logical_and(f > 0, f < n_f - 1))(lambda: sweep(False, False))
        pl.when(f == n_f - 1)(lambda: sweep(False, True))


def _ffn(xe, w_gate, w_up, w_down, cap, tf=512, sub=1024):
    sub = min(sub, cap)
    n_f = EXPERT_FF // tf
    return pl.pallas_call(
        functools.partial(_ffn_body, n_f=n_f, n_sub=cap // sub, sub=sub),
        grid=(N_EXPERTS, n_f),
        in_specs=[pl.BlockSpec((1, cap, D_MODEL), lambda e, f: (e, 0, 0)),
                  pl.BlockSpec((1, D_MODEL, tf), lambda e, f: (e, 0, f)),
                  pl.BlockSpec((1, D_MODEL, tf), lambda e, f: (e, 0, f)),
                  pl.BlockSpec((1, tf, D_MODEL), lambda e, f: (e, f, 0))],
        out_specs=pl.BlockSpec((1, cap, D_MODEL), lambda e, f: (e, 0, 0)),
        out_shape=jax.ShapeDtypeStruct((N_EXPERTS, cap, D_MODEL), BF16),
        scratch_shapes=[pltpu.VMEM((cap, D_MODEL), F32)],
        compiler_params=_params(("arbitrary", "arbitrary"), 56),
        name="expert_ffn",
    )(xe, w_gate, w_up, w_down)


def _combine_body(lo_ref, pm_ref, gm_ref, x1_ref, nf_ref, y_ref, out_ref,
                  a_ref, m_ref, ovw_ref, sem_ref, ovsem_ref, *, rpe, n_blk, cap):
    WIN = COMBINE_WIN
    b = pl.program_id(0)
    slot = b % 2

    def win_start(e, blk):
        return jnp.minimum(_floor_tile(_block_lo(lo_ref, e, blk, rpe)), cap - WIN)

    def window_copy(e, blk, sl):
        start = pl.multiple_of(win_start(e, blk), BF16_TILE_ROWS)
        return pltpu.make_async_copy(y_ref.at[e, pl.ds(start, WIN), :], a_ref.at[sl, pl.ds(e * WIN, WIN), :],
                                     sem_ref.at[sl, e])

    @pl.when(b == 0)
    def _():
        for e in range(N_EXPERTS):
            window_copy(e, 0, 0).start()

    @pl.when(b + 1 < n_blk)
    def _():
        for e in range(N_EXPERTS):
            window_copy(e, b + 1, 1 - slot).start()

    pm = pm_ref[...]
    gm = gm_ref[...]
    ws0 = [win_start(e, b) for e in range(N_EXPERTS)]
    e_col = lax.broadcasted_iota(I32, (1, N_EXPERTS), 1)
    ws_row = jnp.zeros((1, N_EXPERTS), I32)
    for e in range(N_EXPERTS):
        ws_row = jnp.where(e_col == e, ws0[e], ws_row)
    rel = jnp.clip(pm - ws_row, -1, WIN).astype(F32).astype(BF16)
    spread = jnp.where(lax.broadcasted_iota(I32, (N_EXPERTS, N_EXPERTS * WIN), 1) // WIN
                       == lax.broadcasted_iota(I32, (N_EXPERTS, N_EXPERTS * WIN), 0), 1.0, 0.0).astype(BF16)
    tgt = jnp.dot(rel, spread, preferred_element_type=F32)
    gv = jnp.dot(gm.astype(BF16), spread, preferred_element_type=F32)
    lane_r = (lax.broadcasted_iota(I32, (TOK_BLOCK, N_EXPERTS * WIN), 1) & (WIN - 1)).astype(F32)
    gmat = jnp.where(tgt == lane_r, gv, 0.0).astype(BF16)

    for e in range(N_EXPERTS):
        window_copy(e, b, slot).wait()
    m_ref[...] = jnp.dot(gmat, a_ref[slot], preferred_element_type=F32)

    ends = [_block_lo(lo_ref, e, b, rpe) + _block_count(lo_ref, e, b, rpe, n_blk, cap) for e in range(N_EXPERTS)]
    n_wins = [lax.div(jnp.maximum(ends[e] - ws0[e], 0) + (WIN - 1), WIN) for e in range(N_EXPERTS)]

    @pl.when(functools.reduce(jnp.maximum, n_wins) > 1)
    def _():
        lane_w = lax.broadcasted_iota(I32, (TOK_BLOCK, WIN), 1)
        for e in range(N_EXPERTS):
            def extra_window(k, carry):
                first = ws0[e] + k * WIN
                ws = jnp.minimum(first, cap - WIN)
                cp = pltpu.make_async_copy(y_ref.at[e, pl.ds(pl.multiple_of(ws, BF16_TILE_ROWS), WIN), :], ovw_ref,
                                           ovsem_ref)
                cp.start()
                cp.wait()
                pe = pm[:, e:e + 1]
                gk = jnp.where((lane_w == pe - ws) & (pe >= first), gm[:, e:e + 1], 0.0).astype(BF16)
                m_ref[...] += jnp.dot(gk, ovw_ref[...], preferred_element_type=F32)
                return carry

            lax.fori_loop(1, n_wins[e], extra_window, 0)

    xo = x1_ref[...] + m_ref[...]
    out_ref[...] = (xo * lax.rsqrt(jnp.mean(xo * xo, axis=-1, keepdims=True) + RMS_EPS)) * nf_ref[...]


def _combine(lo, posm_t, gate_t, x1, norm_final, y2, cap):
    t = x1.shape[0]
    rpe = t // LANES
    n_blk = t // TOK_BLOCK
    grid_spec = pltpu.PrefetchScalarGridSpec(
        num_scalar_prefetch=1,
        grid=(n_blk,),
        in_specs=[pl.BlockSpec((TOK_BLOCK, N_EXPERTS), lambda b, lo_r: (b, 0)),
                  pl.BlockSpec((TOK_BLOCK, N_EXPERTS), lambda b, lo_r: (b, 0)),
                  pl.BlockSpec((TOK_BLOCK, D_MODEL), lambda b, lo_r: (b, 0)),
                  pl.BlockSpec((1, D_MODEL), lambda b, lo_r: (0, 0)),
                  pl.BlockSpec(memory_space=pl.ANY)],
        out_specs=pl.BlockSpec((TOK_BLOCK, D_MODEL), lambda b, lo_r: (b, 0)),
        scratch_shapes=[pltpu.VMEM((2, N_EXPERTS * COMBINE_WIN, D_MODEL), BF16),
                        pltpu.VMEM((TOK_BLOCK, D_MODEL), F32),
                        pltpu.VMEM((COMBINE_WIN, D_MODEL), BF16),
                        pltpu.SemaphoreType.DMA((2, N_EXPERTS)),
                        pltpu.SemaphoreType.DMA(())],
    )
    return pl.pallas_call(
        functools.partial(_combine_body, rpe=rpe, n_blk=n_blk, cap=cap),
        grid_spec=grid_spec,
        out_shape=jax.ShapeDtypeStruct((t, D_MODEL), F32),
        compiler_params=_params(("arbitrary",), 40),
        name="combine",
    )(lo, posm_t, gate_t, x1, norm_final, y2)


def _trunk(x, p):
    b, l, _ = x.shape
    t = b * l
    n_seq = l // S5_CHUNK
    cap = max(1, EC_CAPACITY_FACTOR * t // N_EXPERTS)
    rpe = t // LANES

    hg, u, u_sm, gates = _inproj(x.reshape(t, D_MODEL), p['norm_mix'], p['w_in'])
    o_fw, o_bw = _hgrn(hg.reshape(b, l, 5 * A_WIDTH), p['lb'])
    ys5 = _s5(u_sm, p['s5_perm'], *p['s5'], n_seq)

    x1, h2, logits = _merge(x.reshape(t, D_MODEL), o_fw.reshape(t, A_WIDTH), o_bw.reshape(t, A_WIDTH), hg, ys5, u,
                            gates, p['hgrn_norm'], p['s5_d'], p['w_glu'], p['b_glu'], p['w_a'], p['w_b'],
                            p['w_out'], p['norm_ffn'], p['w_router'])

    posm, gate, lo = _select(logits.T.reshape(N_EXPERTS * rpe, LANES), rpe, cap)
    lo = lo[:, 0]
    posm = posm.reshape(N_EXPERTS, t)
    xe = _dispatch(lo, posm, h2, cap)
    y2 = _ffn(xe, p['w_gate'], p['w_up'], p['w_down'], cap)
    out = _combine(lo, posm.T, gate.reshape(N_EXPERTS, t).T, x1, p['norm_final'], y2, cap)
    return out.reshape(b, l, D_MODEL)


def kernel(x_prompt, x_sample, norm_mix, w_in, hgrn_gamma, hgrn_norm, s5_a_re, s5_a_im, s5_log_dt, s5_b_re, s5_b_im, s5_c_re, s5_c_im, s5_d, s5_w_glu, s5_b_glu, w_branch_a, w_branch_b, w_out, norm_ffn, w_router, w_exp_gate, w_exp_up, w_exp_down, norm_final):
    assert norm_mix.shape[0] == 1, "single-layer trunk"
    row = lambda a: a.astype(F32).reshape(1, -1)
    lb = jnp.cumsum(jax.nn.softmax(hgrn_gamma.astype(F32), axis=0), axis=0)[0]
    max_chunks = max(x_prompt.shape[1], x_sample.shape[1]) // S5_CHUNK
    s5_ops = _s5_operators(s5_a_re[0], s5_a_im[0], s5_log_dt[0], s5_b_re[0], s5_b_im[0], s5_c_re[0], s5_c_im[0],
                           max_chunks.bit_length() - 1)
    p = {
        'norm_mix': row(norm_mix[0]), 'w_in': w_in[0].astype(BF16), 'lb': lb, 's5': s5_ops,
        's5_perm': _s5_lane_permutation(),
        'hgrn_norm': row(hgrn_norm[0]), 's5_d': row(s5_d[0]), 'w_glu': s5_w_glu[0].astype(BF16),
        'b_glu': row(s5_b_glu[0]), 'w_a': w_branch_a[0].astype(BF16), 'w_b': w_branch_b[0].astype(BF16),
        'w_out': w_out[0].astype(BF16), 'norm_ffn': row(norm_ffn[0]),
        'w_router': _split_router(w_router[0]),
        'w_gate': w_exp_gate[0], 'w_up': w_exp_up[0], 'w_down': w_exp_down[0],
        'norm_final': row(norm_final),
    }
    return (_trunk(x_prompt, p), _trunk(x_sample, p))
```

```python
import functools
import math

import jax
import jax.numpy as jnp
from jax import lax
from jax.experimental import pallas as pl
from jax.experimental.pallas import tpu as pltpu

F32 = jnp.float32
BF16 = jnp.bfloat16
I32 = jnp.int32

D_MODEL = 1024
A_HEADS = 4
A_HEAD_DIM = 128
A_WIDTH = A_HEADS * A_HEAD_DIM
A_CHUNK = 64
S5_WIDTH = 512
S5_GROUP = 16
S5_GROUPS = S5_WIDTH // S5_GROUP
S5_STATE = 64
N_EXPERTS = 16
EXPERT_FF = 2048
EC_CAPACITY_FACTOR = 2
RMS_EPS = 1e-6
IN_COLS = 5 * A_WIDTH + S5_WIDTH + 2 * D_MODEL

LANES = 128
BF16_TILE_ROWS = 16
S5_CHUNK = 16
S5_COLS = S5_CHUNK * S5_GROUP
S5_GROUPS_PER_STEP = LANES // S5_GROUP
TOK_BLOCK = 256
COMBINE_WIN = 64
DISPATCH_WIN = 64
MIB = 2 ** 20


def _params(semantics, vmem_mib):
    return pltpu.CompilerParams(dimension_semantics=semantics, vmem_limit_bytes=vmem_mib * MIB)


def _inproj_body(x_ref, g_ref, w_ref, hg_ref, u_ref, us_ref, gt_ref, uf_ref, *, tm):
    x = x_ref[...]
    ms = jnp.mean(x * x, axis=-1, keepdims=True)
    h = ((x * lax.rsqrt(ms + RMS_EPS)) * g_ref[...]).astype(BF16)
    n_hg = 5 * A_WIDTH // 512
    for c in range(IN_COLS // 512):
        r = jnp.dot(h, w_ref[:, c * 512:(c + 1) * 512], preferred_element_type=F32)
        if c < n_hg:
            hg_ref[:, c * 512:(c + 1) * 512] = r.astype(BF16)
        elif c == n_hg:
            u_ref[...] = r.astype(BF16)
            for cb in range(S5_WIDTH // LANES):
                uf_ref[cb] = r[:, cb * LANES:(cb + 1) * LANES]
                for sg in range(S5_CHUNK):
                    us_ref[sg, :, cb * LANES:(cb + 1) * LANES] = uf_ref[
                        cb, pl.ds(sg, tm // S5_CHUNK, stride=S5_CHUNK), :].astype(BF16)
        else:
            gt_ref[:, (c - n_hg - 1) * 512:(c - n_hg) * 512] = r.astype(BF16)


def _inproj(x, gain, w_bf16, tm=512):
    t = x.shape[0]
    return pl.pallas_call(
        functools.partial(_inproj_body, tm=tm),
        grid=(t // tm,),
        in_specs=[pl.BlockSpec((tm, D_MODEL), lambda i: (i, 0)),
                  pl.BlockSpec((1, D_MODEL), lambda i: (0, 0)),
                  pl.BlockSpec((D_MODEL, IN_COLS), lambda i: (0, 0))],
        out_specs=[pl.BlockSpec((tm, 5 * A_WIDTH), lambda i: (i, 0)),
                   pl.BlockSpec((tm, S5_WIDTH), lambda i: (i, 0)),
                   pl.BlockSpec((S5_CHUNK, tm // S5_CHUNK, S5_WIDTH), lambda i: (0, i, 0)),
                   pl.BlockSpec((tm, 2 * D_MODEL), lambda i: (i, 0))],
        out_shape=[jax.ShapeDtypeStruct((t, 5 * A_WIDTH), BF16),
                   jax.ShapeDtypeStruct((t, S5_WIDTH), BF16),
                   jax.ShapeDtypeStruct((S5_CHUNK, t // S5_CHUNK, S5_WIDTH), BF16),
                   jax.ShapeDtypeStruct((t, 2 * D_MODEL), BF16)],
        scratch_shapes=[pltpu.VMEM((S5_WIDTH // LANES, tm, LANES), F32)],
        compiler_params=_params(("arbitrary",), 48),
        name="inproj",
    )(x, gain, w_bf16)


def _hgrn_pair(q, fr, v, lb, s_ref, forward):
    c, c2 = A_CHUNK, 2 * A_CHUNK
    t_idx = lax.broadcasted_iota(I32, (c2, c2), 0)
    s_idx = lax.broadcasted_iota(I32, (c2, c2), 1)
    one = lambda cond: jnp.where(cond, 1.0, 0.0)
    s_loc = s_idx & (c - 1)
    same_chunk = (t_idx < c) == (s_idx < c)
    tri = same_chunk & ((s_idx <= t_idx) if forward else (s_idx >= t_idx))
    m_run = one(tri)
    m_mid = one(same_chunk & ((s_loc <= c // 2 - 1) if forward else (s_loc >= c // 2)))
    m_all = one(same_chunk)
    order = (0, 1) if forward else (1, 0)

    q = q.astype(F32)
    f = lb + (1.0 - lb) * jax.nn.sigmoid(fr.astype(F32))
    lf = jnp.log2(f)
    kk = 1.0 - f
    hi = lf.astype(BF16)
    hl = jnp.concatenate([hi, (lf - hi.astype(F32)).astype(BF16)], axis=0)

    def sums(m):
        mb = m.astype(BF16)
        return jnp.dot(jnp.concatenate([mb, mb], axis=1), hl, preferred_element_type=F32)

    e_mid = jnp.exp2(sums(m_run - m_mid))
    q_in = (q * e_mid).astype(BF16)
    k_in = kk * (1.0 / e_mid)
    qdec = (q * jnp.exp2(sums(m_run))).astype(BF16)
    kdec = kk * jnp.exp2(sums(m_all - m_run))
    total = [sums(one(jnp.broadcast_to((s_idx < c) == (ch == 0), (c2, c2)))) for ch in (0, 1)]
    zero = jnp.zeros((), BF16)
    outs = []
    for h in range(A_HEADS):
        sl = slice(h * A_HEAD_DIM, (h + 1) * A_HEAD_DIM)
        vh = v[:, sl]
        k_t = k_in[:, sl].T.astype(BF16)
        kd_t = kdec[:, sl].T.astype(BF16)
        s = jnp.where(tri, jnp.dot(q_in[:, sl], k_t, preferred_element_type=F32), 0.0).astype(BF16)
        o_intra = jnp.dot(s, vh, preferred_element_type=F32)
        state = s_ref[h]
        o_inter = [None, None]
        for ch in order:
            rows = slice(ch * c, (ch + 1) * c)
            o_inter[ch] = jnp.dot(qdec[rows, sl], state.astype(BF16), preferred_element_type=F32)
            own_rows = (t_idx < c) if ch == 0 else (t_idx >= c)
            v_ch = jnp.where(own_rows, vh, zero)
            decay = jnp.exp2(total[ch][:, sl].T)
            state = decay * state + jnp.dot(kd_t, v_ch, preferred_element_type=F32)
        s_ref[h] = state
        outs.append(o_intra + jnp.concatenate(o_inter, axis=0))
    return jnp.concatenate(outs, axis=1).astype(BF16)


def _hgrn_body(qf_ref, ff_ref, vf_ref, qb_ref, fb_ref, vb_ref, lb_ref, of_ref, ob_ref, sf_ref, sb_ref,
               *, n_pairs):
    @pl.when(pl.program_id(1) == 0)
    def _():
        sf_ref[...] = jnp.zeros_like(sf_ref)
        sb_ref[...] = jnp.zeros_like(sb_ref)

    c2 = 2 * A_CHUNK

    def step(j, carry):
        rf = pl.ds(pl.multiple_of(j * c2, c2), c2)
        of_ref[0, rf, :] = _hgrn_pair(qf_ref[0, rf, :], ff_ref[0, rf, :], vf_ref[0, rf, :],
                                      lb_ref[0:1, :], sf_ref, True)
        rb = pl.ds(pl.multiple_of((n_pairs - 1 - j) * c2, c2), c2)
        ob_ref[0, rb, :] = _hgrn_pair(qb_ref[0, rb, :], fb_ref[0, rb, :], vb_ref[0, rb, :],
                                      lb_ref[1:2, :], sb_ref, False)
        return carry

    lax.fori_loop(0, n_pairs, step, 0, unroll=True)


def _hgrn(hg, lb, lb_tokens=512):
    b, l, _ = hg.shape
    nb = l // lb_tokens
    blk = (1, lb_tokens, A_WIDTH)
    fwd = lambda col: pl.BlockSpec(blk, lambda bi, i: (bi, i, col))
    bwd = lambda col: pl.BlockSpec(blk, lambda bi, i: (bi, nb - 1 - i, col))
    return pl.pallas_call(
        functools.partial(_hgrn_body, n_pairs=lb_tokens // (2 * A_CHUNK)),
        grid=(b, nb),
        in_specs=[fwd(0), fwd(1), fwd(3), bwd(0), bwd(2), bwd(3),
                  pl.BlockSpec((2, A_WIDTH), lambda bi, i: (0, 0))],
        out_specs=[pl.BlockSpec(blk, lambda bi, i: (bi, i, 0)),
                   pl.BlockSpec(blk, lambda bi, i: (bi, nb - 1 - i, 0))],
        out_shape=[jax.ShapeDtypeStruct((b, l, A_WIDTH), BF16)] * 2,
        scratch_shapes=[pltpu.VMEM((A_HEADS, A_HEAD_DIM, A_HEAD_DIM), F32)] * 2,
        compiler_params=_params(("arbitrary", "arbitrary"), 32),
        name="hgrn2",
    )(hg, hg, hg, hg, hg, hg, lb)


def _s5_direction_terms(a_re, a_im, log_dt, b_re, b_im):
    a_re, a_im = a_re.astype(F32), a_im.astype(F32)
    b_re, b_im = b_re.astype(F32), b_im.astype(F32)
    dt = jnp.exp(log_dt.astype(F32))[:, None]
    j = jnp.arange(S5_CHUNK + 1, dtype=F32)[:, None, None]
    mag = jnp.exp(j * (dt * a_re))
    pw_re = mag * jnp.cos(j * (dt * a_im))
    pw_im = mag * jnp.sin(j * (dt * a_im))
    nr, ni = pw_re[1] - 1.0, pw_im[1]
    den = a_re * a_re + a_im * a_im
    f_re = (nr * a_re + ni * a_im) / den
    f_im = (ni * a_re - nr * a_im) / den
    bb_re = f_re[..., None] * b_re - f_im[..., None] * b_im
    bb_im = f_re[..., None] * b_im + f_im[..., None] * b_re
    pb_re = pw_re[..., None] * bb_re - pw_im[..., None] * bb_im
    pb_im = pw_re[..., None] * bb_im + pw_im[..., None] * bb_re
    return pw_re, pw_im, pb_re, pb_im


def _s5_assemble_body(strip_ref, inf_ref, inb_ref, outf_ref, outb_ref, w1_ref, w2_ref):
    c, half = S5_CHUNK, 2 * S5_STATE
    for g in range(strip_ref.shape[0]):
        strip = strip_ref[g]
        for s in range(c):
            lo = (c - 1 - s) * S5_GROUP
            w1_ref[g, s * S5_GROUP:(s + 1) * S5_GROUP, :S5_COLS] = strip[:, lo:lo + S5_COLS].astype(BF16)
        w1_ref[g, :, S5_COLS:S5_COLS + half] = inf_ref[g].astype(BF16)
        w1_ref[g, :, S5_COLS + half:] = inb_ref[g].astype(BF16)
        w2_ref[g, :half, :] = outf_ref[g].astype(BF16)
        w2_ref[g, half:, :] = outb_ref[g].astype(BF16)


def _s5_assemble(strip, w_in_f, w_in_b, w_out_f, w_out_b):
    g = strip.shape[0]
    gps = S5_GROUPS_PER_STEP
    per_g = lambda a: pl.BlockSpec((gps,) + a.shape[1:], lambda i: (i, 0, 0))
    return pl.pallas_call(
        _s5_assemble_body,
        grid=(g // gps,),
        in_specs=[per_g(strip), per_g(w_in_f), per_g(w_in_b), per_g(w_out_f), per_g(w_out_b)],
        out_specs=[pl.BlockSpec((gps, S5_COLS, 2 * S5_COLS), lambda i: (i, 0, 0)),
                   pl.BlockSpec((gps, S5_COLS, S5_COLS), lambda i: (i, 0, 0))],
        out_shape=[jax.ShapeDtypeStruct((g, S5_COLS, 2 * S5_COLS), BF16),
                   jax.ShapeDtypeStruct((g, S5_COLS, S5_COLS), BF16)],
        compiler_params=_params(("arbitrary",), 32),
        name="s5_assemble",
    )(strip, w_in_f, w_in_b, w_out_f, w_out_b)


def _s5_operators(a_re, a_im, log_dt, b_re, b_im, c_re, c_im, n_steps):
    hp = lax.Precision.HIGHEST
    c_re, c_im = c_re.astype(F32), c_im.astype(F32)
    g = S5_GROUPS
    terms = [_s5_direction_terms(a_re[d], a_im[d], log_dt[d], b_re, b_im) for d in range(2)]
    taps = []
    for pw_re, pw_im, pb_re, pb_im in terms:
        k = (jnp.einsum('gop,jgpi->jgoi', c_re, pb_re[:S5_CHUNK], precision=hp)
             - jnp.einsum('gop,jgpi->jgoi', c_im, pb_im[:S5_CHUNK], precision=hp))
        taps.append(k)
    c = S5_CHUNK
    strip = jnp.concatenate([taps[1][:0:-1], (taps[0][0] + taps[1][0])[None], taps[0][1:]], axis=0)
    strip = strip.transpose(1, 3, 0, 2).reshape(g, S5_GROUP, (2 * c - 1) * S5_GROUP)

    def w_in(pb_re, pb_im):
        re = pb_re.transpose(1, 0, 3, 2)
        im = pb_im.transpose(1, 0, 3, 2)
        return jnp.concatenate([re, im], axis=-1).reshape(g, S5_COLS, 2 * S5_STATE)

    def w_out(pr, pi):
        m_re = c_re[None] * pr[:, :, None, :] - c_im[None] * pi[:, :, None, :]
        m_im = c_re[None] * pi[:, :, None, :] + c_im[None] * pr[:, :, None, :]
        re = m_re.transpose(1, 3, 0, 2).reshape(g, S5_STATE, S5_COLS)
        im = -m_im.transpose(1, 3, 0, 2).reshape(g, S5_STATE, S5_COLS)
        return jnp.concatenate([re, im], axis=1)

    w_in_f = w_in(terms[0][2][c - 1::-1], terms[0][3][c - 1::-1])
    w_in_b = w_in(terms[1][2][:c], terms[1][3][:c])
    w_out_f = w_out(terms[0][0][1:], terms[0][1][1:])
    w_out_b = w_out(terms[1][0][:0:-1], terms[1][1][:0:-1])
    w1, w2 = _s5_assemble(strip, w_in_f, w_in_b, w_out_f, w_out_b)

    ars, ais = [], []
    cur = [(t[0][S5_CHUNK], t[1][S5_CHUNK]) for t in terms]
    for _ in range(n_steps):
        ars.append(jnp.concatenate([cur[0][0], cur[0][0], cur[1][0], cur[1][0]], axis=-1))
        ais.append(jnp.concatenate([-cur[0][1], cur[0][1], -cur[1][1], cur[1][1]], axis=-1))
        cur = [(r * r - i * i, 2.0 * r * i) for r, i in cur]
    return w1, w2, jnp.stack(ars, axis=1), jnp.stack(ais, axis=1)


def _s5_group(u, w1, w2, ar, ai, n, n_seq, n_steps):
    rows = u.shape[0]
    half = 2 * S5_STATE
    yv = jnp.dot(u, w1, preferred_element_type=F32)
    zf = jnp.where(n >= 1, pltpu.roll(yv[:, S5_COLS:S5_COLS + half], 1, axis=0), 0.0)
    zb = jnp.where(n < n_seq - 1, pltpu.roll(yv[:, S5_COLS + half:], rows - 1, axis=0), 0.0)
    for k in range(n_steps):
        s = 1 << k
        sf = jnp.where(n >= s, pltpu.roll(zf, s, axis=0), 0.0)
        sb = jnp.where(n < n_seq - s, pltpu.roll(zb, rows - s, axis=0), 0.0)
        zf = zf + ar[k:k + 1, :half] * sf + ai[k:k + 1, :half] * pltpu.roll(sf, S5_STATE, axis=1)
        zb = zb + ar[k:k + 1, half:] * sb + ai[k:k + 1, half:] * pltpu.roll(sb, S5_STATE, axis=1)
    x = jnp.concatenate([zf, zb], axis=1).astype(BF16)
    return yv[:, :S5_COLS] + jnp.dot(x, w2, preferred_element_type=F32)


def _s5_body(u_ref, perm_ref, w1_ref, w2_ref, ar_ref, ai_ref, y_ref, z_ref, *, n_seq, n_steps):
    gps = S5_GROUPS_PER_STEP
    rows = u_ref.shape[1]
    n = lax.broadcasted_iota(I32, (rows, 2 * S5_STATE), 0) & (n_seq - 1)
    perm = perm_ref[...]
    for hf in range(S5_CHUNK // gps):
        x = jnp.concatenate([u_ref[hf * gps + j] for j in range(gps)], axis=1)
        z_ref[:, hf * gps * LANES:(hf + 1) * gps * LANES] = jnp.dot(
            x, perm, preferred_element_type=F32).astype(BF16)
    for g in range(gps):
        u = jnp.concatenate([z_ref[:, (hf * gps + g) * LANES:(hf * gps + g + 1) * LANES]
                             for hf in range(S5_CHUNK // gps)], axis=1)
        y = _s5_group(u, w1_ref[g], w2_ref[g], ar_ref[g], ai_ref[g], n, n_seq, n_steps).astype(BF16)
        for hf in range(S5_CHUNK // gps):
            z_ref[:, (hf * gps + g) * LANES:(hf * gps + g + 1) * LANES] = y[:, hf * LANES:(hf + 1) * LANES]
    for hf in range(S5_CHUNK // gps):
        x = jnp.dot(z_ref[:, hf * gps * LANES:(hf + 1) * gps * LANES], perm,
                    preferred_element_type=F32).astype(BF16)
        for j in range(gps):
            y_ref[hf * gps + j] = x[:, j * LANES:(j + 1) * LANES]


def _s5_lane_permutation():
    idx = jnp.arange(S5_GROUPS_PER_STEP * LANES)
    a, b, c = idx // LANES, (idx % LANES) // S5_GROUP, idx % S5_GROUP
    dst = b * LANES + a * S5_GROUP + c
    return (dst[:, None] == idx[None, :]).astype(BF16)


def _s5(us, perm, w1, w2, ar, ai, n_seq):
    _, rows, _ = us.shape
    assert n_seq & (n_seq - 1) == 0, n_seq
    n_steps = n_seq.bit_length() - 1
    gps = S5_GROUPS_PER_STEP
    tok = pl.BlockSpec((S5_CHUNK, rows, LANES), lambda i: (0, 0, i))
    per_g = lambda a: pl.BlockSpec((gps,) + a.shape[1:], lambda i: (i, 0, 0))
    return pl.pallas_call(
        functools.partial(_s5_body, n_seq=n_seq, n_steps=n_steps),
        grid=(S5_GROUPS // gps,),
        in_specs=[tok, pl.BlockSpec(perm.shape, lambda i: (0, 0)), per_g(w1), per_g(w2), per_g(ar), per_g(ai)],
        out_specs=tok,
        out_shape=jax.ShapeDtypeStruct(us.shape, BF16),
        scratch_shapes=[pltpu.VMEM((rows, S5_CHUNK * LANES), BF16)],
        compiler_params=_params(("arbitrary",), 48),
        name="s5",
    )(us, perm, w1, w2, ar, ai)


def _gelu_tanh(x):
    return 0.5 * x * (1.0 + jnp.tanh(math.sqrt(2.0 / math.pi) * (x + 0.044715 * (x * x * x))))


def _merge_body(x_ref, of_ref, ob_ref, go_ref, ys_ref, u_ref, ga_ref, gb_ref,
                hn_ref, sd_ref, wglu_ref, bglu_ref, wa_ref, wb_ref, wo_ref, nf_ref, wr_ref,
                x1_ref, h2_ref, lg_ref, yn_ref, *, tm):
    for sg in range(S5_CHUNK):
        for cb in range(S5_WIDTH // LANES):
            yn_ref[cb, pl.ds(sg, tm // S5_CHUNK, stride=S5_CHUNK), :] = ys_ref[
                sg, :, cb * LANES:(cb + 1) * LANES].astype(F32)
    o = of_ref[...].astype(F32) + ob_ref[...].astype(F32)
    parts = []
    for h in range(A_HEADS):
        oh = o[:, h * A_HEAD_DIM:(h + 1) * A_HEAD_DIM]
        parts.append(oh * lax.rsqrt(jnp.mean(oh * oh, axis=-1, keepdims=True) + RMS_EPS))
    h_a = jnp.concatenate(parts, axis=1) * hn_ref[...] * jax.nn.silu(go_ref[...].astype(F32))

    u = u_ref[...].astype(F32)
    y_s5 = jnp.concatenate([yn_ref[cb] for cb in range(S5_WIDTH // LANES)], axis=1)
    ys = _gelu_tanh(y_s5 + sd_ref[...] * u)
    z = jnp.dot(ys.astype(BF16), wglu_ref[...], preferred_element_type=F32) + bglu_ref[...]
    h_b = ys * jax.nn.sigmoid(z)

    y_a = jnp.dot(h_a.astype(BF16), wa_ref[...], preferred_element_type=F32)
    y_b = jnp.dot(h_b.astype(BF16), wb_ref[...], preferred_element_type=F32)
    merged = jax.nn.sigmoid(ga_ref[...].astype(F32)) * y_a + jax.nn.sigmoid(gb_ref[...].astype(F32)) * y_b
    x1 = x_ref[...] + jnp.dot(merged.astype(BF16), wo_ref[...], preferred_element_type=F32)
    x1_ref[...] = x1

    h2 = (x1 * lax.rsqrt(jnp.mean(x1 * x1, axis=-1, keepdims=True) + RMS_EPS)) * nf_ref[...]
    h2_hi = h2.astype(BF16)
    h2_ref[...] = h2_hi
    h2_lo = (h2 - h2_hi.astype(F32)).astype(BF16)
    r_hi = jnp.dot(h2_hi, wr_ref[...], preferred_element_type=F32)
    r_lo = jnp.dot(h2_lo, wr_ref[...], preferred_element_type=F32)
    lg_ref[...] = r_hi[:, :N_EXPERTS] + r_hi[:, N_EXPERTS:2 * N_EXPERTS] + r_lo[:, :N_EXPERTS]


def _split_router(w_router):
    w = w_router.astype(F32)
    hi = w.astype(BF16)
    lo = (w - hi.astype(F32)).astype(BF16)
    pad = jnp.zeros((w.shape[0], LANES - 2 * N_EXPERTS), BF16)
    return jnp.concatenate([hi, lo, pad], axis=1)


def _merge(x, o_fw, o_bw, hg, ys5, u, gates, hn, sd, wglu, bglu, wa, wb, wo, nf, wr_pad, tm=512):
    t = x.shape[0]
    row = lambda w, col=0: pl.BlockSpec((tm, w), lambda i: (i, col))
    full = lambda a: pl.BlockSpec(a.shape, lambda i: (0,) * a.ndim)
    return pl.pallas_call(
        functools.partial(_merge_body, tm=tm),
        grid=(t // tm,),
        in_specs=[row(D_MODEL), row(A_WIDTH), row(A_WIDTH), row(A_WIDTH, 4),
                  pl.BlockSpec((S5_CHUNK, tm // S5_CHUNK, S5_WIDTH), lambda i: (0, i, 0)), row(S5_WIDTH),
                  row(D_MODEL, 0), row(D_MODEL, 1),
                  full(hn), full(sd), full(wglu), full(bglu), full(wa), full(wb), full(wo), full(nf), full(wr_pad)],
        out_specs=[row(D_MODEL), row(D_MODEL), row(N_EXPERTS)],
        out_shape=[jax.ShapeDtypeStruct((t, D_MODEL), F32),
                   jax.ShapeDtypeStruct((t, D_MODEL), BF16),
                   jax.ShapeDtypeStruct((t, N_EXPERTS), F32)],
        scratch_shapes=[pltpu.VMEM((S5_WIDTH // LANES, tm, LANES), F32)],
        compiler_params=_params(("arbitrary",), 48),
        name="merge",
    )(x, o_fw, o_bw, hg, ys5, u, gates, gates, hn, sd, wglu, bglu, wa, wb, wo, nf, wr_pad)


def _exclusive_prefix(m, upper, strict_lower):
    mb = m.astype(BF16)
    incl = jnp.dot(mb, upper, preferred_element_type=F32)
    row_off = jnp.sum(jnp.dot(strict_lower, mb, preferred_element_type=F32), axis=1, keepdims=True)
    return incl - m + row_off, row_off


def _select_body(lg_ref, pos_ref, gate_ref, lo_ref, aff_ref, *, rpe, cap):
    e_n = N_EXPERTS
    slab = lambda ref, e: ref[e * rpe:(e + 1) * rpe, :]
    mx = slab(lg_ref, 0)
    for e in range(1, e_n):
        mx = jnp.maximum(mx, slab(lg_ref, e))
    den = jnp.zeros_like(mx)
    for e in range(e_n):
        ex = jnp.exp(slab(lg_ref, e) - mx)
        aff_ref[e * rpe:(e + 1) * rpe, :] = ex
        den = den + ex
    for e in range(e_n):
        aff_ref[e * rpe:(e + 1) * rpe, :] = slab(aff_ref, e) / den

    capf = float(cap)

    def bit_step(i, cand):
        bit = jnp.left_shift(jnp.int32(1), 30 - i)
        rows = []
        for e in range(e_n):
            cur = cand[e:e + 1, :]
            trial = cur | bit
            cnt = jnp.sum(jnp.where(slab(aff_ref, e) >= pltpu.bitcast(trial, F32), 1.0, 0.0), keepdims=True)
            rows.append(jnp.where(cnt >= capf, trial, cur))
        return jnp.concatenate(rows, axis=0)

    thr = pltpu.bitcast(lax.fori_loop(0, 31, bit_step, jnp.zeros((e_n, LANES), I32)), F32)

    upper = jnp.where(lax.broadcasted_iota(I32, (LANES, LANES), 0) <= lax.broadcasted_iota(I32, (LANES, LANES), 1),
                      1.0, 0.0).astype(BF16)
    strict_lower = jnp.where(lax.broadcasted_iota(I32, (rpe, rpe), 1) < lax.broadcasted_iota(I32, (rpe, rpe), 0),
                             1.0, 0.0).astype(BF16)
    for e in range(e_n):
        aff = slab(aff_ref, e)
        t = thr[e:e + 1, :]
        gt = jnp.where(aff > t, 1.0, 0.0)
        eq = jnp.where(aff == t, 1.0, 0.0)
        need = capf - jnp.sum(gt, keepdims=True)
        rank_eq, _ = _exclusive_prefix(eq, upper, strict_lower)
        sel = gt + eq * jnp.where(rank_eq < need, 1.0, 0.0)
        pos, row_off = _exclusive_prefix(sel, upper, strict_lower)
        chosen = sel > 0.5
        pos_ref[e * rpe:(e + 1) * rpe, :] = jnp.where(chosen, pos, -1.0).astype(I32)
        gate_ref[e * rpe:(e + 1) * rpe, :] = jnp.where(chosen, aff, 0.0)
        lo_ref[e * rpe:(e + 1) * rpe, :] = jnp.broadcast_to(row_off, (rpe, LANES)).astype(I32)


def _select(lg_em, rpe, cap):
    rows = N_EXPERTS * rpe
    shp = lambda dt: jax.ShapeDtypeStruct((rows, LANES), dt)
    return pl.pallas_call(
        functools.partial(_select_body, rpe=rpe, cap=cap),
        out_shape=[shp(I32), shp(F32), shp(I32)],
        scratch_shapes=[pltpu.VMEM((rows, LANES), F32)],
        compiler_params=pltpu.CompilerParams(vmem_limit_bytes=32 * MIB),
        name="select",
    )(lg_em)


def _block_lo(lo_ref, e, blk, rpe):
    return lo_ref[e * rpe + blk * (TOK_BLOCK // LANES)]


def _block_count(lo_ref, e, blk, rpe, n_blk, cap):
    nxt = lo_ref[e * rpe + jnp.minimum((blk + 1) * (TOK_BLOCK // LANES), rpe - 1)]
    return jnp.where(blk == n_blk - 1, cap, nxt) - _block_lo(lo_ref, e, blk, rpe)


def _floor_tile(row):
    return row - (row & (BF16_TILE_ROWS - 1))


def _dispatch_body(lo_ref, pm_ref, h_ref, xe_ref, stage_ref, carry_ref, sem_ref, *, rpe, n_blk, cap):
    b = pl.program_id(0)

    @pl.when(b == 0)
    def _():
        carry_ref[...] = jnp.zeros_like(carry_ref)

    pm = pm_ref[...]
    hb = h_ref[...]
    r_idx = lax.broadcasted_iota(I32, (DISPATCH_WIN, TOK_BLOCK), 0)
    head_row = lax.broadcasted_iota(I32, (BF16_TILE_ROWS, D_MODEL), 0)
    los = [_block_lo(lo_ref, e, b, rpe) for e in range(N_EXPERTS)]
    bases = [_floor_tile(lo) for lo in los]
    onehot = jnp.concatenate(
        [jnp.where(r_idx == pm[e:e + 1, :] - bases[e], 1.0, 0.0) for e in range(N_EXPERTS)], axis=0).astype(BF16)
    xw = jnp.dot(onehot, hb, preferred_element_type=F32)

    def window_copy(e, first_row):
        start = pl.multiple_of(first_row, BF16_TILE_ROWS)
        return pltpu.make_async_copy(stage_ref.at[e], xe_ref.at[e, pl.ds(start, DISPATCH_WIN), :], sem_ref.at[e])

    for e in range(N_EXPERTS):
        @pl.when(b > 0)
        def _():
            window_copy(e, 0).wait()

        rows = xw[e * DISPATCH_WIN:(e + 1) * DISPATCH_WIN, :]
        head = jnp.where(head_row < los[e] - bases[e], carry_ref[e].astype(F32), rows[:BF16_TILE_ROWS, :])
        stage_ref[e, :BF16_TILE_ROWS, :] = head.astype(BF16)
        stage_ref[e, BF16_TILE_ROWS:, :] = rows[BF16_TILE_ROWS:, :].astype(BF16)
        window_copy(e, bases[e]).start()

    ends = [los[e] + _block_count(lo_ref, e, b, rpe, n_blk, cap) for e in range(N_EXPERTS)]
    n_wins = [lax.div(ends[e] - bases[e] + (DISPATCH_WIN - 1), DISPATCH_WIN) for e in range(N_EXPERTS)]

    @pl.when(functools.reduce(jnp.maximum, n_wins) > 1)
    def _():
        for e in range(N_EXPERTS):
            def extra_window(k, carry):
                window_copy(e, 0).wait()
                oh = jnp.where(r_idx + k * DISPATCH_WIN == pm[e:e + 1, :] - bases[e], 1.0, 0.0).astype(BF16)
                stage_ref[e] = jnp.dot(oh, hb, preferred_element_type=F32).astype(BF16)
                window_copy(e, bases[e] + k * DISPATCH_WIN).start()
                return carry

            lax.fori_loop(1, n_wins[e], extra_window, 0)

    for e in range(N_EXPERTS):
        last_start = bases[e] + (jnp.maximum(n_wins[e], 1) - 1) * DISPATCH_WIN
        tile = jnp.minimum(_floor_tile(ends[e]) - last_start, DISPATCH_WIN - BF16_TILE_ROWS)
        carry_ref[e] = stage_ref[e, pl.ds(pl.multiple_of(tile, BF16_TILE_ROWS), BF16_TILE_ROWS), :]

    @pl.when(b == n_blk - 1)
    def _():
        for e in range(N_EXPERTS):
            window_copy(e, 0).wait()
            stage_ref[e] = jnp.zeros((DISPATCH_WIN, D_MODEL), BF16)
            window_copy(e, cap).start()
        for e in range(N_EXPERTS):
            window_copy(e, cap).wait()


def _dispatch(lo, posm, h2, cap):
    t = h2.shape[0]
    rpe = t // LANES
    n_blk = t // TOK_BLOCK
    grid_spec = pltpu.PrefetchScalarGridSpec(
        num_scalar_prefetch=1,
        grid=(n_blk,),
        in_specs=[pl.BlockSpec((N_EXPERTS, TOK_BLOCK), lambda b, lo_r: (0, b)),
                  pl.BlockSpec((TOK_BLOCK, D_MODEL), lambda b, lo_r: (b, 0))],
        out_specs=pl.BlockSpec(memory_space=pl.ANY),
        scratch_shapes=[pltpu.VMEM((N_EXPERTS, DISPATCH_WIN, D_MODEL), BF16),
                        pltpu.VMEM((N_EXPERTS, BF16_TILE_ROWS, D_MODEL), BF16),
                        pltpu.SemaphoreType.DMA((N_EXPERTS,))],
    )
    return pl.pallas_call(
        functools.partial(_dispatch_body, rpe=rpe, n_blk=n_blk, cap=cap),
        grid_spec=grid_spec,
        out_shape=jax.ShapeDtypeStruct((N_EXPERTS, cap + DISPATCH_WIN, D_MODEL), BF16),
        compiler_params=_params(("arbitrary",), 32),
        name="dispatch",
    )(lo, posm, h2)


def _ffn_body(x_ref, wg_ref, wu_ref, wd_ref, y_ref, acc_ref, *, n_f, n_sub, sub):
    f = pl.program_id(1)
    wg = wg_ref[0].astype(BF16)
    wu = wu_ref[0].astype(BF16)
    wd = wd_ref[0].astype(BF16)

    def sweep(first, last):
        def rows_step(i, carry):
            r = pl.ds(pl.multiple_of(i * sub, sub), sub)
            x = x_ref[0, r, :]
            a = jnp.dot(x, wg, preferred_element_type=F32)
            u = jnp.dot(x, wu, preferred_element_type=F32)
            h = (jax.nn.silu(a) * u).astype(BF16)
            d = jnp.dot(h, wd, preferred_element_type=F32)
            if not first:
                d = acc_ref[r, :] + d
            if last:
                y_ref[0, r, :] = d.astype(BF16)
            else:
                acc_ref[r, :] = d
            return carry

        lax.fori_loop(0, n_sub, rows_step, 0)

    if n_f == 1:
        sweep(True, True)
    else:
        pl.when(f == 0)(lambda: sweep(True, False))
        pl.when(jnp.logical_and(f > 0, f < n_f - 1))(lambda: sweep(False, False))
        pl.when(f == n_f - 1)(lambda: sweep(False, True))


def _ffn(xe, w_gate, w_up, w_down, cap, tf=512, sub=1024):
    sub = min(sub, cap)
    n_f = EXPERT_FF // tf
    return pl.pallas_call(
        functools.partial(_ffn_body, n_f=n_f, n_sub=cap // sub, sub=sub),
        grid=(N_EXPERTS, n_f),
        in_specs=[pl.BlockSpec((1, cap, D_MODEL), lambda e, f: (e, 0, 0)),
                  pl.BlockSpec((1, D_MODEL, tf), lambda e, f: (e, 0, f)),
                  pl.BlockSpec((1, D_MODEL, tf), lambda e, f: (e, 0, f)),
                  pl.BlockSpec((1, tf, D_MODEL), lambda e, f: (e, f, 0))],
        out_specs=pl.BlockSpec((1, cap, D_MODEL), lambda e, f: (e, 0, 0)),
        out_shape=jax.ShapeDtypeStruct((N_EXPERTS, cap, D_MODEL), BF16),
        scratch_shapes=[pltpu.VMEM((cap, D_MODEL), F32)],
        compiler_params=_params(("arbitrary", "arbitrary"), 56),
        name="expert_ffn",
    )(xe, w_gate, w_up, w_down)


def _combine_body(lo_ref, pm_ref, gm_ref, x1_ref, nf_ref, y_ref, out_ref,
                  a_ref, m_ref, ovw_ref, sem_ref, ovsem_ref, *, rpe, n_blk, cap):
    WIN = COMBINE_WIN
    b = pl.program_id(0)
    slot = b % 2

    def win_start(e, blk):
        return jnp.minimum(_floor_tile(_block_lo(lo_ref, e, blk, rpe)), cap - WIN)

    def window_copy(e, blk, sl):
        start = pl.multiple_of(win_start(e, blk), BF16_TILE_ROWS)
        return pltpu.make_async_copy(y_ref.at[e, pl.ds(start, WIN), :], a_ref.at[sl, pl.ds(e * WIN, WIN), :],
                                     sem_ref.at[sl, e])

    @pl.when(b == 0)
    def _():
        for e in range(N_EXPERTS):
            window_copy(e, 0, 0).start()

    @pl.when(b + 1 < n_blk)
    def _():
        for e in range(N_EXPERTS):
            window_copy(e, b + 1, 1 - slot).start()

    pm = pm_ref[...]
    gm = gm_ref[...]
    ws0 = [win_start(e, b) for e in range(N_EXPERTS)]
    e_col = lax.broadcasted_iota(I32, (1, N_EXPERTS), 1)
    ws_row = jnp.zeros((1, N_EXPERTS), I32)
    for e in range(N_EXPERTS):
        ws_row = jnp.where(e_col == e, ws0[e], ws_row)
    rel = jnp.clip(pm - ws_row, -1, WIN).astype(F32).astype(BF16)
    spread = jnp.where(lax.broadcasted_iota(I32, (N_EXPERTS, N_EXPERTS * WIN), 1) // WIN
                       == lax.broadcasted_iota(I32, (N_EXPERTS, N_EXPERTS * WIN), 0), 1.0, 0.0).astype(BF16)
    tgt = jnp.dot(rel, spread, preferred_element_type=F32)
    gv = jnp.dot(gm.astype(BF16), spread, preferred_element_type=F32)
    lane_r = (lax.broadcasted_iota(I32, (TOK_BLOCK, N_EXPERTS * WIN), 1) & (WIN - 1)).astype(F32)
    gmat = jnp.where(tgt == lane_r, gv, 0.0).astype(BF16)

    for e in range(N_EXPERTS):
        window_copy(e, b, slot).wait()
    m_ref[...] = jnp.dot(gmat, a_ref[slot], preferred_element_type=F32)

    ends = [_block_lo(lo_ref, e, b, rpe) + _block_count(lo_ref, e, b, rpe, n_blk, cap) for e in range(N_EXPERTS)]
    n_wins = [lax.div(jnp.maximum(ends[e] - ws0[e], 0) + (WIN - 1), WIN) for e in range(N_EXPERTS)]

    @pl.when(functools.reduce(jnp.maximum, n_wins) > 1)
    def _():
        lane_w = lax.broadcasted_iota(I32, (TOK_BLOCK, WIN), 1)
        for e in range(N_EXPERTS):
            def extra_window(k, carry):
                first = ws0[e] + k * WIN
                ws = jnp.minimum(first, cap - WIN)
                cp = pltpu.make_async_copy(y_ref.at[e, pl.ds(pl.multiple_of(ws, BF16_TILE_ROWS), WIN), :], ovw_ref,
                                           ovsem_ref)
                cp.start()
                cp.wait()
                pe = pm[:, e:e + 1]
                gk = jnp.where((lane_w == pe - ws) & (pe >= first), gm[:, e:e + 1], 0.0).astype(BF16)
                m_ref[...] += jnp.dot(gk, ovw_ref[...], preferred_element_type=F32)
                return carry

            lax.fori_loop(1, n_wins[e], extra_window, 0)

    xo = x1_ref[...] + m_ref[...]
    out_ref[...] = (xo * lax.rsqrt(jnp.mean(xo * xo, axis=-1, keepdims=True) + RMS_EPS)) * nf_ref[...]


def _combine(lo, posm_t, gate_t, x1, norm_final, y2, cap):
    t = x1.shape[0]
    rpe = t // LANES
    n_blk = t // TOK_BLOCK
    grid_spec = pltpu.PrefetchScalarGridSpec(
        num_scalar_prefetch=1,
        grid=(n_blk,),
        in_specs=[pl.BlockSpec((TOK_BLOCK, N_EXPERTS), lambda b, lo_r: (b, 0)),
                  pl.BlockSpec((TOK_BLOCK, N_EXPERTS), lambda b, lo_r: (b, 0)),
                  pl.BlockSpec((TOK_BLOCK, D_MODEL), lambda b, lo_r: (b, 0)),
                  pl.BlockSpec((1, D_MODEL), lambda b, lo_r: (0, 0)),
                  pl.BlockSpec(memory_space=pl.ANY)],
        out_specs=pl.BlockSpec((TOK_BLOCK, D_MODEL), lambda b, lo_r: (b, 0)),
        scratch_shapes=[pltpu.VMEM((2, N_EXPERTS * COMBINE_WIN, D_MODEL), BF16),
                        pltpu.VMEM((TOK_BLOCK, D_MODEL), F32),
                        pltpu.VMEM((COMBINE_WIN, D_MODEL), BF16),
                        pltpu.SemaphoreType.DMA((2, N_EXPERTS)),
                        pltpu.SemaphoreType.DMA(())],
    )
    return pl.pallas_call(
        functools.partial(_combine_body, rpe=rpe, n_blk=n_blk, cap=cap),
        grid_spec=grid_spec,
        out_shape=jax.ShapeDtypeStruct((t, D_MODEL), F32),
        compiler_params=_params(("arbitrary",), 40),
        name="combine",
    )(lo, posm_t, gate_t, x1, norm_final, y2)


def _trunk(x, p):
    b, l, _ = x.shape
    t = b * l
    n_seq = l // S5_CHUNK
    cap = max(1, EC_CAPACITY_FACTOR * t // N_EXPERTS)
    rpe = t // LANES

    hg, u, u_sm, gates = _inproj(x.reshape(t, D_MODEL), p['norm_mix'], p['w_in'])
    o_fw, o_bw = _hgrn(hg.reshape(b, l, 5 * A_WIDTH), p['lb'])
    ys5 = _s5(u_sm, p['s5_perm'], *p['s5'], n_seq)

    x1, h2, logits = _merge(x.reshape(t, D_MODEL), o_fw.reshape(t, A_WIDTH), o_bw.reshape(t, A_WIDTH), hg, ys5, u,
                            gates, p['hgrn_norm'], p['s5_d'], p['w_glu'], p['b_glu'], p['w_a'], p['w_b'],
                            p['w_out'], p['norm_ffn'], p['w_router'])

    posm, gate, lo = _select(logits.T.reshape(N_EXPERTS * rpe, LANES), rpe, cap)
    lo = lo[:, 0]
    posm = posm.reshape(N_EXPERTS, t)
    xe = _dispatch(lo, posm, h2, cap)
    y2 = _ffn(xe, p['w_gate'], p['w_up'], p['w_down'], cap)
    out = _combine(lo, posm.T, gate.reshape(N_EXPERTS, t).T, x1, p['norm_final'], y2, cap)
    return out.reshape(b, l, D_MODEL)


def kernel(x_prompt, x_sample, norm_mix, w_in, hgrn_gamma, hgrn_norm, s5_a_re, s5_a_im, s5_log_dt, s5_b_re, s5_b_im, s5_c_re, s5_c_im, s5_d, s5_w_glu, s5_b_glu, w_branch_a, w_branch_b, w_out, norm_ffn, w_router, w_exp_gate, w_exp_up, w_exp_down, norm_final):
    assert norm_mix.shape[0] == 1, "single-layer trunk"
    row = lambda a: a.astype(F32).reshape(1, -1)
    lb = jnp.cumsum(jax.nn.softmax(hgrn_gamma.astype(F32), axis=0), axis=0)[0]
    max_chunks = max(x_prompt.shape[1], x_sample.shape[1]) // S5_CHUNK
    s5_ops = _s5_operators(s5_a_re[0], s5_a_im[0], s5_log_dt[0], s5_b_re[0], s5_b_im[0], s5_c_re[0], s5_c_im[0],
                           max_chunks.bit_length() - 1)
    p = {
        'norm_mix': row(norm_mix[0]), 'w_in': w_in[0].astype(BF16), 'lb': lb, 's5': s5_ops,
        's5_perm': _s5_lane_permutation(),
        'hgrn_norm': row(hgrn_norm[0]), 's5_d': row(s5_d[0]), 'w_glu': s5_w_glu[0].astype(BF16),
        'b_glu': row(s5_b_glu[0]), 'w_a': w_branch_a[0].astype(BF16), 'w_b': w_branch_b[0].astype(BF16),
        'w_out': w_out[0].astype(BF16), 'norm_ffn': row(norm_ffn[0]),
        'w_router': _split_router(w_router[0]),
        'w_gate': w_exp_gate[0], 'w_up': w_exp_up[0], 'w_down': w_exp_down[0],
        'norm_final': row(norm_final),
    }
    return (_trunk(x_prompt, p), _trunk(x_sample, p))
```

```python
import functools
import math

import jax
import jax.numpy as jnp
from jax import lax
from jax.experimental import pallas as pl
from jax.experimental.pallas import tpu as pltpu

F32 = jnp.float32
BF16 = jnp.bfloat16
I32 = jnp.int32

D_MODEL = 1024
A_HEADS = 4
A_HEAD_DIM = 128
A_WIDTH = A_HEADS * A_HEAD_DIM
A_CHUNK = 64
S5_WIDTH = 512
S5_GROUP = 16
S5_GROUPS = S5_WIDTH // S5_GROUP
S5_STATE = 64
N_EXPERTS = 16
EXPERT_FF = 2048
EC_CAPACITY_FACTOR = 2
RMS_EPS = 1e-6
IN_COLS = 5 * A_WIDTH + S5_WIDTH + 2 * D_MODEL

LANES = 128
BF16_TILE_ROWS = 16
S5_CHUNK = 16
S5_COLS = S5_CHUNK * S5_GROUP
S5_GROUPS_PER_STEP = LANES // S5_GROUP
TOK_BLOCK = 256
COMBINE_WIN = 64
DISPATCH_WIN = 64
MIB = 2 ** 20


def _params(semantics, vmem_mib):
    return pltpu.CompilerParams(dimension_semantics=semantics, vmem_limit_bytes=vmem_mib * MIB)


def _inproj_body(x_ref, g_ref, w_ref, hg_ref, u_ref, us_ref, gt_ref, uf_ref, *, tm):
    x = x_ref[...]
    ms = jnp.mean(x * x, axis=-1, keepdims=True)
    h = ((x * lax.rsqrt(ms + RMS_EPS)) * g_ref[...]).astype(BF16)
    n_hg = 5 * A_WIDTH // 512
    for c in range(IN_COLS // 512):
        r = jnp.dot(h, w_ref[:, c * 512:(c + 1) * 512], preferred_element_type=F32)
        if c < n_hg:
            hg_ref[:, c * 512:(c + 1) * 512] = r.astype(BF16)
        elif c == n_hg:
            u_ref[...] = r.astype(BF16)
            for cb in range(S5_WIDTH // LANES):
                uf_ref[cb] = r[:, cb * LANES:(cb + 1) * LANES]
                for sg in range(S5_CHUNK):
                    us_ref[sg, :, cb * LANES:(cb + 1) * LANES] = uf_ref[
                        cb, pl.ds(sg, tm // S5_CHUNK, stride=S5_CHUNK), :].astype(BF16)
        else:
            gt_ref[:, (c - n_hg - 1) * 512:(c - n_hg) * 512] = r.astype(BF16)


def _inproj(x, gain, w_bf16, tm=512):
    t = x.shape[0]
    return pl.pallas_call(
        functools.partial(_inproj_body, tm=tm),
        grid=(t // tm,),
        in_specs=[pl.BlockSpec((tm, D_MODEL), lambda i: (i, 0)),
                  pl.BlockSpec((1, D_MODEL), lambda i: (0, 0)),
                  pl.BlockSpec((D_MODEL, IN_COLS), lambda i: (0, 0))],
        out_specs=[pl.BlockSpec((tm, 5 * A_WIDTH), lambda i: (i, 0)),
                   pl.BlockSpec((tm, S5_WIDTH), lambda i: (i, 0)),
                   pl.BlockSpec((S5_CHUNK, tm // S5_CHUNK, S5_WIDTH), lambda i: (0, i, 0)),
                   pl.BlockSpec((tm, 2 * D_MODEL), lambda i: (i, 0))],
        out_shape=[jax.ShapeDtypeStruct((t, 5 * A_WIDTH), BF16),
                   jax.ShapeDtypeStruct((t, S5_WIDTH), BF16),
                   jax.ShapeDtypeStruct((S5_CHUNK, t // S5_CHUNK, S5_WIDTH), BF16),
                   jax.ShapeDtypeStruct((t, 2 * D_MODEL), BF16)],
        scratch_shapes=[pltpu.VMEM((S5_WIDTH // LANES, tm, LANES), F32)],
        compiler_params=_params(("arbitrary",), 48),
        name="inproj",
    )(x, gain, w_bf16)


def _hgrn_pair(q, fr, v, lb, s_ref, forward):
    c, c2 = A_CHUNK, 2 * A_CHUNK
    t_idx = lax.broadcasted_iota(I32, (c2, c2), 0)
    s_idx = lax.broadcasted_iota(I32, (c2, c2), 1)
    one = lambda cond: jnp.where(cond, 1.0, 0.0)
    s_loc = s_idx & (c - 1)
    same_chunk = (t_idx < c) == (s_idx < c)
    tri = same_chunk & ((s_idx <= t_idx) if forward else (s_idx >= t_idx))
    m_run = one(tri)
    m_mid = one(same_chunk & ((s_loc <= c // 2 - 1) if forward else (s_loc >= c // 2)))
    m_all = one(same_chunk)
    order = (0, 1) if forward else (1, 0)

    q = q.astype(F32)
    f = lb + (1.0 - lb) * jax.nn.sigmoid(fr.astype(F32))
    lf = jnp.log2(f)
    kk = 1.0 - f
    hi = lf.astype(BF16)
    hl = jnp.concatenate([hi, (lf - hi.astype(F32)).astype(BF16)], axis=0)

    def sums(m):
        mb = m.astype(BF16)
        return jnp.dot(jnp.concatenate([mb, mb], axis=1), hl, preferred_element_type=F32)

    e_mid = jnp.exp2(sums(m_run - m_mid))
    q_in = (q * e_mid).astype(BF16)
    k_in = kk * (1.0 / e_mid)
    qdec = (q * jnp.exp2(sums(m_run))).astype(BF16)
    kdec = kk * jnp.exp2(sums(m_all - m_run))
    total = [sums(one(jnp.broadcast_to((s_idx < c) == (ch == 0), (c2, c2)))) for ch in (0, 1)]
    zero = jnp.zeros((), BF16)
    outs = []
    for h in range(A_HEADS):
        sl = slice(h * A_HEAD_DIM, (h + 1) * A_HEAD_DIM)
        vh = v[:, sl]
        k_t = k_in[:, sl].T.astype(BF16)
        kd_t = kdec[:, sl].T.astype(BF16)
        s = jnp.where(tri, jnp.dot(q_in[:, sl], k_t, preferred_element_type=F32), 0.0).astype(BF16)
        o_intra = jnp.dot(s, vh, preferred_element_type=F32)
        state = s_ref[h]
        o_inter = [None, None]
        for ch in order:
            rows = slice(ch * c, (ch + 1) * c)
            o_inter[ch] = jnp.dot(qdec[rows, sl], state.astype(BF16), preferred_element_type=F32)
            own_rows = (t_idx < c) if ch == 0 else (t_idx >= c)
            v_ch = jnp.where(own_rows, vh, zero)
            decay = jnp.exp2(total[ch][:, sl].T)
            state = decay * state + jnp.dot(kd_t, v_ch, preferred_element_type=F32)
        s_ref[h] = state
        outs.append(o_intra + jnp.concatenate(o_inter, axis=0))
    return jnp.concatenate(outs, axis=1).astype(BF16)


def _hgrn_body(qf_ref, ff_ref, vf_ref, qb_ref, fb_ref, vb_ref, lb_ref, of_ref, ob_ref, sf_ref, sb_ref,
               *, n_pairs):
    @pl.when(pl.program_id(1) == 0)
    def _():
        sf_ref[...] = jnp.zeros_like(sf_ref)
        sb_ref[...] = jnp.zeros_like(sb_ref)

    c2 = 2 * A_CHUNK

    def step(j, carry):
        rf = pl.ds(pl.multiple_of(j * c2, c2), c2)
        of_ref[0, rf, :] = _hgrn_pair(qf_ref[0, rf, :], ff_ref[0, rf, :], vf_ref[0, rf, :],
                                      lb_ref[0:1, :], sf_ref, True)
        rb = pl.ds(pl.multiple_of((n_pairs - 1 - j) * c2, c2), c2)
        ob_ref[0, rb, :] = _hgrn_pair(qb_ref[0, rb, :], fb_ref[0, rb, :], vb_ref[0, rb, :],
                                      lb_ref[1:2, :], sb_ref, False)
        return carry

    lax.fori_loop(0, n_pairs, step, 0, unroll=True)


def _hgrn(hg, lb, lb_tokens=1024):
    b, l, _ = hg.shape
    nb = l // lb_tokens
    blk = (1, lb_tokens, A_WIDTH)
    fwd = lambda col: pl.BlockSpec(blk, lambda bi, i: (bi, i, col))
    bwd = lambda col: pl.BlockSpec(blk, lambda bi, i: (bi, nb - 1 - i, col))
    return pl.pallas_call(
        functools.partial(_hgrn_body, n_pairs=lb_tokens // (2 * A_CHUNK)),
        grid=(b, nb),
        in_specs=[fwd(0), fwd(1), fwd(3), bwd(0), bwd(2), bwd(3),
                  pl.BlockSpec((2, A_WIDTH), lambda bi, i: (0, 0))],
        out_specs=[pl.BlockSpec(blk, lambda bi, i: (bi, i, 0)),
                   pl.BlockSpec(blk, lambda bi, i: (bi, nb - 1 - i, 0))],
        out_shape=[jax.ShapeDtypeStruct((b, l, A_WIDTH), BF16)] * 2,
        scratch_shapes=[pltpu.VMEM((A_HEADS, A_HEAD_DIM, A_HEAD_DIM), F32)] * 2,
        compiler_params=_params(("arbitrary", "arbitrary"), 32),
        name="hgrn2",
    )(hg, hg, hg, hg, hg, hg, lb)


def _s5_direction_terms(a_re, a_im, log_dt, b_re, b_im):
    a_re, a_im = a_re.astype(F32), a_im.astype(F32)
    b_re, b_im = b_re.astype(F32), b_im.astype(F32)
    dt = jnp.exp(log_dt.astype(F32))[:, None]
    j = jnp.arange(S5_CHUNK + 1, dtype=F32)[:, None, None]
    mag = jnp.exp(j * (dt * a_re))
    pw_re = mag * jnp.cos(j * (dt * a_im))
    pw_im = mag * jnp.sin(j * (dt * a_im))
    nr, ni = pw_re[1] - 1.0, pw_im[1]
    den = a_re * a_re + a_im * a_im
    f_re = (nr * a_re + ni * a_im) / den
    f_im = (ni * a_re - nr * a_im) / den
    bb_re = f_re[..., None] * b_re - f_im[..., None] * b_im
    bb_im = f_re[..., None] * b_im + f_im[..., None] * b_re
    pb_re = pw_re[..., None] * bb_re - pw_im[..., None] * bb_im
    pb_im = pw_re[..., None] * bb_im + pw_im[..., None] * bb_re
    return pw_re, pw_im, pb_re, pb_im


def _s5_assemble_body(strip_ref, inf_ref, inb_ref, outf_ref, outb_ref, w1_ref, w2_ref):
    c, half = S5_CHUNK, 2 * S5_STATE
    for g in range(strip_ref.shape[0]):
        strip = strip_ref[g]
        for s in range(c):
            lo = (c - 1 - s) * S5_GROUP
            w1_ref[g, s * S5_GROUP:(s + 1) * S5_GROUP, :S5_COLS] = strip[:, lo:lo + S5_COLS].astype(BF16)
        w1_ref[g, :, S5_COLS:S5_COLS + half] = inf_ref[g].astype(BF16)
        w1_ref[g, :, S5_COLS + half:] = inb_ref[g].astype(BF16)
        w2_ref[g, :half, :] = outf_ref[g].astype(BF16)
        w2_ref[g, half:, :] = outb_ref[g].astype(BF16)


def _s5_assemble(strip, w_in_f, w_in_b, w_out_f, w_out_b):
    g = strip.shape[0]
    gps = S5_GROUPS_PER_STEP
    per_g = lambda a: pl.BlockSpec((gps,) + a.shape[1:], lambda i: (i, 0, 0))
    return pl.pallas_call(
        _s5_assemble_body,
        grid=(g // gps,),
        in_specs=[per_g(strip), per_g(w_in_f), per_g(w_in_b), per_g(w_out_f), per_g(w_out_b)],
        out_specs=[pl.BlockSpec((gps, S5_COLS, 2 * S5_COLS), lambda i: (i, 0, 0)),
                   pl.BlockSpec((gps, S5_COLS, S5_COLS), lambda i: (i, 0, 0))],
        out_shape=[jax.ShapeDtypeStruct((g, S5_COLS, 2 * S5_COLS), BF16),
                   jax.ShapeDtypeStruct((g, S5_COLS, S5_COLS), BF16)],
        compiler_params=_params(("arbitrary",), 32),
        name="s5_assemble",
    )(strip, w_in_f, w_in_b, w_out_f, w_out_b)


def _s5_operators(a_re, a_im, log_dt, b_re, b_im, c_re, c_im, n_steps):
    hp = lax.Precision.HIGHEST
    c_re, c_im = c_re.astype(F32), c_im.astype(F32)
    g = S5_GROUPS
    terms = [_s5_direction_terms(a_re[d], a_im[d], log_dt[d], b_re, b_im) for d in range(2)]
    taps = []
    for pw_re, pw_im, pb_re, pb_im in terms:
        k = (jnp.einsum('gop,jgpi->jgoi', c_re, pb_re[:S5_CHUNK], precision=hp)
             - jnp.einsum('gop,jgpi->jgoi', c_im, pb_im[:S5_CHUNK], precision=hp))
        taps.append(k)
    c = S5_CHUNK
    strip = jnp.concatenate([taps[1][:0:-1], (taps[0][0] + taps[1][0])[None], taps[0][1:]], axis=0)
    strip = strip.transpose(1, 3, 0, 2).reshape(g, S5_GROUP, (2 * c - 1) * S5_GROUP)

    def w_in(pb_re, pb_im):
        re = pb_re.transpose(1, 0, 3, 2)
        im = pb_im.transpose(1, 0, 3, 2)
        return jnp.concatenate([re, im], axis=-1).reshape(g, S5_COLS, 2 * S5_STATE)

    def w_out(pr, pi):
        m_re = c_re[None] * pr[:, :, None, :] - c_im[None] * pi[:, :, None, :]
        m_im = c_re[None] * pi[:, :, None, :] + c_im[None] * pr[:, :, None, :]
        re = m_re.transpose(1, 3, 0, 2).reshape(g, S5_STATE, S5_COLS)
        im = -m_im.transpose(1, 3, 0, 2).reshape(g, S5_STATE, S5_COLS)
        return jnp.concatenate([re, im], axis=1)

    w_in_f = w_in(terms[0][2][c - 1::-1], terms[0][3][c - 1::-1])
    w_in_b = w_in(terms[1][2][:c], terms[1][3][:c])
    w_out_f = w_out(terms[0][0][1:], terms[0][1][1:])
    w_out_b = w_out(terms[1][0][:0:-1], terms[1][1][:0:-1])
    w1, w2 = _s5_assemble(strip, w_in_f, w_in_b, w_out_f, w_out_b)

    ars, ais = [], []
    cur = [(t[0][S5_CHUNK], t[1][S5_CHUNK]) for t in terms]
    for _ in range(n_steps):
        ars.append(jnp.concatenate([cur[0][0], cur[0][0], cur[1][0], cur[1][0]], axis=-1))
        ais.append(jnp.concatenate([-cur[0][1], cur[0][1], -cur[1][1], cur[1][1]], axis=-1))
        cur = [(r * r - i * i, 2.0 * r * i) for r, i in cur]
    return w1, w2, jnp.stack(ars, axis=1), jnp.stack(ais, axis=1)


def _s5_group(u, w1, w2, ar, ai, n, n_seq, n_steps):
    rows = u.shape[0]
    half = 2 * S5_STATE
    yv = jnp.dot(u, w1, preferred_element_type=F32)
    zf = jnp.where(n >= 1, pltpu.roll(yv[:, S5_COLS:S5_COLS + half], 1, axis=0), 0.0)
    zb = jnp.where(n < n_seq - 1, pltpu.roll(yv[:, S5_COLS + half:], rows - 1, axis=0), 0.0)
    for k in range(n_steps):
        s = 1 << k
        sf = jnp.where(n >= s, pltpu.roll(zf, s, axis=0), 0.0)
        sb = jnp.where(n < n_seq - s, pltpu.roll(zb, rows - s, axis=0), 0.0)
        zf = zf + ar[k:k + 1, :half] * sf + ai[k:k + 1, :half] * pltpu.roll(sf, S5_STATE, axis=1)
        zb = zb + ar[k:k + 1, half:] * sb + ai[k:k + 1, half:] * pltpu.roll(sb, S5_STATE, axis=1)
    x = jnp.concatenate([zf, zb], axis=1).astype(BF16)
    return yv[:, :S5_COLS] + jnp.dot(x, w2, preferred_element_type=F32)


def _s5_body(u_ref, perm_ref, w1_ref, w2_ref, ar_ref, ai_ref, y_ref, z_ref, *, n_seq, n_steps):
    gps = S5_GROUPS_PER_STEP
    rows = u_ref.shape[1]
    n = lax.broadcasted_iota(I32, (rows, 2 * S5_STATE), 0) & (n_seq - 1)
    perm = perm_ref[...]
    for hf in range(S5_CHUNK // gps):
        x = jnp.concatenate([u_ref[hf * gps + j] for j in range(gps)], axis=1)
        z_ref[:, hf * gps * LANES:(hf + 1) * gps * LANES] = jnp.dot(
            x, perm, preferred_element_type=F32).astype(BF16)
    for g in range(gps):
        u = jnp.concatenate([z_ref[:, (hf * gps + g) * LANES:(hf * gps + g + 1) * LANES]
                             for hf in range(S5_CHUNK // gps)], axis=1)
        y = _s5_group(u, w1_ref[g], w2_ref[g], ar_ref[g], ai_ref[g], n, n_seq, n_steps).astype(BF16)
        for hf in range(S5_CHUNK // gps):
            z_ref[:, (hf * gps + g) * LANES:(hf * gps + g + 1) * LANES] = y[:, hf * LANES:(hf + 1) * LANES]
    for hf in range(S5_CHUNK // gps):
        x = jnp.dot(z_ref[:, hf * gps * LANES:(hf + 1) * gps * LANES], perm,
                    preferred_element_type=F32).astype(BF16)
        for j in range(gps):
            y_ref[hf * gps + j] = x[:, j * LANES:(j + 1) * LANES]


def _s5_lane_permutation():
    idx = jnp.arange(S5_GROUPS_PER_STEP * LANES)
    a, b, c = idx // LANES, (idx % LANES) // S5_GROUP, idx % S5_GROUP
    dst = b * LANES + a * S5_GROUP + c
    return (dst[:, None] == idx[None, :]).astype(BF16)


def _s5(us, perm, w1, w2, ar, ai, n_seq):
    _, rows, _ = us.shape
    assert n_seq & (n_seq - 1) == 0, n_seq
    n_steps = n_seq.bit_length() - 1
    gps = S5_GROUPS_PER_STEP
    tok = pl.BlockSpec((S5_CHUNK, rows, LANES), lambda i: (0, 0, i))
    per_g = lambda a: pl.BlockSpec((gps,) + a.shape[1:], lambda i: (i, 0, 0))
    return pl.pallas_call(
        functools.partial(_s5_body, n_seq=n_seq, n_steps=n_steps),
        grid=(S5_GROUPS // gps,),
        in_specs=[tok, pl.BlockSpec(perm.shape, lambda i: (0, 0)), per_g(w1), per_g(w2), per_g(ar), per_g(ai)],
        out_specs=tok,
        out_shape=jax.ShapeDtypeStruct(us.shape, BF16),
        scratch_shapes=[pltpu.VMEM((rows, S5_CHUNK * LANES), BF16)],
        compiler_params=_params(("arbitrary",), 48),
        name="s5",
    )(us, perm, w1, w2, ar, ai)


def _gelu_tanh(x):
    return 0.5 * x * (1.0 + jnp.tanh(math.sqrt(2.0 / math.pi) * (x + 0.044715 * (x * x * x))))


def _merge_body(x_ref, of_ref, ob_ref, go_ref, ys_ref, u_ref, ga_ref, gb_ref,
                hn_ref, sd_ref, wglu_ref, bglu_ref, wa_ref, wb_ref, wo_ref, nf_ref, wr_ref,
                x1_ref, h2_ref, lg_ref, yn_ref, *, tm):
    for sg in range(S5_CHUNK):
        for cb in range(S5_WIDTH // LANES):
            yn_ref[cb, pl.ds(sg, tm // S5_CHUNK, stride=S5_CHUNK), :] = ys_ref[
                sg, :, cb * LANES:(cb + 1) * LANES].astype(F32)
    o = of_ref[...].astype(F32) + ob_ref[...].astype(F32)
    parts = []
    for h in range(A_HEADS):
        oh = o[:, h * A_HEAD_DIM:(h + 1) * A_HEAD_DIM]
        parts.append(oh * lax.rsqrt(jnp.mean(oh * oh, axis=-1, keepdims=True) + RMS_EPS))
    h_a = jnp.concatenate(parts, axis=1) * hn_ref[...] * jax.nn.silu(go_ref[...].astype(F32))

    u = u_ref[...].astype(F32)
    y_s5 = jnp.concatenate([yn_ref[cb] for cb in range(S5_WIDTH // LANES)], axis=1)
    ys = _gelu_tanh(y_s5 + sd_ref[...] * u)
    z = jnp.dot(ys.astype(BF16), wglu_ref[...], preferred_element_type=F32) + bglu_ref[...]
    h_b = ys * jax.nn.sigmoid(z)

    y_a = jnp.dot(h_a.astype(BF16), wa_ref[...], preferred_element_type=F32)
    y_b = jnp.dot(h_b.astype(BF16), wb_ref[...], preferred_element_type=F32)
    merged = jax.nn.sigmoid(ga_ref[...].astype(F32)) * y_a + jax.nn.sigmoid(gb_ref[...].astype(F32)) * y_b
    x1 = x_ref[...] + jnp.dot(merged.astype(BF16), wo_ref[...], preferred_element_type=F32)
    x1_ref[...] = x1

    h2 = (x1 * lax.rsqrt(jnp.mean(x1 * x1, axis=-1, keepdims=True) + RMS_EPS)) * nf_ref[...]
    h2_hi = h2.astype(BF16)
    h2_ref[...] = h2_hi
    h2_lo = (h2 - h2_hi.astype(F32)).astype(BF16)
    r_hi = jnp.dot(h2_hi, wr_ref[...], preferred_element_type=F32)
    r_lo = jnp.dot(h2_lo, wr_ref[...], preferred_element_type=F32)
    lg_ref[...] = r_hi[:, :N_EXPERTS] + r_hi[:, N_EXPERTS:2 * N_EXPERTS] + r_lo[:, :N_EXPERTS]


def _split_router(w_router):
    w = w_router.astype(F32)
    hi = w.astype(BF16)
    lo = (w - hi.astype(F32)).astype(BF16)
    pad = jnp.zeros((w.shape[0], LANES - 2 * N_EXPERTS), BF16)
    return jnp.concatenate([hi, lo, pad], axis=1)


def _merge(x, o_fw, o_bw, hg, ys5, u, gates, hn, sd, wglu, bglu, wa, wb, wo, nf, wr_pad, tm=512):
    t = x.shape[0]
    row = lambda w, col=0: pl.BlockSpec((tm, w), lambda i: (i, col))
    full = lambda a: pl.BlockSpec(a.shape, lambda i: (0,) * a.ndim)
    return pl.pallas_call(
        functools.partial(_merge_body, tm=tm),
        grid=(t // tm,),
        in_specs=[row(D_MODEL), row(A_WIDTH), row(A_WIDTH), row(A_WIDTH, 4),
                  pl.BlockSpec((S5_CHUNK, tm // S5_CHUNK, S5_WIDTH), lambda i: (0, i, 0)), row(S5_WIDTH),
                  row(D_MODEL, 0), row(D_MODEL, 1),
                  full(hn), full(sd), full(wglu), full(bglu), full(wa), full(wb), full(wo), full(nf), full(wr_pad)],
        out_specs=[row(D_MODEL), row(D_MODEL), row(N_EXPERTS)],
        out_shape=[jax.ShapeDtypeStruct((t, D_MODEL), F32),
                   jax.ShapeDtypeStruct((t, D_MODEL), BF16),
                   jax.ShapeDtypeStruct((t, N_EXPERTS), F32)],
        scratch_shapes=[pltpu.VMEM((S5_WIDTH // LANES, tm, LANES), F32)],
        compiler_params=_params(("arbitrary",), 48),
        name="merge",
    )(x, o_fw, o_bw, hg, ys5, u, gates, gates, hn, sd, wglu, bglu, wa, wb, wo, nf, wr_pad)


def _exclusive_prefix(m, upper, strict_lower):
    mb = m.astype(BF16)
    incl = jnp.dot(mb, upper, preferred_element_type=F32)
    row_off = jnp.sum(jnp.dot(strict_lower, mb, preferred_element_type=F32), axis=1, keepdims=True)
    return incl - m + row_off, row_off


def _select_body(lg_ref, pos_ref, gate_ref, lo_ref, aff_ref, *, rpe, cap):
    e_n = N_EXPERTS
    slab = lambda ref, e: ref[e * rpe:(e + 1) * rpe, :]
    mx = slab(lg_ref, 0)
    for e in range(1, e_n):
        mx = jnp.maximum(mx, slab(lg_ref, e))
    den = jnp.zeros_like(mx)
    for e in range(e_n):
        ex = jnp.exp(slab(lg_ref, e) - mx)
        aff_ref[e * rpe:(e + 1) * rpe, :] = ex
        den = den + ex
    for e in range(e_n):
        aff_ref[e * rpe:(e + 1) * rpe, :] = slab(aff_ref, e) / den

    capf = float(cap)

    def bit_step(i, cand):
        bit = jnp.left_shift(jnp.int32(1), 30 - i)
        rows = []
        for e in range(e_n):
            cur = cand[e:e + 1, :]
            trial = cur | bit
            cnt = jnp.sum(jnp.where(slab(aff_ref, e) >= pltpu.bitcast(trial, F32), 1.0, 0.0), keepdims=True)
            rows.append(jnp.where(cnt >= capf, trial, cur))
        return jnp.concatenate(rows, axis=0)

    thr = pltpu.bitcast(lax.fori_loop(0, 31, bit_step, jnp.zeros((e_n, LANES), I32)), F32)

    upper = jnp.where(lax.broadcasted_iota(I32, (LANES, LANES), 0) <= lax.broadcasted_iota(I32, (LANES, LANES), 1),
                      1.0, 0.0).astype(BF16)
    strict_lower = jnp.where(lax.broadcasted_iota(I32, (rpe, rpe), 1) < lax.broadcasted_iota(I32, (rpe, rpe), 0),
                             1.0, 0.0).astype(BF16)
    for e in range(e_n):
        aff = slab(aff_ref, e)
        t = thr[e:e + 1, :]
        gt = jnp.where(aff > t, 1.0, 0.0)
        eq = jnp.where(aff == t, 1.0, 0.0)
        need = capf - jnp.sum(gt, keepdims=True)
        rank_eq, _ = _exclusive_prefix(eq, upper, strict_lower)
        sel = gt + eq * jnp.where(rank_eq < need, 1.0, 0.0)
        pos, row_off = _exclusive_prefix(sel, upper, strict_lower)
        chosen = sel > 0.5
        pos_ref[e * rpe:(e + 1) * rpe, :] = jnp.where(chosen, pos, -1.0).astype(I32)
        gate_ref[e * rpe:(e + 1) * rpe, :] = jnp.where(chosen, aff, 0.0)
        lo_ref[e * rpe:(e + 1) * rpe, :] = jnp.broadcast_to(row_off, (rpe, LANES)).astype(I32)


def _select(lg_em, rpe, cap):
    rows = N_EXPERTS * rpe
    shp = lambda dt: jax.ShapeDtypeStruct((rows, LANES), dt)
    return pl.pallas_call(
        functools.partial(_select_body, rpe=rpe, cap=cap),
        out_shape=[shp(I32), shp(F32), shp(I32)],
        scratch_shapes=[pltpu.VMEM((rows, LANES), F32)],
        compiler_params=pltpu.CompilerParams(vmem_limit_bytes=32 * MIB),
        name="select",
    )(lg_em)


def _block_lo(lo_ref, e, blk, rpe):
    return lo_ref[e * rpe + blk * (TOK_BLOCK // LANES)]


def _block_count(lo_ref, e, blk, rpe, n_blk, cap):
    nxt = lo_ref[e * rpe + jnp.minimum((blk + 1) * (TOK_BLOCK // LANES), rpe - 1)]
    return jnp.where(blk == n_blk - 1, cap, nxt) - _block_lo(lo_ref, e, blk, rpe)


def _floor_tile(row):
    return row - (row & (BF16_TILE_ROWS - 1))


def _dispatch_body(lo_ref, pm_ref, h_ref, xe_ref, stage_ref, carry_ref, sem_ref, *, rpe, n_blk, cap):
    b = pl.program_id(0)

    @pl.when(b == 0)
    def _():
        carry_ref[...] = jnp.zeros_like(carry_ref)

    pm = pm_ref[...]
    hb = h_ref[...]
    r_idx = lax.broadcasted_iota(I32, (DISPATCH_WIN, TOK_BLOCK), 0)
    head_row = lax.broadcasted_iota(I32, (BF16_TILE_ROWS, D_MODEL), 0)
    los = [_block_lo(lo_ref, e, b, rpe) for e in range(N_EXPERTS)]
    bases = [_floor_tile(lo) for lo in los]
    onehot = jnp.concatenate(
        [jnp.where(r_idx == pm[e:e + 1, :] - bases[e], 1.0, 0.0) for e in range(N_EXPERTS)], axis=0).astype(BF16)
    xw = jnp.dot(onehot, hb, preferred_element_type=F32)

    def window_copy(e, first_row):
        start = pl.multiple_of(first_row, BF16_TILE_ROWS)
        return pltpu.make_async_copy(stage_ref.at[e], xe_ref.at[e, pl.ds(start, DISPATCH_WIN), :], sem_ref.at[e])

    for e in range(N_EXPERTS):
        @pl.when(b > 0)
        def _():
            window_copy(e, 0).wait()

        rows = xw[e * DISPATCH_WIN:(e + 1) * DISPATCH_WIN, :]
        head = jnp.where(head_row < los[e] - bases[e], carry_ref[e].astype(F32), rows[:BF16_TILE_ROWS, :])
        stage_ref[e, :BF16_TILE_ROWS, :] = head.astype(BF16)
        stage_ref[e, BF16_TILE_ROWS:, :] = rows[BF16_TILE_ROWS:, :].astype(BF16)
        window_copy(e, bases[e]).start()

    ends = [los[e] + _block_count(lo_ref, e, b, rpe, n_blk, cap) for e in range(N_EXPERTS)]
    n_wins = [lax.div(ends[e] - bases[e] + (DISPATCH_WIN - 1), DISPATCH_WIN) for e in range(N_EXPERTS)]

    @pl.when(functools.reduce(jnp.maximum, n_wins) > 1)
    def _():
        for e in range(N_EXPERTS):
            def extra_window(k, carry):
                window_copy(e, 0).wait()
                oh = jnp.where(r_idx + k * DISPATCH_WIN == pm[e:e + 1, :] - bases[e], 1.0, 0.0).astype(BF16)
                stage_ref[e] = jnp.dot(oh, hb, preferred_element_type=F32).astype(BF16)
                window_copy(e, bases[e] + k * DISPATCH_WIN).start()
                return carry

            lax.fori_loop(1, n_wins[e], extra_window, 0)

    for e in range(N_EXPERTS):
        last_start = bases[e] + (jnp.maximum(n_wins[e], 1) - 1) * DISPATCH_WIN
        tile = jnp.minimum(_floor_tile(ends[e]) - last_start, DISPATCH_WIN - BF16_TILE_ROWS)
        carry_ref[e] = stage_ref[e, pl.ds(pl.multiple_of(tile, BF16_TILE_ROWS), BF16_TILE_ROWS), :]

    @pl.when(b == n_blk - 1)
    def _():
        for e in range(N_EXPERTS):
            window_copy(e, 0).wait()
            stage_ref[e] = jnp.zeros((DISPATCH_WIN, D_MODEL), BF16)
            window_copy(e, cap).start()
        for e in range(N_EXPERTS):
            window_copy(e, cap).wait()


def _dispatch(lo, posm, h2, cap):
    t = h2.shape[0]
    rpe = t // LANES
    n_blk = t // TOK_BLOCK
    grid_spec = pltpu.PrefetchScalarGridSpec(
        num_scalar_prefetch=1,
        grid=(n_blk,),
        in_specs=[pl.BlockSpec((N_EXPERTS, TOK_BLOCK), lambda b, lo_r: (0, b)),
                  pl.BlockSpec((TOK_BLOCK, D_MODEL), lambda b, lo_r: (b, 0))],
        out_specs=pl.BlockSpec(memory_space=pl.ANY),
        scratch_shapes=[pltpu.VMEM((N_EXPERTS, DISPATCH_WIN, D_MODEL), BF16),
                        pltpu.VMEM((N_EXPERTS, BF16_TILE_ROWS, D_MODEL), BF16),
                        pltpu.SemaphoreType.DMA((N_EXPERTS,))],
    )
    return pl.pallas_call(
        functools.partial(_dispatch_body, rpe=rpe, n_blk=n_blk, cap=cap),
        grid_spec=grid_spec,
        out_shape=jax.ShapeDtypeStruct((N_EXPERTS, cap + DISPATCH_WIN, D_MODEL), BF16),
        compiler_params=_params(("arbitrary",), 32),
        name="dispatch",
    )(lo, posm, h2)


def _ffn_body(x_ref, wg_ref, wu_ref, wd_ref, y_ref, acc_ref, *, n_f, n_sub, sub):
    f = pl.program_id(1)
    wg = wg_ref[0].astype(BF16)
    wu = wu_ref[0].astype(BF16)
    wd = wd_ref[0].astype(BF16)

    def sweep(first, last):
        def rows_step(i, carry):
            r = pl.ds(pl.multiple_of(i * sub, sub), sub)
            x = x_ref[0, r, :]
            a = jnp.dot(x, wg, preferred_element_type=F32)
            u = jnp.dot(x, wu, preferred_element_type=F32)
            h = (jax.nn.silu(a) * u).astype(BF16)
            d = jnp.dot(h, wd, preferred_element_type=F32)
            if not first:
                d = acc_ref[r, :] + d
            if last:
                y_ref[0, r, :] = d.astype(BF16)
            else:
                acc_ref[r, :] = d
            return carry

        lax.fori_loop(0, n_sub, rows_step, 0, unroll=True)

    if n_f == 1:
        sweep(True, True)
    else:
        pl.when(f == 0)(lambda: sweep(True, False))
        pl.when(jnp.logical_and(f > 0, f < n_f - 1))(lambda: sweep(False, False))
        pl.when(f == n_f - 1)(lambda: sweep(False, True))


def _ffn(xe, w_gate, w_up, w_down, cap, tf=512, sub=1024):
    sub = min(sub, cap)
    n_f = EXPERT_FF // tf
    return pl.pallas_call(
        functools.partial(_ffn_body, n_f=n_f, n_sub=cap // sub, sub=sub),
        grid=(N_EXPERTS, n_f),
        in_specs=[pl.BlockSpec((1, cap, D_MODEL), lambda e, f: (e, 0, 0)),
                  pl.BlockSpec((1, D_MODEL, tf), lambda e, f: (e, 0, f)),
                  pl.BlockSpec((1, D_MODEL, tf), lambda e, f: (e, 0, f)),
                  pl.BlockSpec((1, tf, D_MODEL), lambda e, f: (e, f, 0))],
        out_specs=pl.BlockSpec((1, cap, D_MODEL), lambda e, f: (e, 0, 0)),
        out_shape=jax.ShapeDtypeStruct((N_EXPERTS, cap, D_MODEL), BF16),
        scratch_shapes=[pltpu.VMEM((cap, D_MODEL), F32)],
        compiler_params=_params(("arbitrary", "arbitrary"), 56),
        name="expert_ffn",
    )(xe, w_gate, w_up, w_down)


def _combine_body(lo_ref, pm_ref, gm_ref, x1_ref, nf_ref, y_ref, out_ref,
                  a_ref, m_ref, ovw_ref, sem_ref, ovsem_ref, *, rpe, n_blk, cap):
    WIN = COMBINE_WIN
    b = pl.program_id(0)
    slot = b % 2

    def win_start(e, blk):
        return jnp.minimum(_floor_tile(_block_lo(lo_ref, e, blk, rpe)), cap - WIN)

    def window_copy(e, blk, sl):
        start = pl.multiple_of(win_start(e, blk), BF16_TILE_ROWS)
        return pltpu.make_async_copy(y_ref.at[e, pl.ds(start, WIN), :], a_ref.at[sl, pl.ds(e * WIN, WIN), :],
                                     sem_ref.at[sl, e])

    @pl.when(b == 0)
    def _():
        for e in range(N_EXPERTS):
            window_copy(e, 0, 0).start()

    @pl.when(b + 1 < n_blk)
    def _():
        for e in range(N_EXPERTS):
            window_copy(e, b + 1, 1 - slot).start()

    pm = pm_ref[...]
    gm = gm_ref[...]
    ws0 = [win_start(e, b) for e in range(N_EXPERTS)]
    e_col = lax.broadcasted_iota(I32, (1, N_EXPERTS), 1)
    ws_row = jnp.zeros((1, N_EXPERTS), I32)
    for e in range(N_EXPERTS):
        ws_row = jnp.where(e_col == e, ws0[e], ws_row)
    rel = jnp.clip(pm - ws_row, -1, WIN).astype(F32).astype(BF16)
    spread = jnp.where(lax.broadcasted_iota(I32, (N_EXPERTS, N_EXPERTS * WIN), 1) // WIN
                       == lax.broadcasted_iota(I32, (N_EXPERTS, N_EXPERTS * WIN), 0), 1.0, 0.0).astype(BF16)
    tgt = jnp.dot(rel, spread, preferred_element_type=F32)
    gv = jnp.dot(gm.astype(BF16), spread, preferred_element_type=F32)
    lane_r = (lax.broadcasted_iota(I32, (TOK_BLOCK, N_EXPERTS * WIN), 1) & (WIN - 1)).astype(F32)
    gmat = jnp.where(tgt == lane_r, gv, 0.0).astype(BF16)

    for e in range(N_EXPERTS):
        window_copy(e, b, slot).wait()
    mixed = jnp.dot(gmat, a_ref[slot], preferred_element_type=F32)

    def finish(m):
        xo = x1_ref[...] + m
        out_ref[...] = (xo * lax.rsqrt(jnp.mean(xo * xo, axis=-1, keepdims=True) + RMS_EPS)) * nf_ref[...]

    finish(mixed)
    ends = [_block_lo(lo_ref, e, b, rpe) + _block_count(lo_ref, e, b, rpe, n_blk, cap) for e in range(N_EXPERTS)]
    n_wins = [lax.div(jnp.maximum(ends[e] - ws0[e], 0) + (WIN - 1), WIN) for e in range(N_EXPERTS)]

    @pl.when(functools.reduce(jnp.maximum, n_wins) > 1)
    def _():
        m_ref[...] = mixed
        lane_w = lax.broadcasted_iota(I32, (TOK_BLOCK, WIN), 1)
        for e in range(N_EXPERTS):
            def extra_window(k, carry):
                first = ws0[e] + k * WIN
                ws = jnp.minimum(first, cap - WIN)
                cp = pltpu.make_async_copy(y_ref.at[e, pl.ds(pl.multiple_of(ws, BF16_TILE_ROWS), WIN), :], ovw_ref,
                                           ovsem_ref)
                cp.start()
                cp.wait()
                pe = pm[:, e:e + 1]
                gk = jnp.where((lane_w == pe - ws) & (pe >= first), gm[:, e:e + 1], 0.0).astype(BF16)
                m_ref[...] += jnp.dot(gk, ovw_ref[...], preferred_element_type=F32)
                return carry

            lax.fori_loop(1, n_wins[e], extra_window, 0)
        finish(m_ref[...])


def _combine(lo, posm_t, gate_t, x1, norm_final, y2, cap):
    t = x1.shape[0]
    rpe = t // LANES
    n_blk = t // TOK_BLOCK
    grid_spec = pltpu.PrefetchScalarGridSpec(
        num_scalar_prefetch=1,
        grid=(n_blk,),
        in_specs=[pl.BlockSpec((TOK_BLOCK, N_EXPERTS), lambda b, lo_r: (b, 0)),
                  pl.BlockSpec((TOK_BLOCK, N_EXPERTS), lambda b, lo_r: (b, 0)),
                  pl.BlockSpec((TOK_BLOCK, D_MODEL), lambda b, lo_r: (b, 0)),
                  pl.BlockSpec((1, D_MODEL), lambda b, lo_r: (0, 0)),
                  pl.BlockSpec(memory_space=pl.ANY)],
        out_specs=pl.BlockSpec((TOK_BLOCK, D_MODEL), lambda b, lo_r: (b, 0)),
        scratch_shapes=[pltpu.VMEM((2, N_EXPERTS * COMBINE_WIN, D_MODEL), BF16),
                        pltpu.VMEM((TOK_BLOCK, D_MODEL), F32),
                        pltpu.VMEM((COMBINE_WIN, D_MODEL), BF16),
                        pltpu.SemaphoreType.DMA((2, N_EXPERTS)),
                        pltpu.SemaphoreType.DMA(())],
    )
    return pl.pallas_call(
        functools.partial(_combine_body, rpe=rpe, n_blk=n_blk, cap=cap),
        grid_spec=grid_spec,
        out_shape=jax.ShapeDtypeStruct((t, D_MODEL), F32),
        compiler_params=_params(("arbitrary",), 40),
        name="combine",
    )(lo, posm_t, gate_t, x1, norm_final, y2)


def _trunk(x, p):
    b, l, _ = x.shape
    t = b * l
    n_seq = l // S5_CHUNK
    cap = max(1, EC_CAPACITY_FACTOR * t // N_EXPERTS)
    rpe = t // LANES

    hg, u, u_sm, gates = _inproj(x.reshape(t, D_MODEL), p['norm_mix'], p['w_in'])
    o_fw, o_bw = _hgrn(hg.reshape(b, l, 5 * A_WIDTH), p['lb'])
    ys5 = _s5(u_sm, p['s5_perm'], *p['s5'], n_seq)

    x1, h2, logits = _merge(x.reshape(t, D_MODEL), o_fw.reshape(t, A_WIDTH), o_bw.reshape(t, A_WIDTH), hg, ys5, u,
                            gates, p['hgrn_norm'], p['s5_d'], p['w_glu'], p['b_glu'], p['w_a'], p['w_b'],
                            p['w_out'], p['norm_ffn'], p['w_router'])

    posm, gate, lo = _select(logits.T.reshape(N_EXPERTS * rpe, LANES), rpe, cap)
    lo = lo[:, 0]
    posm = posm.reshape(N_EXPERTS, t)
    xe = _dispatch(lo, posm, h2, cap)
    y2 = _ffn(xe, p['w_gate'], p['w_up'], p['w_down'], cap)
    out = _combine(lo, posm.T, gate.reshape(N_EXPERTS, t).T, x1, p['norm_final'], y2, cap)
    return out.reshape(b, l, D_MODEL)


def kernel(x_prompt, x_sample, norm_mix, w_in, hgrn_gamma, hgrn_norm, s5_a_re, s5_a_im, s5_log_dt, s5_b_re, s5_b_im, s5_c_re, s5_c_im, s5_d, s5_w_glu, s5_b_glu, w_branch_a, w_branch_b, w_out, norm_ffn, w_router, w_exp_gate, w_exp_up, w_exp_down, norm_final):
    assert norm_mix.shape[0] == 1, "single-layer trunk"
    row = lambda a: a.astype(F32).reshape(1, -1)
    lb = jnp.cumsum(jax.nn.softmax(hgrn_gamma.astype(F32), axis=0), axis=0)[0]
    max_chunks = max(x_prompt.shape[1], x_sample.shape[1]) // S5_CHUNK
    s5_ops = _s5_operators(s5_a_re[0], s5_a_im[0], s5_log_dt[0], s5_b_re[0], s5_b_im[0], s5_c_re[0], s5_c_im[0],
                           max_chunks.bit_length() - 1)
    p = {
        'norm_mix': row(norm_mix[0]), 'w_in': w_in[0].astype(BF16), 'lb': lb, 's5': s5_ops,
        's5_perm': _s5_lane_permutation(),
        'hgrn_norm': row(hgrn_norm[0]), 's5_d': row(s5_d[0]), 'w_glu': s5_w_glu[0].astype(BF16),
        'b_glu': row(s5_b_glu[0]), 'w_a': w_branch_a[0].astype(BF16), 'w_b': w_branch_b[0].astype(BF16),
        'w_out': w_out[0].astype(BF16), 'norm_ffn': row(norm_ffn[0]),
        'w_router': _split_router(w_router[0]),
        'w_gate': w_exp_gate[0], 'w_up': w_exp_up[0], 'w_down': w_exp_down[0],
        'norm_final': row(norm_final),
    }
    return (_trunk(x_prompt, p), _trunk(x_sample, p))
```

```python
import functools
import math

import jax
import jax.numpy as jnp
from jax import lax
from jax.experimental import pallas as pl
from jax.experimental.pallas import tpu as pltpu

F32 = jnp.float32
BF16 = jnp.bfloat16
I32 = jnp.int32

D_MODEL = 1024
A_HEADS = 4
A_HEAD_DIM = 128
A_WIDTH = A_HEADS * A_HEAD_DIM
A_CHUNK = 64
S5_WIDTH = 512
S5_GROUP = 16
S5_GROUPS = S5_WIDTH // S5_GROUP
S5_STATE = 64
N_EXPERTS = 16
EXPERT_FF = 2048
EC_CAPACITY_FACTOR = 2
RMS_EPS = 1e-6
IN_COLS = 5 * A_WIDTH + S5_WIDTH + 2 * D_MODEL

LANES = 128
BF16_TILE_ROWS = 16
S5_CHUNK = 16
S5_COLS = S5_CHUNK * S5_GROUP
S5_GROUPS_PER_STEP = LANES // S5_GROUP
TOK_BLOCK = 256
COMBINE_WIN = 64
DISPATCH_WIN = 64
MIB = 2 ** 20


def _params(semantics, vmem_mib):
    return pltpu.CompilerParams(dimension_semantics=semantics, vmem_limit_bytes=vmem_mib * MIB)


def _inproj_body(x_ref, g_ref, w_ref, hg_ref, u_ref, us_ref, gt_ref, uf_ref, *, tm):
    x = x_ref[...]
    ms = jnp.mean(x * x, axis=-1, keepdims=True)
    h = ((x * lax.rsqrt(ms + RMS_EPS)) * g_ref[...]).astype(BF16)
    n_hg = 5 * A_WIDTH // 512
    for c in range(IN_COLS // 512):
        r = jnp.dot(h, w_ref[:, c * 512:(c + 1) * 512], preferred_element_type=F32)
        if c < n_hg:
            hg_ref[:, c * 512:(c + 1) * 512] = r.astype(BF16)
        elif c == n_hg:
            u_ref[...] = r.astype(BF16)
            for cb in range(S5_WIDTH // LANES):
                uf_ref[cb] = r[:, cb * LANES:(cb + 1) * LANES]
                for sg in range(S5_CHUNK):
                    us_ref[sg, :, cb * LANES:(cb + 1) * LANES] = uf_ref[
                        cb, pl.ds(sg, tm // S5_CHUNK, stride=S5_CHUNK), :].astype(BF16)
        else:
            gt_ref[:, (c - n_hg - 1) * 512:(c - n_hg) * 512] = r.astype(BF16)


def _inproj(x, gain, w_bf16, tm=512):
    t = x.shape[0]
    return pl.pallas_call(
        functools.partial(_inproj_body, tm=tm),
        grid=(t // tm,),
        in_specs=[pl.BlockSpec((tm, D_MODEL), lambda i: (i, 0)),
                  pl.BlockSpec((1, D_MODEL), lambda i: (0, 0)),
                  pl.BlockSpec((D_MODEL, IN_COLS), lambda i: (0, 0))],
        out_specs=[pl.BlockSpec((tm, 5 * A_WIDTH), lambda i: (i, 0)),
                   pl.BlockSpec((tm, S5_WIDTH), lambda i: (i, 0)),
                   pl.BlockSpec((S5_CHUNK, tm // S5_CHUNK, S5_WIDTH), lambda i: (0, i, 0)),
                   pl.BlockSpec((tm, 2 * D_MODEL), lambda i: (i, 0))],
        out_shape=[jax.ShapeDtypeStruct((t, 5 * A_WIDTH), BF16),
                   jax.ShapeDtypeStruct((t, S5_WIDTH), BF16),
                   jax.ShapeDtypeStruct((S5_CHUNK, t // S5_CHUNK, S5_WIDTH), BF16),
                   jax.ShapeDtypeStruct((t, 2 * D_MODEL), BF16)],
        scratch_shapes=[pltpu.VMEM((S5_WIDTH // LANES, tm, LANES), F32)],
        compiler_params=_params(("arbitrary",), 48),
        name="inproj",
    )(x, gain, w_bf16)


def _hgrn_pair(q, fr, v, lb, s_ref, forward):
    c, c2 = A_CHUNK, 2 * A_CHUNK
    t_idx = lax.broadcasted_iota(I32, (c2, c2), 0)
    s_idx = lax.broadcasted_iota(I32, (c2, c2), 1)
    one = lambda cond: jnp.where(cond, 1.0, 0.0)
    s_loc = s_idx & (c - 1)
    same_chunk = (t_idx < c) == (s_idx < c)
    tri = same_chunk & ((s_idx <= t_idx) if forward else (s_idx >= t_idx))
    m_run = one(tri)
    m_mid = one(same_chunk & ((s_loc <= c // 2 - 1) if forward else (s_loc >= c // 2)))
    m_all = one(same_chunk)
    order = (0, 1) if forward else (1, 0)

    q = q.astype(F32)
    f = lb + (1.0 - lb) * jax.nn.sigmoid(fr.astype(F32))
    lf = jnp.log2(f)
    kk = 1.0 - f
    hi = lf.astype(BF16)
    hl = jnp.concatenate([hi, (lf - hi.astype(F32)).astype(BF16)], axis=0)

    def sums(m):
        mb = m.astype(BF16)
        return jnp.dot(jnp.concatenate([mb, mb], axis=1), hl, preferred_element_type=F32)

    e_mid = jnp.exp2(sums(m_run - m_mid))
    q_in = (q * e_mid).astype(BF16)
    k_in = kk * (1.0 / e_mid)
    qdec = (q * jnp.exp2(sums(m_run))).astype(BF16)
    kdec = kk * jnp.exp2(sums(m_all - m_run))
    total = [jnp.exp2(sums(one((s_idx[:8, :] < c) == (ch == 0)))[0:1, :]) for ch in (0, 1)]
    zero = jnp.zeros((), BF16)
    outs = []
    for h in range(A_HEADS):
        sl = slice(h * A_HEAD_DIM, (h + 1) * A_HEAD_DIM)
        vh = v[:, sl]
        k_t = k_in[:, sl].T.astype(BF16)
        kd_t = kdec[:, sl].T.astype(BF16)
        s = jnp.where(tri, jnp.dot(q_in[:, sl], k_t, preferred_element_type=F32), 0.0).astype(BF16)
        o_intra = jnp.dot(s, vh, preferred_element_type=F32)
        state = s_ref[h]
        o_inter = [None, None]
        for ch in order:
            rows = slice(ch * c, (ch + 1) * c)
            o_inter[ch] = jnp.dot(qdec[rows, sl], state.astype(BF16), preferred_element_type=F32)
            own_rows = (t_idx < c) if ch == 0 else (t_idx >= c)
            v_ch = jnp.where(own_rows, vh, zero)
            decay = jnp.broadcast_to(total[ch][:, sl], (c2, A_HEAD_DIM)).T
            state = decay * state + jnp.dot(kd_t, v_ch, preferred_element_type=F32)
        s_ref[h] = state
        outs.append(o_intra + jnp.concatenate(o_inter, axis=0))
    return jnp.concatenate(outs, axis=1).astype(BF16)


def _hgrn_body(qf_ref, ff_ref, vf_ref, qb_ref, fb_ref, vb_ref, lb_ref, of_ref, ob_ref, sf_ref, sb_ref,
               *, n_pairs):
    @pl.when(pl.program_id(1) == 0)
    def _():
        sf_ref[...] = jnp.zeros_like(sf_ref)
        sb_ref[...] = jnp.zeros_like(sb_ref)

    c2 = 2 * A_CHUNK

    def step(j, carry):
        rf = pl.ds(pl.multiple_of(j * c2, c2), c2)
        of_ref[0, rf, :] = _hgrn_pair(qf_ref[0, rf, :], ff_ref[0, rf, :], vf_ref[0, rf, :],
                                      lb_ref[0:1, :], sf_ref, True)
        rb = pl.ds(pl.multiple_of((n_pairs - 1 - j) * c2, c2), c2)
        ob_ref[0, rb, :] = _hgrn_pair(qb_ref[0, rb, :], fb_ref[0, rb, :], vb_ref[0, rb, :],
                                      lb_ref[1:2, :], sb_ref, False)
        return carry

    lax.fori_loop(0, n_pairs, step, 0, unroll=True)


def _hgrn(hg, lb, lb_tokens=1024):
    b, l, _ = hg.shape
    nb = l // lb_tokens
    blk = (1, lb_tokens, A_WIDTH)
    fwd = lambda col: pl.BlockSpec(blk, lambda bi, i: (bi, i, col))
    bwd = lambda col: pl.BlockSpec(blk, lambda bi, i: (bi, nb - 1 - i, col))
    return pl.pallas_call(
        functools.partial(_hgrn_body, n_pairs=lb_tokens // (2 * A_CHUNK)),
        grid=(b, nb),
        in_specs=[fwd(0), fwd(1), fwd(3), bwd(0), bwd(2), bwd(3),
                  pl.BlockSpec((2, A_WIDTH), lambda bi, i: (0, 0))],
        out_specs=[pl.BlockSpec(blk, lambda bi, i: (bi, i, 0)),
                   pl.BlockSpec(blk, lambda bi, i: (bi, nb - 1 - i, 0))],
        out_shape=[jax.ShapeDtypeStruct((b, l, A_WIDTH), BF16)] * 2,
        scratch_shapes=[pltpu.VMEM((A_HEADS, A_HEAD_DIM, A_HEAD_DIM), F32)] * 2,
        compiler_params=_params(("arbitrary", "arbitrary"), 32),
        name="hgrn2",
    )(hg, hg, hg, hg, hg, hg, lb)


def _s5_direction_terms(a_re, a_im, log_dt, b_re, b_im):
    a_re, a_im = a_re.astype(F32), a_im.astype(F32)
    b_re, b_im = b_re.astype(F32), b_im.astype(F32)
    dt = jnp.exp(log_dt.astype(F32))[:, None]
    j = jnp.arange(S5_CHUNK + 1, dtype=F32)[:, None, None]
    mag = jnp.exp(j * (dt * a_re))
    pw_re = mag * jnp.cos(j * (dt * a_im))
    pw_im = mag * jnp.sin(j * (dt * a_im))
    nr, ni = pw_re[1] - 1.0, pw_im[1]
    den = a_re * a_re + a_im * a_im
    f_re = (nr * a_re + ni * a_im) / den
    f_im = (ni * a_re - nr * a_im) / den
    bt_re, bt_im = b_re.transpose(0, 2, 1), b_im.transpose(0, 2, 1)
    bb_re = f_re[:, None, :] * bt_re - f_im[:, None, :] * bt_im
    bb_im = f_re[:, None, :] * bt_im + f_im[:, None, :] * bt_re
    pb_re = pw_re[:, :, None, :] * bb_re - pw_im[:, :, None, :] * bb_im
    pb_im = pw_re[:, :, None, :] * bb_im + pw_im[:, :, None, :] * bb_re
    return pw_re, pw_im, pb_re, pb_im


def _s5_assemble_body(strip_ref, inf_ref, inb_ref, outf_ref, outb_ref, w1_ref, w2_ref):
    c, half = S5_CHUNK, 2 * S5_STATE
    for g in range(strip_ref.shape[0]):
        strip = strip_ref[g]
        for s in range(c):
            lo = (c - 1 - s) * S5_GROUP
            w1_ref[g, s * S5_GROUP:(s + 1) * S5_GROUP, :S5_COLS] = strip[:, lo:lo + S5_COLS].astype(BF16)
        w1_ref[g, :, S5_COLS:S5_COLS + half] = inf_ref[g].astype(BF16)
        w1_ref[g, :, S5_COLS + half:] = inb_ref[g].astype(BF16)
        w2_ref[g, :half, :] = outf_ref[g].T.astype(BF16)
        w2_ref[g, half:, :] = outb_ref[g].T.astype(BF16)


def _s5_assemble(strip, w_in_f, w_in_b, w_out_f, w_out_b):
    g = strip.shape[0]
    gps = S5_GROUPS_PER_STEP
    per_g = lambda a: pl.BlockSpec((gps,) + a.shape[1:], lambda i: (i, 0, 0))
    return pl.pallas_call(
        _s5_assemble_body,
        grid=(g // gps,),
        in_specs=[per_g(strip), per_g(w_in_f), per_g(w_in_b), per_g(w_out_f), per_g(w_out_b)],
        out_specs=[pl.BlockSpec((gps, S5_COLS, 2 * S5_COLS), lambda i: (i, 0, 0)),
                   pl.BlockSpec((gps, S5_COLS, S5_COLS), lambda i: (i, 0, 0))],
        out_shape=[jax.ShapeDtypeStruct((g, S5_COLS, 2 * S5_COLS), BF16),
                   jax.ShapeDtypeStruct((g, S5_COLS, S5_COLS), BF16)],
        compiler_params=_params(("arbitrary",), 32),
        name="s5_assemble",
    )(strip, w_in_f, w_in_b, w_out_f, w_out_b)


def _s5_operators(a_re, a_im, log_dt, b_re, b_im, c_re, c_im, n_steps):
    hp = lax.Precision.HIGHEST
    c_re, c_im = c_re.astype(F32), c_im.astype(F32)
    g = S5_GROUPS
    terms = [_s5_direction_terms(a_re[d], a_im[d], log_dt[d], b_re, b_im) for d in range(2)]
    taps = []
    for pw_re, pw_im, pb_re, pb_im in terms:
        k = (jnp.einsum('gop,jgip->jgoi', c_re, pb_re[:S5_CHUNK], precision=hp)
             - jnp.einsum('gop,jgip->jgoi', c_im, pb_im[:S5_CHUNK], precision=hp))
        taps.append(k)
    c = S5_CHUNK
    strip = jnp.concatenate([taps[1][:0:-1], (taps[0][0] + taps[1][0])[None], taps[0][1:]], axis=0)
    strip = strip.transpose(1, 3, 0, 2).reshape(g, S5_GROUP, (2 * c - 1) * S5_GROUP)

    def w_in(pb_re, pb_im):
        both = jnp.concatenate([pb_re, pb_im], axis=-1)
        return both.transpose(1, 0, 2, 3).reshape(g, S5_COLS, 2 * S5_STATE)

    def w_out(pr, pi):
        m_re = c_re[None] * pr[:, :, None, :] - c_im[None] * pi[:, :, None, :]
        m_im = c_re[None] * pi[:, :, None, :] + c_im[None] * pr[:, :, None, :]
        both = jnp.concatenate([m_re, -m_im], axis=-1)
        return both.transpose(1, 0, 2, 3).reshape(g, S5_COLS, 2 * S5_STATE)

    w_in_f = w_in(terms[0][2][c - 1::-1], terms[0][3][c - 1::-1])
    w_in_b = w_in(terms[1][2][:c], terms[1][3][:c])
    w_out_f = w_out(terms[0][0][1:], terms[0][1][1:])
    w_out_b = w_out(terms[1][0][:0:-1], terms[1][1][:0:-1])
    w1, w2 = _s5_assemble(strip, w_in_f, w_in_b, w_out_f, w_out_b)

    ars, ais = [], []
    cur = [(t[0][S5_CHUNK], t[1][S5_CHUNK]) for t in terms]
    for _ in range(n_steps):
        ars.append(jnp.concatenate([cur[0][0], cur[0][0], cur[1][0], cur[1][0]], axis=-1))
        ais.append(jnp.concatenate([-cur[0][1], cur[0][1], -cur[1][1], cur[1][1]], axis=-1))
        cur = [(r * r - i * i, 2.0 * r * i) for r, i in cur]
    return w1, w2, jnp.stack(ars, axis=1), jnp.stack(ais, axis=1)


def _s5_group(u, w1, w2, ar, ai, n, n_seq, n_steps):
    rows = u.shape[0]
    half = 2 * S5_STATE
    yv = jnp.dot(u, w1, preferred_element_type=F32)
    zf = jnp.where(n >= 1, pltpu.roll(yv[:, S5_COLS:S5_COLS + half], 1, axis=0), 0.0)
    zb = jnp.where(n < n_seq - 1, pltpu.roll(yv[:, S5_COLS + half:], rows - 1, axis=0), 0.0)
    for k in range(n_steps):
        s = 1 << k
        sf = jnp.where(n >= s, pltpu.roll(zf, s, axis=0), 0.0)
        sb = jnp.where(n < n_seq - s, pltpu.roll(zb, rows - s, axis=0), 0.0)
        zf = zf + ar[k:k + 1, :half] * sf + ai[k:k + 1, :half] * pltpu.roll(sf, S5_STATE, axis=1)
        zb = zb + ar[k:k + 1, half:] * sb + ai[k:k + 1, half:] * pltpu.roll(sb, S5_STATE, axis=1)
    x = jnp.concatenate([zf, zb], axis=1).astype(BF16)
    return yv[:, :S5_COLS] + jnp.dot(x, w2, preferred_element_type=F32)


def _s5_body(u_ref, perm_ref, w1_ref, w2_ref, ar_ref, ai_ref, y_ref, z_ref, *, n_seq, n_steps):
    gps = S5_GROUPS_PER_STEP
    rows = u_ref.shape[1]
    n = lax.broadcasted_iota(I32, (rows, 2 * S5_STATE), 0) & (n_seq - 1)
    perm = perm_ref[...]
    for hf in range(S5_CHUNK // gps):
        x = jnp.concatenate([u_ref[hf * gps + j] for j in range(gps)], axis=1)
        z_ref[:, hf * gps * LANES:(hf + 1) * gps * LANES] = jnp.dot(
            x, perm, preferred_element_type=F32).astype(BF16)
    for g in range(gps):
        u = jnp.concatenate([z_ref[:, (hf * gps + g) * LANES:(hf * gps + g + 1) * LANES]
                             for hf in range(S5_CHUNK // gps)], axis=1)
        y = _s5_group(u, w1_ref[g], w2_ref[g], ar_ref[g], ai_ref[g], n, n_seq, n_steps).astype(BF16)
        for hf in range(S5_CHUNK // gps):
            z_ref[:, (hf * gps + g) * LANES:(hf * gps + g + 1) * LANES] = y[:, hf * LANES:(hf + 1) * LANES]
    for hf in range(S5_CHUNK // gps):
        x = jnp.dot(z_ref[:, hf * gps * LANES:(hf + 1) * gps * LANES], perm,
                    preferred_element_type=F32).astype(BF16)
        for j in range(gps):
            y_ref[hf * gps + j] = x[:, j * LANES:(j + 1) * LANES]


def _s5_lane_permutation():
    idx = jnp.arange(S5_GROUPS_PER_STEP * LANES)
    a, b, c = idx // LANES, (idx % LANES) // S5_GROUP, idx % S5_GROUP
    dst = b * LANES + a * S5_GROUP + c
    return (dst[:, None] == idx[None, :]).astype(BF16)


def _s5(us, perm, w1, w2, ar, ai, n_seq):
    _, rows, _ = us.shape
    assert n_seq & (n_seq - 1) == 0, n_seq
    n_steps = n_seq.bit_length() - 1
    gps = S5_GROUPS_PER_STEP
    tok = pl.BlockSpec((S5_CHUNK, rows, LANES), lambda i: (0, 0, i))
    per_g = lambda a: pl.BlockSpec((gps,) + a.shape[1:], lambda i: (i, 0, 0))
    return pl.pallas_call(
        functools.partial(_s5_body, n_seq=n_seq, n_steps=n_steps),
        grid=(S5_GROUPS // gps,),
        in_specs=[tok, pl.BlockSpec(perm.shape, lambda i: (0, 0)), per_g(w1), per_g(w2), per_g(ar), per_g(ai)],
        out_specs=tok,
        out_shape=jax.ShapeDtypeStruct(us.shape, BF16),
        scratch_shapes=[pltpu.VMEM((rows, S5_CHUNK * LANES), BF16)],
        compiler_params=_params(("arbitrary",), 48),
        name="s5",
    )(us, perm, w1, w2, ar, ai)


def _gelu_tanh(x):
    return 0.5 * x * (1.0 + jnp.tanh(math.sqrt(2.0 / math.pi) * (x + 0.044715 * (x * x * x))))


def _merge_body(x_ref, of_ref, ob_ref, go_ref, ys_ref, u_ref, ga_ref, gb_ref,
                hn_ref, sd_ref, wglu_ref, bglu_ref, wa_ref, wb_ref, wo_ref, nf_ref, wr_ref,
                x1_ref, h2_ref, lg_ref, yn_ref, *, tm):
    for sg in range(S5_CHUNK):
        for cb in range(S5_WIDTH // LANES):
            yn_ref[cb, pl.ds(sg, tm // S5_CHUNK, stride=S5_CHUNK), :] = ys_ref[
                sg, :, cb * LANES:(cb + 1) * LANES].astype(F32)
    o = of_ref[...].astype(F32) + ob_ref[...].astype(F32)
    parts = []
    for h in range(A_HEADS):
        oh = o[:, h * A_HEAD_DIM:(h + 1) * A_HEAD_DIM]
        parts.append(oh * lax.rsqrt(jnp.mean(oh * oh, axis=-1, keepdims=True) + RMS_EPS))
    h_a = jnp.concatenate(parts, axis=1) * hn_ref[...] * jax.nn.silu(go_ref[...].astype(F32))

    u = u_ref[...].astype(F32)
    y_s5 = jnp.concatenate([yn_ref[cb] for cb in range(S5_WIDTH // LANES)], axis=1)
    ys = _gelu_tanh(y_s5 + sd_ref[...] * u)
    z = jnp.dot(ys.astype(BF16), wglu_ref[...], preferred_element_type=F32) + bglu_ref[...]
    h_b = ys * jax.nn.sigmoid(z)

    y_a = jnp.dot(h_a.astype(BF16), wa_ref[...], preferred_element_type=F32)
    y_b = jnp.dot(h_b.astype(BF16), wb_ref[...], preferred_element_type=F32)
    merged = jax.nn.sigmoid(ga_ref[...].astype(F32)) * y_a + jax.nn.sigmoid(gb_ref[...].astype(F32)) * y_b
    x1 = x_ref[...] + jnp.dot(merged.astype(BF16), wo_ref[...], preferred_element_type=F32)
    x1_ref[...] = x1

    h2 = (x1 * lax.rsqrt(jnp.mean(x1 * x1, axis=-1, keepdims=True) + RMS_EPS)) * nf_ref[...]
    h2_hi = h2.astype(BF16)
    h2_ref[...] = h2_hi
    h2_lo = (h2 - h2_hi.astype(F32)).astype(BF16)
    r_hi = jnp.dot(h2_hi, wr_ref[...], preferred_element_type=F32)
    r_lo = jnp.dot(h2_lo, wr_ref[...], preferred_element_type=F32)
    lg_ref[...] = r_hi[:, :N_EXPERTS] + r_hi[:, N_EXPERTS:2 * N_EXPERTS] + r_lo[:, :N_EXPERTS]


def _split_router(w_router):
    w = w_router.astype(F32)
    hi = w.astype(BF16)
    lo = (w - hi.astype(F32)).astype(BF16)
    pad = jnp.zeros((w.shape[0], LANES - 2 * N_EXPERTS), BF16)
    return jnp.concatenate([hi, lo, pad], axis=1)


def _merge(x, o_fw, o_bw, hg, ys5, u, gates, hn, sd, wglu, bglu, wa, wb, wo, nf, wr_pad, tm=512):
    t = x.shape[0]
    row = lambda w, col=0: pl.BlockSpec((tm, w), lambda i: (i, col))
    full = lambda a: pl.BlockSpec(a.shape, lambda i: (0,) * a.ndim)
    return pl.pallas_call(
        functools.partial(_merge_body, tm=tm),
        grid=(t // tm,),
        in_specs=[row(D_MODEL), row(A_WIDTH), row(A_WIDTH), row(A_WIDTH, 4),
                  pl.BlockSpec((S5_CHUNK, tm // S5_CHUNK, S5_WIDTH), lambda i: (0, i, 0)), row(S5_WIDTH),
                  row(D_MODEL, 0), row(D_MODEL, 1),
                  full(hn), full(sd), full(wglu), full(bglu), full(wa), full(wb), full(wo), full(nf), full(wr_pad)],
        out_specs=[row(D_MODEL), row(D_MODEL), row(N_EXPERTS)],
        out_shape=[jax.ShapeDtypeStruct((t, D_MODEL), F32),
                   jax.ShapeDtypeStruct((t, D_MODEL), BF16),
                   jax.ShapeDtypeStruct((t, N_EXPERTS), F32)],
        scratch_shapes=[pltpu.VMEM((S5_WIDTH // LANES, tm, LANES), F32)],
        compiler_params=_params(("arbitrary",), 48),
        name="merge",
    )(x, o_fw, o_bw, hg, ys5, u, gates, gates, hn, sd, wglu, bglu, wa, wb, wo, nf, wr_pad)


def _exclusive_prefix(m, upper, strict_lower):
    mb = m.astype(BF16)
    incl = jnp.dot(mb, upper, preferred_element_type=F32)
    row_off = jnp.sum(jnp.dot(strict_lower, mb, preferred_element_type=F32), axis=1, keepdims=True)
    return incl - m + row_off, row_off


def _select_body(lg_ref, pos_ref, gate_ref, lo_ref, aff_ref, *, rpe, cap):
    e_n = N_EXPERTS
    slab = lambda ref, e: ref[e * rpe:(e + 1) * rpe, :]
    mx = slab(lg_ref, 0)
    for e in range(1, e_n):
        mx = jnp.maximum(mx, slab(lg_ref, e))
    den = jnp.zeros_like(mx)
    for e in range(e_n):
        ex = jnp.exp(slab(lg_ref, e) - mx)
        aff_ref[e * rpe:(e + 1) * rpe, :] = ex
        den = den + ex
    for e in range(e_n):
        aff_ref[e * rpe:(e + 1) * rpe, :] = slab(aff_ref, e) / den

    capf = float(cap)

    def bit_step(i, cand):
        bit = jnp.left_shift(jnp.int32(1), 30 - i)
        rows = []
        for e in range(e_n):
            cur = cand[e:e + 1, :]
            trial = cur | bit
            cnt = jnp.sum(jnp.where(slab(aff_ref, e) >= pltpu.bitcast(trial, F32), 1.0, 0.0), keepdims=True)
            rows.append(jnp.where(cnt >= capf, trial, cur))
        return jnp.concatenate(rows, axis=0)

    thr = pltpu.bitcast(lax.fori_loop(0, 31, bit_step, jnp.zeros((e_n, LANES), I32)), F32)

    upper = jnp.where(lax.broadcasted_iota(I32, (LANES, LANES), 0) <= lax.broadcasted_iota(I32, (LANES, LANES), 1),
                      1.0, 0.0).astype(BF16)
    strict_lower = jnp.where(lax.broadcasted_iota(I32, (rpe, rpe), 1) < lax.broadcasted_iota(I32, (rpe, rpe), 0),
                             1.0, 0.0).astype(BF16)
    for e in range(e_n):
        aff = slab(aff_ref, e)
        t = thr[e:e + 1, :]
        gt = jnp.where(aff > t, 1.0, 0.0)
        eq = jnp.where(aff == t, 1.0, 0.0)
        need = capf - jnp.sum(gt, keepdims=True)
        rank_eq, _ = _exclusive_prefix(eq, upper, strict_lower)
        sel = gt + eq * jnp.where(rank_eq < need, 1.0, 0.0)
        pos, row_off = _exclusive_prefix(sel, upper, strict_lower)
        chosen = sel > 0.5
        pos_ref[e * rpe:(e + 1) * rpe, :] = jnp.where(chosen, pos, -1.0).astype(I32)
        gate_ref[e * rpe:(e + 1) * rpe, :] = jnp.where(chosen, aff, 0.0)
        lo_ref[e * rpe:(e + 1) * rpe, :] = jnp.broadcast_to(row_off, (rpe, LANES)).astype(I32)


def _select(lg_em, rpe, cap):
    rows = N_EXPERTS * rpe
    shp = lambda dt: jax.ShapeDtypeStruct((rows, LANES), dt)
    return pl.pallas_call(
        functools.partial(_select_body, rpe=rpe, cap=cap),
        out_shape=[shp(I32), shp(F32), shp(I32)],
        scratch_shapes=[pltpu.VMEM((rows, LANES), F32)],
        compiler_params=pltpu.CompilerParams(vmem_limit_bytes=32 * MIB),
        name="select",
    )(lg_em)


def _block_lo(lo_ref, e, blk, rpe):
    return lo_ref[e * rpe + blk * (TOK_BLOCK // LANES)]


def _block_count(lo_ref, e, blk, rpe, n_blk, cap):
    nxt = lo_ref[e * rpe + jnp.minimum((blk + 1) * (TOK_BLOCK // LANES), rpe - 1)]
    return jnp.where(blk == n_blk - 1, cap, nxt) - _block_lo(lo_ref, e, blk, rpe)


def _floor_tile(row):
    return row - (row & (BF16_TILE_ROWS - 1))


def _dispatch_body(lo_ref, pm_ref, h_ref, xe_ref, stage_ref, carry_ref, sem_ref, *, rpe, n_blk, cap):
    b = pl.program_id(0)

    @pl.when(b == 0)
    def _():
        carry_ref[...] = jnp.zeros_like(carry_ref)

    pm = pm_ref[...]
    hb = h_ref[...]
    r_idx = lax.broadcasted_iota(I32, (DISPATCH_WIN, TOK_BLOCK), 0)
    head_row = lax.broadcasted_iota(I32, (BF16_TILE_ROWS, D_MODEL), 0)
    los = [_block_lo(lo_ref, e, b, rpe) for e in range(N_EXPERTS)]
    bases = [_floor_tile(lo) for lo in los]
    onehot = jnp.concatenate(
        [jnp.where(r_idx == pm[e:e + 1, :] - bases[e], 1.0, 0.0) for e in range(N_EXPERTS)], axis=0).astype(BF16)
    xw = jnp.dot(onehot, hb, preferred_element_type=F32)

    def window_copy(e, first_row):
        start = pl.multiple_of(first_row, BF16_TILE_ROWS)
        return pltpu.make_async_copy(stage_ref.at[e], xe_ref.at[e, pl.ds(start, DISPATCH_WIN), :], sem_ref.at[e])

    for e in range(N_EXPERTS):
        @pl.when(b > 0)
        def _():
            window_copy(e, 0).wait()

        rows = xw[e * DISPATCH_WIN:(e + 1) * DISPATCH_WIN, :]
        head = jnp.where(head_row < los[e] - bases[e], carry_ref[e].astype(F32), rows[:BF16_TILE_ROWS, :])
        stage_ref[e, :BF16_TILE_ROWS, :] = head.astype(BF16)
        stage_ref[e, BF16_TILE_ROWS:, :] = rows[BF16_TILE_ROWS:, :].astype(BF16)
        window_copy(e, bases[e]).start()

    ends = [los[e] + _block_count(lo_ref, e, b, rpe, n_blk, cap) for e in range(N_EXPERTS)]
    n_wins = [lax.div(ends[e] - bases[e] + (DISPATCH_WIN - 1), DISPATCH_WIN) for e in range(N_EXPERTS)]

    @pl.when(functools.reduce(jnp.maximum, n_wins) > 1)
    def _():
        for e in range(N_EXPERTS):
            def extra_window(k, carry):
                window_copy(e, 0).wait()
                oh = jnp.where(r_idx + k * DISPATCH_WIN == pm[e:e + 1, :] - bases[e], 1.0, 0.0).astype(BF16)
                stage_ref[e] = jnp.dot(oh, hb, preferred_element_type=F32).astype(BF16)
                window_copy(e, bases[e] + k * DISPATCH_WIN).start()
                return carry

            lax.fori_loop(1, n_wins[e], extra_window, 0)

    for e in range(N_EXPERTS):
        last_start = bases[e] + (jnp.maximum(n_wins[e], 1) - 1) * DISPATCH_WIN
        tile = jnp.minimum(_floor_tile(ends[e]) - last_start, DISPATCH_WIN - BF16_TILE_ROWS)
        carry_ref[e] = stage_ref[e, pl.ds(pl.multiple_of(tile, BF16_TILE_ROWS), BF16_TILE_ROWS), :]

    @pl.when(b == n_blk - 1)
    def _():
        for e in range(N_EXPERTS):
            window_copy(e, 0).wait()
            stage_ref[e] = jnp.zeros((DISPATCH_WIN, D_MODEL), BF16)
            window_copy(e, cap).start()
        for e in range(N_EXPERTS):
            window_copy(e, cap).wait()


def _dispatch(lo, posm, h2, cap):
    t = h2.shape[0]
    rpe = t // LANES
    n_blk = t // TOK_BLOCK
    grid_spec = pltpu.PrefetchScalarGridSpec(
        num_scalar_prefetch=1,
        grid=(n_blk,),
        in_specs=[pl.BlockSpec((N_EXPERTS, TOK_BLOCK), lambda b, lo_r: (0, b)),
                  pl.BlockSpec((TOK_BLOCK, D_MODEL), lambda b, lo_r: (b, 0))],
        out_specs=pl.BlockSpec(memory_space=pl.ANY),
        scratch_shapes=[pltpu.VMEM((N_EXPERTS, DISPATCH_WIN, D_MODEL), BF16),
                        pltpu.VMEM((N_EXPERTS, BF16_TILE_ROWS, D_MODEL), BF16),
                        pltpu.SemaphoreType.DMA((N_EXPERTS,))],
    )
    return pl.pallas_call(
        functools.partial(_dispatch_body, rpe=rpe, n_blk=n_blk, cap=cap),
        grid_spec=grid_spec,
        out_shape=jax.ShapeDtypeStruct((N_EXPERTS, cap + DISPATCH_WIN, D_MODEL), BF16),
        compiler_params=_params(("arbitrary",), 32),
        name="dispatch",
    )(lo, posm, h2)


def _ffn_body(x_ref, wg_ref, wu_ref, wd_ref, y_ref, acc_ref, *, n_f, n_sub, sub):
    f = pl.program_id(1)
    wg = wg_ref[0].astype(BF16)
    wu = wu_ref[0].astype(BF16)
    wd = wd_ref[0].astype(BF16)

    def sweep(first, last):
        def rows_step(i, carry):
            r = pl.ds(pl.multiple_of(i * sub, sub), sub)
            x = x_ref[0, r, :]
            a = jnp.dot(x, wg, preferred_element_type=F32)
            u = jnp.dot(x, wu, preferred_element_type=F32)
            h = (jax.nn.silu(a) * u).astype(BF16)
            d = jnp.dot(h, wd, preferred_element_type=F32)
            if not first:
                d = acc_ref[r, :] + d
            if last:
                y_ref[0, r, :] = d.astype(BF16)
            else:
                acc_ref[r, :] = d
            return carry

        lax.fori_loop(0, n_sub, rows_step, 0, unroll=True)

    if n_f == 1:
        sweep(True, True)
    else:
        pl.when(f == 0)(lambda: sweep(True, False))
        pl.when(jnp.logical_and(f > 0, f < n_f - 1))(lambda: sweep(False, False))
        pl.when(f == n_f - 1)(lambda: sweep(False, True))


def _ffn(xe, w_gate, w_up, w_down, cap, tf=512, sub=1024):
    sub = min(sub, cap)
    n_f = EXPERT_FF // tf
    return pl.pallas_call(
        functools.partial(_ffn_body, n_f=n_f, n_sub=cap // sub, sub=sub),
        grid=(N_EXPERTS, n_f),
        in_specs=[pl.BlockSpec((1, cap, D_MODEL), lambda e, f: (e, 0, 0)),
                  pl.BlockSpec((1, D_MODEL, tf), lambda e, f: (e, 0, f)),
                  pl.BlockSpec((1, D_MODEL, tf), lambda e, f: (e, 0, f)),
                  pl.BlockSpec((1, tf, D_MODEL), lambda e, f: (e, f, 0))],
        out_specs=pl.BlockSpec((1, cap, D_MODEL), lambda e, f: (e, 0, 0)),
        out_shape=jax.ShapeDtypeStruct((N_EXPERTS, cap, D_MODEL), BF16),
        scratch_shapes=[pltpu.VMEM((cap, D_MODEL), F32)],
        compiler_params=_params(("arbitrary", "arbitrary"), 56),
        name="expert_ffn",
    )(xe, w_gate, w_up, w_down)


def _combine_body(lo_ref, pm_ref, gm_ref, x1_ref, nf_ref, y_ref, out_ref,
                  a_ref, m_ref, ovw_ref, sem_ref, ovsem_ref, *, rpe, n_blk, cap):
    WIN = COMBINE_WIN
    b = pl.program_id(0)
    slot = b % 2

    def win_start(e, blk):
        return jnp.minimum(_floor_tile(_block_lo(lo_ref, e, blk, rpe)), cap - WIN)

    def window_copy(e, blk, sl):
        start = pl.multiple_of(win_start(e, blk), BF16_TILE_ROWS)
        return pltpu.make_async_copy(y_ref.at[e, pl.ds(start, WIN), :], a_ref.at[sl, pl.ds(e * WIN, WIN), :],
                                     sem_ref.at[sl, e])

    @pl.when(b == 0)
    def _():
        for e in range(N_EXPERTS):
            window_copy(e, 0, 0).start()

    @pl.when(b + 1 < n_blk)
    def _():
        for e in range(N_EXPERTS):
            window_copy(e, b + 1, 1 - slot).start()

    pm = pm_ref[...]
    gm = gm_ref[...]
    ws0 = [win_start(e, b) for e in range(N_EXPERTS)]
    e_col = lax.broadcasted_iota(I32, (1, N_EXPERTS), 1)
    ws_row = jnp.zeros((1, N_EXPERTS), I32)
    for e in range(N_EXPERTS):
        ws_row = jnp.where(e_col == e, ws0[e], ws_row)
    rel = jnp.clip(pm - ws_row, -1, WIN).astype(F32).astype(BF16)
    spread = jnp.where(lax.broadcasted_iota(I32, (N_EXPERTS, N_EXPERTS * WIN), 1) // WIN
                       == lax.broadcasted_iota(I32, (N_EXPERTS, N_EXPERTS * WIN), 0), 1.0, 0.0).astype(BF16)
    tgt = jnp.dot(rel, spread, preferred_element_type=F32)
    gv = jnp.dot(gm.astype(BF16), spread, preferred_element_type=F32)
    lane_r = (lax.broadcasted_iota(I32, (TOK_BLOCK, N_EXPERTS * WIN), 1) & (WIN - 1)).astype(F32)
    gmat = jnp.where(tgt == lane_r, gv, 0.0).astype(BF16)

    for e in range(N_EXPERTS):
        window_copy(e, b, slot).wait()
    mixed = jnp.dot(gmat, a_ref[slot], preferred_element_type=F32)

    def finish(m):
        xo = x1_ref[...] + m
        out_ref[...] = (xo * lax.rsqrt(jnp.mean(xo * xo, axis=-1, keepdims=True) + RMS_EPS)) * nf_ref[...]

    finish(mixed)
    ends = [_block_lo(lo_ref, e, b, rpe) + _block_count(lo_ref, e, b, rpe, n_blk, cap) for e in range(N_EXPERTS)]
    n_wins = [lax.div(jnp.maximum(ends[e] - ws0[e], 0) + (WIN - 1), WIN) for e in range(N_EXPERTS)]

    @pl.when(functools.reduce(jnp.maximum, n_wins) > 1)
    def _():
        m_ref[...] = mixed
        lane_w = lax.broadcasted_iota(I32, (TOK_BLOCK, WIN), 1)
        for e in range(N_EXPERTS):
            def extra_window(k, carry):
                first = ws0[e] + k * WIN
                ws = jnp.minimum(first, cap - WIN)
                cp = pltpu.make_async_copy(y_ref.at[e, pl.ds(pl.multiple_of(ws, BF16_TILE_ROWS), WIN), :], ovw_ref,
                                           ovsem_ref)
                cp.start()
                cp.wait()
                pe = pm[:, e:e + 1]
                gk = jnp.where((lane_w == pe - ws) & (pe >= first), gm[:, e:e + 1], 0.0).astype(BF16)
                m_ref[...] += jnp.dot(gk, ovw_ref[...], preferred_element_type=F32)
                return carry

            lax.fori_loop(1, n_wins[e], extra_window, 0)
        finish(m_ref[...])


def _combine(lo, posm_t, gate_t, x1, norm_final, y2, cap):
    t = x1.shape[0]
    rpe = t // LANES
    n_blk = t // TOK_BLOCK
    grid_spec = pltpu.PrefetchScalarGridSpec(
        num_scalar_prefetch=1,
        grid=(n_blk,),
        in_specs=[pl.BlockSpec((TOK_BLOCK, N_EXPERTS), lambda b, lo_r: (b, 0)),
                  pl.BlockSpec((TOK_BLOCK, N_EXPERTS), lambda b, lo_r: (b, 0)),
                  pl.BlockSpec((TOK_BLOCK, D_MODEL), lambda b, lo_r: (b, 0)),
                  pl.BlockSpec((1, D_MODEL), lambda b, lo_r: (0, 0)),
                  pl.BlockSpec(memory_space=pl.ANY)],
        out_specs=pl.BlockSpec((TOK_BLOCK, D_MODEL), lambda b, lo_r: (b, 0)),
        scratch_shapes=[pltpu.VMEM((2, N_EXPERTS * COMBINE_WIN, D_MODEL), BF16),
                        pltpu.VMEM((TOK_BLOCK, D_MODEL), F32),
                        pltpu.VMEM((COMBINE_WIN, D_MODEL), BF16),
                        pltpu.SemaphoreType.DMA((2, N_EXPERTS)),
                        pltpu.SemaphoreType.DMA(())],
    )
    return pl.pallas_call(
        functools.partial(_combine_body, rpe=rpe, n_blk=n_blk, cap=cap),
        grid_spec=grid_spec,
        out_shape=jax.ShapeDtypeStruct((t, D_MODEL), F32),
        compiler_params=_params(("arbitrary",), 40),
        name="combine",
    )(lo, posm_t, gate_t, x1, norm_final, y2)


def _trunk(x, p):
    b, l, _ = x.shape
    t = b * l
    n_seq = l // S5_CHUNK
    cap = max(1, EC_CAPACITY_FACTOR * t // N_EXPERTS)
    rpe = t // LANES

    hg, u, u_sm, gates = _inproj(x.reshape(t, D_MODEL), p['norm_mix'], p['w_in'])
    o_fw, o_bw = _hgrn(hg.reshape(b, l, 5 * A_WIDTH), p['lb'])
    ys5 = _s5(u_sm, p['s5_perm'], *p['s5'], n_seq)

    x1, h2, logits = _merge(x.reshape(t, D_MODEL), o_fw.reshape(t, A_WIDTH), o_bw.reshape(t, A_WIDTH), hg, ys5, u,
                            gates, p['hgrn_norm'], p['s5_d'], p['w_glu'], p['b_glu'], p['w_a'], p['w_b'],
                            p['w_out'], p['norm_ffn'], p['w_router'])

    posm, gate, lo = _select(logits.T.reshape(N_EXPERTS * rpe, LANES), rpe, cap)
    lo = lo[:, 0]
    posm = posm.reshape(N_EXPERTS, t)
    xe = _dispatch(lo, posm, h2, cap)
    y2 = _ffn(xe, p['w_gate'], p['w_up'], p['w_down'], cap)
    out = _combine(lo, posm.T, gate.reshape(N_EXPERTS, t).T, x1, p['norm_final'], y2, cap)
    return out.reshape(b, l, D_MODEL)


def kernel(x_prompt, x_sample, norm_mix, w_in, hgrn_gamma, hgrn_norm, s5_a_re, s5_a_im, s5_log_dt, s5_b_re, s5_b_im, s5_c_re, s5_c_im, s5_d, s5_w_glu, s5_b_glu, w_branch_a, w_branch_b, w_out, norm_ffn, w_router, w_exp_gate, w_exp_up, w_exp_down, norm_final):
    assert norm_mix.shape[0] == 1, "single-layer trunk"
    row = lambda a: a.astype(F32).reshape(1, -1)
    lb = jnp.cumsum(jax.nn.softmax(hgrn_gamma.astype(F32), axis=0), axis=0)[0]
    max_chunks = max(x_prompt.shape[1], x_sample.shape[1]) // S5_CHUNK
    s5_ops = _s5_operators(s5_a_re[0], s5_a_im[0], s5_log_dt[0], s5_b_re[0], s5_b_im[0], s5_c_re[0], s5_c_im[0],
                           max_chunks.bit_length() - 1)
    p = {
        'norm_mix': row(norm_mix[0]), 'w_in': w_in[0].astype(BF16), 'lb': lb, 's5': s5_ops,
        's5_perm': _s5_lane_permutation(),
        'hgrn_norm': row(hgrn_norm[0]), 's5_d': row(s5_d[0]), 'w_glu': s5_w_glu[0].astype(BF16),
        'b_glu': row(s5_b_glu[0]), 'w_a': w_branch_a[0].astype(BF16), 'w_b': w_branch_b[0].astype(BF16),
        'w_out': w_out[0].astype(BF16), 'norm_ffn': row(norm_ffn[0]),
        'w_router': _split_router(w_router[0]),
        'w_gate': w_exp_gate[0], 'w_up': w_exp_up[0], 'w_down': w_exp_down[0],
        'norm_final': row(norm_final),
    }
    return (_trunk(x_prompt, p), _trunk(x_sample, p))
```

```python
import functools
import math

import jax
import jax.numpy as jnp
from jax import lax
from jax.experimental import pallas as pl
from jax.experimental.pallas import tpu as pltpu

F32 = jnp.float32
BF16 = jnp.bfloat16
I32 = jnp.int32

D_MODEL = 1024
A_HEADS = 4
A_HEAD_DIM = 128
A_WIDTH = A_HEADS * A_HEAD_DIM
A_CHUNK = 64
S5_WIDTH = 512
S5_GROUP = 16
S5_GROUPS = S5_WIDTH // S5_GROUP
S5_STATE = 64
N_EXPERTS = 16
EXPERT_FF = 2048
EC_CAPACITY_FACTOR = 2
RMS_EPS = 1e-6
IN_COLS = 5 * A_WIDTH + S5_WIDTH + 2 * D_MODEL

LANES = 128
BF16_TILE_ROWS = 16
S5_CHUNK = 16
S5_COLS = S5_CHUNK * S5_GROUP
S5_GROUPS_PER_STEP = LANES // S5_GROUP
TOK_BLOCK = 256
COMBINE_WIN = 64
DISPATCH_WIN = 64
MIB = 2 ** 20


def _params(semantics, vmem_mib):
    return pltpu.CompilerParams(dimension_semantics=semantics, vmem_limit_bytes=vmem_mib * MIB)


def _inproj_body(x_ref, g_ref, w_ref, hg_ref, u_ref, us_ref, gt_ref, uf_ref, *, tm):
    x = x_ref[...]
    ms = jnp.mean(x * x, axis=-1, keepdims=True)
    h = ((x * lax.rsqrt(ms + RMS_EPS)) * g_ref[...]).astype(BF16)
    n_hg = 5 * A_WIDTH // 512
    for c in range(IN_COLS // 512):
        r = jnp.dot(h, w_ref[:, c * 512:(c + 1) * 512], preferred_element_type=F32)
        if c < n_hg:
            hg_ref[:, c * 512:(c + 1) * 512] = r.astype(BF16)
        elif c == n_hg:
            u_ref[...] = r.astype(BF16)
            for cb in range(S5_WIDTH // LANES):
                uf_ref[cb] = r[:, cb * LANES:(cb + 1) * LANES]
                for sg in range(S5_CHUNK):
                    us_ref[sg, :, cb * LANES:(cb + 1) * LANES] = uf_ref[
                        cb, pl.ds(sg, tm // S5_CHUNK, stride=S5_CHUNK), :].astype(BF16)
        else:
            gt_ref[:, (c - n_hg - 1) * 512:(c - n_hg) * 512] = r.astype(BF16)


def _inproj(x, gain, w_bf16, tm=512):
    t = x.shape[0]
    return pl.pallas_call(
        functools.partial(_inproj_body, tm=tm),
        grid=(t // tm,),
        in_specs=[pl.BlockSpec((tm, D_MODEL), lambda i: (i, 0)),
                  pl.BlockSpec((1, D_MODEL), lambda i: (0, 0)),
                  pl.BlockSpec((D_MODEL, IN_COLS), lambda i: (0, 0))],
        out_specs=[pl.BlockSpec((tm, 5 * A_WIDTH), lambda i: (i, 0)),
                   pl.BlockSpec((tm, S5_WIDTH), lambda i: (i, 0)),
                   pl.BlockSpec((S5_CHUNK, tm // S5_CHUNK, S5_WIDTH), lambda i: (0, i, 0)),
                   pl.BlockSpec((tm, 2 * D_MODEL), lambda i: (i, 0))],
        out_shape=[jax.ShapeDtypeStruct((t, 5 * A_WIDTH), BF16),
                   jax.ShapeDtypeStruct((t, S5_WIDTH), BF16),
                   jax.ShapeDtypeStruct((S5_CHUNK, t // S5_CHUNK, S5_WIDTH), BF16),
                   jax.ShapeDtypeStruct((t, 2 * D_MODEL), BF16)],
        scratch_shapes=[pltpu.VMEM((S5_WIDTH // LANES, tm, LANES), F32)],
        compiler_params=_params(("arbitrary",), 48),
        name="inproj",
    )(x, gain, w_bf16)


def _hgrn_pair(q, fr, v, lb, s_ref, forward):
    c, c2 = A_CHUNK, 2 * A_CHUNK
    t_idx = lax.broadcasted_iota(I32, (c2, c2), 0)
    s_idx = lax.broadcasted_iota(I32, (c2, c2), 1)
    one = lambda cond: jnp.where(cond, 1.0, 0.0)
    s_loc = s_idx & (c - 1)
    same_chunk = (t_idx < c) == (s_idx < c)
    tri = same_chunk & ((s_idx <= t_idx) if forward else (s_idx >= t_idx))
    m_run = one(tri)
    m_mid = one(same_chunk & ((s_loc <= c // 2 - 1) if forward else (s_loc >= c // 2)))
    m_all = one(same_chunk)
    order = (0, 1) if forward else (1, 0)

    q = q.astype(F32)
    f = lb + (1.0 - lb) * jax.nn.sigmoid(fr.astype(F32))
    lf = jnp.log2(f)
    kk = 1.0 - f
    hi = lf.astype(BF16)
    hl = jnp.concatenate([hi, (lf - hi.astype(F32)).astype(BF16)], axis=0)

    def sums(m):
        mb = m.astype(BF16)
        return jnp.dot(jnp.concatenate([mb, mb], axis=1), hl, preferred_element_type=F32)

    e_mid = jnp.exp2(sums(m_run - m_mid))
    q_in = (q * e_mid).astype(BF16)
    k_in = kk * (1.0 / e_mid)
    qdec = (q * jnp.exp2(sums(m_run))).astype(BF16)
    kdec = kk * jnp.exp2(sums(m_all - m_run))
    total = [sums(one(jnp.broadcast_to((s_idx < c) == (ch == 0), (c2, c2)))) for ch in (0, 1)]
    zero = jnp.zeros((), BF16)
    outs = []
    for h in range(A_HEADS):
        sl = slice(h * A_HEAD_DIM, (h + 1) * A_HEAD_DIM)
        vh = v[:, sl]
        k_t = k_in[:, sl].T.astype(BF16)
        kd_t = kdec[:, sl].T.astype(BF16)
        s = jnp.where(tri, jnp.dot(q_in[:, sl], k_t, preferred_element_type=F32), 0.0).astype(BF16)
        o_intra = jnp.dot(s, vh, preferred_element_type=F32)
        state = s_ref[h]
        o_inter = [None, None]
        for ch in order:
            rows = slice(ch * c, (ch + 1) * c)
            o_inter[ch] = jnp.dot(qdec[rows, sl], state.astype(BF16), preferred_element_type=F32)
            own_rows = (t_idx < c) if ch == 0 else (t_idx >= c)
            v_ch = jnp.where(own_rows, vh, zero)
            decay = jnp.exp2(total[ch][:, sl].T)
            state = decay * state + jnp.dot(kd_t, v_ch, preferred_element_type=F32)
        s_ref[h] = state
        outs.append(o_intra + jnp.concatenate(o_inter, axis=0))
    return jnp.concatenate(outs, axis=1).astype(BF16)


def _hgrn_body(qf_ref, ff_ref, vf_ref, qb_ref, fb_ref, vb_ref, lb_ref, of_ref, ob_ref, sf_ref, sb_ref,
               *, n_pairs):
    @pl.when(pl.program_id(1) == 0)
    def _():
        sf_ref[...] = jnp.zeros_like(sf_ref)
        sb_ref[...] = jnp.zeros_like(sb_ref)

    c2 = 2 * A_CHUNK

    def step(j, carry):
        rf = pl.ds(pl.multiple_of(j * c2, c2), c2)
        of_ref[0, rf, :] = _hgrn_pair(qf_ref[0, rf, :], ff_ref[0, rf, :], vf_ref[0, rf, :],
                                      lb_ref[0:1, :], sf_ref, True)
        rb = pl.ds(pl.multiple_of((n_pairs - 1 - j) * c2, c2), c2)
        ob_ref[0, rb, :] = _hgrn_pair(qb_ref[0, rb, :], fb_ref[0, rb, :], vb_ref[0, rb, :],
                                      lb_ref[1:2, :], sb_ref, False)
        return carry

    lax.fori_loop(0, n_pairs, step, 0, unroll=True)


def _hgrn(hg, lb, lb_tokens=1024):
    b, l, _ = hg.shape
    nb = l // lb_tokens
    blk = (1, lb_tokens, A_WIDTH)
    fwd = lambda col: pl.BlockSpec(blk, lambda bi, i: (bi, i, col))
    bwd = lambda col: pl.BlockSpec(blk, lambda bi, i: (bi, nb - 1 - i, col))
    return pl.pallas_call(
        functools.partial(_hgrn_body, n_pairs=lb_tokens // (2 * A_CHUNK)),
        grid=(b, nb),
        in_specs=[fwd(0), fwd(1), fwd(3), bwd(0), bwd(2), bwd(3),
                  pl.BlockSpec((2, A_WIDTH), lambda bi, i: (0, 0))],
        out_specs=[pl.BlockSpec(blk, lambda bi, i: (bi, i, 0)),
                   pl.BlockSpec(blk, lambda bi, i: (bi, nb - 1 - i, 0))],
        out_shape=[jax.ShapeDtypeStruct((b, l, A_WIDTH), BF16)] * 2,
        scratch_shapes=[pltpu.VMEM((A_HEADS, A_HEAD_DIM, A_HEAD_DIM), F32)] * 2,
        compiler_params=_params(("arbitrary", "arbitrary"), 32),
        name="hgrn2",
    )(hg, hg, hg, hg, hg, hg, lb)


def _s5_direction_terms(a_re, a_im, log_dt, b_re, b_im):
    a_re, a_im = a_re.astype(F32), a_im.astype(F32)
    b_re, b_im = b_re.astype(F32), b_im.astype(F32)
    dt = jnp.exp(log_dt.astype(F32))[:, None]
    j = jnp.arange(S5_CHUNK + 1, dtype=F32)[:, None, None]
    mag = jnp.exp(j * (dt * a_re))
    pw_re = mag * jnp.cos(j * (dt * a_im))
    pw_im = mag * jnp.sin(j * (dt * a_im))
    nr, ni = pw_re[1] - 1.0, pw_im[1]
    den = a_re * a_re + a_im * a_im
    f_re = (nr * a_re + ni * a_im) / den
    f_im = (ni * a_re - nr * a_im) / den
    bb_re = f_re[..., None] * b_re - f_im[..., None] * b_im
    bb_im = f_re[..., None] * b_im + f_im[..., None] * b_re
    pb_re = pw_re[..., None] * bb_re - pw_im[..., None] * bb_im
    pb_im = pw_re[..., None] * bb_im + pw_im[..., None] * bb_re
    return pw_re, pw_im, pb_re, pb_im


def _s5_assemble_body(strip_ref, inf_ref, inb_ref, outf_ref, outb_ref, w1_ref, w2_ref):
    c, half = S5_CHUNK, 2 * S5_STATE
    for g in range(strip_ref.shape[0]):
        strip = strip_ref[g]
        for s in range(c):
            lo = (c - 1 - s) * S5_GROUP
            w1_ref[g, s * S5_GROUP:(s + 1) * S5_GROUP, :S5_COLS] = strip[:, lo:lo + S5_COLS].astype(BF16)
        w1_ref[g, :, S5_COLS:S5_COLS + half] = inf_ref[g].astype(BF16)
        w1_ref[g, :, S5_COLS + half:] = inb_ref[g].astype(BF16)
        w2_ref[g, :half, :] = outf_ref[g].astype(BF16)
        w2_ref[g, half:, :] = outb_ref[g].astype(BF16)


def _s5_assemble(strip, w_in_f, w_in_b, w_out_f, w_out_b):
    g = strip.shape[0]
    gps = S5_GROUPS_PER_STEP
    per_g = lambda a: pl.BlockSpec((gps,) + a.shape[1:], lambda i: (i, 0, 0))
    return pl.pallas_call(
        _s5_assemble_body,
        grid=(g // gps,),
        in_specs=[per_g(strip), per_g(w_in_f), per_g(w_in_b), per_g(w_out_f), per_g(w_out_b)],
        out_specs=[pl.BlockSpec((gps, S5_COLS, 2 * S5_COLS), lambda i: (i, 0, 0)),
                   pl.BlockSpec((gps, S5_COLS, S5_COLS), lambda i: (i, 0, 0))],
        out_shape=[jax.ShapeDtypeStruct((g, S5_COLS, 2 * S5_COLS), BF16),
                   jax.ShapeDtypeStruct((g, S5_COLS, S5_COLS), BF16)],
        compiler_params=_params(("arbitrary",), 32),
        name="s5_assemble",
    )(strip, w_in_f, w_in_b, w_out_f, w_out_b)


def _s5_operators(a_re, a_im, log_dt, b_re, b_im, c_re, c_im, n_steps):
    hp = lax.Precision.HIGHEST
    c_re, c_im = c_re.astype(F32), c_im.astype(F32)
    g = S5_GROUPS
    terms = [_s5_direction_terms(a_re[d], a_im[d], log_dt[d], b_re, b_im) for d in range(2)]
    taps = []
    for pw_re, pw_im, pb_re, pb_im in terms:
        k = (jnp.einsum('gop,jgpi->jgoi', c_re, pb_re[:S5_CHUNK], precision=hp)
             - jnp.einsum('gop,jgpi->jgoi', c_im, pb_im[:S5_CHUNK], precision=hp))
        taps.append(k)
    c = S5_CHUNK
    strip = jnp.concatenate([taps[1][:0:-1], (taps[0][0] + taps[1][0])[None], taps[0][1:]], axis=0)
    strip = strip.transpose(1, 3, 0, 2).reshape(g, S5_GROUP, (2 * c - 1) * S5_GROUP)

    def w_in(pb_re, pb_im):
        re = pb_re.transpose(1, 0, 3, 2)
        im = pb_im.transpose(1, 0, 3, 2)
        return jnp.concatenate([re, im], axis=-1).reshape(g, S5_COLS, 2 * S5_STATE)

    def w_out(pr, pi):
        m_re = c_re[None] * pr[:, :, None, :] - c_im[None] * pi[:, :, None, :]
        m_im = c_re[None] * pi[:, :, None, :] + c_im[None] * pr[:, :, None, :]
        re = m_re.transpose(1, 3, 0, 2).reshape(g, S5_STATE, S5_COLS)
        im = -m_im.transpose(1, 3, 0, 2).reshape(g, S5_STATE, S5_COLS)
        return jnp.concatenate([re, im], axis=1)

    w_in_f = w_in(terms[0][2][c - 1::-1], terms[0][3][c - 1::-1])
    w_in_b = w_in(terms[1][2][:c], terms[1][3][:c])
    w_out_f = w_out(terms[0][0][1:], terms[0][1][1:])
    w_out_b = w_out(terms[1][0][:0:-1], terms[1][1][:0:-1])
    w1, w2 = _s5_assemble(strip, w_in_f, w_in_b, w_out_f, w_out_b)

    ars, ais = [], []
    cur = [(t[0][S5_CHUNK], t[1][S5_CHUNK]) for t in terms]
    for _ in range(n_steps):
        ars.append(jnp.concatenate([cur[0][0], cur[0][0], cur[1][0], cur[1][0]], axis=-1))
        ais.append(jnp.concatenate([-cur[0][1], cur[0][1], -cur[1][1], cur[1][1]], axis=-1))
        cur = [(r * r - i * i, 2.0 * r * i) for r, i in cur]
    return w1, w2, jnp.stack(ars, axis=1), jnp.stack(ais, axis=1)


def _s5_group(u, w1, w2, ar, ai, n, n_seq, n_steps):
    rows = u.shape[0]
    half = 2 * S5_STATE
    yv = jnp.dot(u, w1, preferred_element_type=F32)
    zf = jnp.where(n >= 1, pltpu.roll(yv[:, S5_COLS:S5_COLS + half], 1, axis=0), 0.0)
    zb = jnp.where(n < n_seq - 1, pltpu.roll(yv[:, S5_COLS + half:], rows - 1, axis=0), 0.0)
    for k in range(n_steps):
        s = 1 << k
        sf = jnp.where(n >= s, pltpu.roll(zf, s, axis=0), 0.0)
        sb = jnp.where(n < n_seq - s, pltpu.roll(zb, rows - s, axis=0), 0.0)
        zf = zf + ar[k:k + 1, :half] * sf + ai[k:k + 1, :half] * pltpu.roll(sf, S5_STATE, axis=1)
        zb = zb + ar[k:k + 1, half:] * sb + ai[k:k + 1, half:] * pltpu.roll(sb, S5_STATE, axis=1)
    x = jnp.concatenate([zf, zb], axis=1).astype(BF16)
    return yv[:, :S5_COLS] + jnp.dot(x, w2, preferred_element_type=F32)


def _s5_body(u_ref, perm_ref, w1_ref, w2_ref, ar_ref, ai_ref, y_ref, z_ref, *, n_seq, n_steps):
    gps = S5_GROUPS_PER_STEP
    rows = u_ref.shape[1]
    n = lax.broadcasted_iota(I32, (rows, 2 * S5_STATE), 0) & (n_seq - 1)
    perm = perm_ref[...]
    for hf in range(S5_CHUNK // gps):
        x = jnp.concatenate([u_ref[hf * gps + j] for j in range(gps)], axis=1)
        z_ref[:, hf * gps * LANES:(hf + 1) * gps * LANES] = jnp.dot(
            x, perm, preferred_element_type=F32).astype(BF16)
    for g in range(gps):
        u = jnp.concatenate([z_ref[:, (hf * gps + g) * LANES:(hf * gps + g + 1) * LANES]
                             for hf in range(S5_CHUNK // gps)], axis=1)
        y = _s5_group(u, w1_ref[g], w2_ref[g], ar_ref[g], ai_ref[g], n, n_seq, n_steps).astype(BF16)
        for hf in range(S5_CHUNK // gps):
            z_ref[:, (hf * gps + g) * LANES:(hf * gps + g + 1) * LANES] = y[:, hf * LANES:(hf + 1) * LANES]
    for hf in range(S5_CHUNK // gps):
        x = jnp.dot(z_ref[:, hf * gps * LANES:(hf + 1) * gps * LANES], perm,
                    preferred_element_type=F32).astype(BF16)
        for j in range(gps):
            y_ref[hf * gps + j] = x[:, j * LANES:(j + 1) * LANES]


def _s5_lane_permutation():
    idx = jnp.arange(S5_GROUPS_PER_STEP * LANES)
    a, b, c = idx // LANES, (idx % LANES) // S5_GROUP, idx % S5_GROUP
    dst = b * LANES + a * S5_GROUP + c
    return (dst[:, None] == idx[None, :]).astype(BF16)


def _s5(us, perm, w1, w2, ar, ai, n_seq):
    _, rows, _ = us.shape
    assert n_seq & (n_seq - 1) == 0, n_seq
    n_steps = n_seq.bit_length() - 1
    gps = S5_GROUPS_PER_STEP
    tok = pl.BlockSpec((S5_CHUNK, rows, LANES), lambda i: (0, 0, i))
    per_g = lambda a: pl.BlockSpec((gps,) + a.shape[1:], lambda i: (i, 0, 0))
    return pl.pallas_call(
        functools.partial(_s5_body, n_seq=n_seq, n_steps=n_steps),
        grid=(S5_GROUPS // gps,),
        in_specs=[tok, pl.BlockSpec(perm.shape, lambda i: (0, 0)), per_g(w1), per_g(w2), per_g(ar), per_g(ai)],
        out_specs=tok,
        out_shape=jax.ShapeDtypeStruct(us.shape, BF16),
        scratch_shapes=[pltpu.VMEM((rows, S5_CHUNK * LANES), BF16)],
        compiler_params=_params(("arbitrary",), 48),
        name="s5",
    )(us, perm, w1, w2, ar, ai)


def _gelu_tanh(x):
    return 0.5 * x * (1.0 + jnp.tanh(math.sqrt(2.0 / math.pi) * (x + 0.044715 * (x * x * x))))


def _merge_body(x_ref, of_ref, ob_ref, go_ref, ys_ref, u_ref, ga_ref, gb_ref,
                hn_ref, sd_ref, wglu_ref, bglu_ref, wa_ref, wb_ref, wo_ref, nf_ref, wr_ref,
                x1_ref, h2_ref, lg_ref, yn_ref, *, tm):
    for sg in range(S5_CHUNK):
        for cb in range(S5_WIDTH // LANES):
            yn_ref[cb, pl.ds(sg, tm // S5_CHUNK, stride=S5_CHUNK), :] = ys_ref[
                sg, :, cb * LANES:(cb + 1) * LANES].astype(F32)
    o = of_ref[...].astype(F32) + ob_ref[...].astype(F32)
    parts = []
    for h in range(A_HEADS):
        oh = o[:, h * A_HEAD_DIM:(h + 1) * A_HEAD_DIM]
        parts.append(oh * lax.rsqrt(jnp.mean(oh * oh, axis=-1, keepdims=True) + RMS_EPS))
    h_a = jnp.concatenate(parts, axis=1) * hn_ref[...] * jax.nn.silu(go_ref[...].astype(F32))

    u = u_ref[...].astype(F32)
    y_s5 = jnp.concatenate([yn_ref[cb] for cb in range(S5_WIDTH // LANES)], axis=1)
    ys = _gelu_tanh(y_s5 + sd_ref[...] * u)
    z = jnp.dot(ys.astype(BF16), wglu_ref[...], preferred_element_type=F32) + bglu_ref[...]
    h_b = ys * jax.nn.sigmoid(z)

    y_a = jnp.dot(h_a.astype(BF16), wa_ref[...], preferred_element_type=F32)
    y_b = jnp.dot(h_b.astype(BF16), wb_ref[...], preferred_element_type=F32)
    merged = jax.nn.sigmoid(ga_ref[...].astype(F32)) * y_a + jax.nn.sigmoid(gb_ref[...].astype(F32)) * y_b
    x1 = x_ref[...] + jnp.dot(merged.astype(BF16), wo_ref[...], preferred_element_type=F32)
    x1_ref[...] = x1

    h2 = (x1 * lax.rsqrt(jnp.mean(x1 * x1, axis=-1, keepdims=True) + RMS_EPS)) * nf_ref[...]
    h2_hi = h2.astype(BF16)
    h2_ref[...] = h2_hi
    h2_lo = (h2 - h2_hi.astype(F32)).astype(BF16)
    r_hi = jnp.dot(h2_hi, wr_ref[...], preferred_element_type=F32)
    r_lo = jnp.dot(h2_lo, wr_ref[...], preferred_element_type=F32)
    lg_ref[...] = r_hi[:, :N_EXPERTS] + r_hi[:, N_EXPERTS:2 * N_EXPERTS] + r_lo[:, :N_EXPERTS]


def _split_router(w_router):
    w = w_router.astype(F32)
    hi = w.astype(BF16)
    lo = (w - hi.astype(F32)).astype(BF16)
    pad = jnp.zeros((w.shape[0], LANES - 2 * N_EXPERTS), BF16)
    return jnp.concatenate([hi, lo, pad], axis=1)


def _merge(x, o_fw, o_bw, hg, ys5, u, gates, hn, sd, wglu, bglu, wa, wb, wo, nf, wr_pad, tm=512):
    t = x.shape[0]
    row = lambda w, col=0: pl.BlockSpec((tm, w), lambda i: (i, col))
    full = lambda a: pl.BlockSpec(a.shape, lambda i: (0,) * a.ndim)
    return pl.pallas_call(
        functools.partial(_merge_body, tm=tm),
        grid=(t // tm,),
        in_specs=[row(D_MODEL), row(A_WIDTH), row(A_WIDTH), row(A_WIDTH, 4),
                  pl.BlockSpec((S5_CHUNK, tm // S5_CHUNK, S5_WIDTH), lambda i: (0, i, 0)), row(S5_WIDTH),
                  row(D_MODEL, 0), row(D_MODEL, 1),
                  full(hn), full(sd), full(wglu), full(bglu), full(wa), full(wb), full(wo), full(nf), full(wr_pad)],
        out_specs=[row(D_MODEL), row(D_MODEL), row(N_EXPERTS)],
        out_shape=[jax.ShapeDtypeStruct((t, D_MODEL), F32),
                   jax.ShapeDtypeStruct((t, D_MODEL), BF16),
                   jax.ShapeDtypeStruct((t, N_EXPERTS), F32)],
        scratch_shapes=[pltpu.VMEM((S5_WIDTH // LANES, tm, LANES), F32)],
        compiler_params=_params(("arbitrary",), 48),
        name="merge",
    )(x, o_fw, o_bw, hg, ys5, u, gates, gates, hn, sd, wglu, bglu, wa, wb, wo, nf, wr_pad)


def _exclusive_prefix(m, upper, strict_lower):
    mb = m.astype(BF16)
    incl = jnp.dot(mb, upper, preferred_element_type=F32)
    row_off = jnp.sum(jnp.dot(strict_lower, mb, preferred_element_type=F32), axis=1, keepdims=True)
    return incl - m + row_off, row_off


def _select_body(lg_ref, pos_ref, gate_ref, lo_ref, aff_ref, *, rpe, cap):
    e_n = N_EXPERTS
    slab = lambda ref, e: ref[e * rpe:(e + 1) * rpe, :]
    mx = slab(lg_ref, 0)
    for e in range(1, e_n):
        mx = jnp.maximum(mx, slab(lg_ref, e))
    den = jnp.zeros_like(mx)
    for e in range(e_n):
        ex = jnp.exp(slab(lg_ref, e) - mx)
        aff_ref[e * rpe:(e + 1) * rpe, :] = ex
        den = den + ex
    for e in range(e_n):
        aff_ref[e * rpe:(e + 1) * rpe, :] = slab(aff_ref, e) / den

    capf = float(cap)

    def bit_step(i, cand):
        bit = jnp.left_shift(jnp.int32(1), 30 - i)
        rows = []
        for e in range(e_n):
            cur = cand[e:e + 1, :]
            trial = cur | bit
            cnt = jnp.sum(jnp.where(slab(aff_ref, e) >= pltpu.bitcast(trial, F32), 1.0, 0.0), keepdims=True)
            rows.append(jnp.where(cnt >= capf, trial, cur))
        return jnp.concatenate(rows, axis=0)

    thr = pltpu.bitcast(lax.fori_loop(0, 31, bit_step, jnp.zeros((e_n, LANES), I32)), F32)

    upper = jnp.where(lax.broadcasted_iota(I32, (LANES, LANES), 0) <= lax.broadcasted_iota(I32, (LANES, LANES), 1),
                      1.0, 0.0).astype(BF16)
    strict_lower = jnp.where(lax.broadcasted_iota(I32, (rpe, rpe), 1) < lax.broadcasted_iota(I32, (rpe, rpe), 0),
                             1.0, 0.0).astype(BF16)
    for e in range(e_n):
        aff = slab(aff_ref, e)
        t = thr[e:e + 1, :]
        gt = jnp.where(aff > t, 1.0, 0.0)
        eq = jnp.where(aff == t, 1.0, 0.0)
        need = capf - jnp.sum(gt, keepdims=True)
        rank_eq, _ = _exclusive_prefix(eq, upper, strict_lower)
        sel = gt + eq * jnp.where(rank_eq < need, 1.0, 0.0)
        pos, row_off = _exclusive_prefix(sel, upper, strict_lower)
        chosen = sel > 0.5
        pos_ref[e * rpe:(e + 1) * rpe, :] = jnp.where(chosen, pos, -1.0).astype(I32)
        gate_ref[e * rpe:(e + 1) * rpe, :] = jnp.where(chosen, aff, 0.0)
        lo_ref[e * rpe:(e + 1) * rpe, :] = jnp.broadcast_to(row_off, (rpe, LANES)).astype(I32)


def _select(lg_em, rpe, cap):
    rows = N_EXPERTS * rpe
    shp = lambda dt: jax.ShapeDtypeStruct((rows, LANES), dt)
    return pl.pallas_call(
        functools.partial(_select_body, rpe=rpe, cap=cap),
        out_shape=[shp(I32), shp(F32), shp(I32)],
        scratch_shapes=[pltpu.VMEM((rows, LANES), F32)],
        compiler_params=pltpu.CompilerParams(vmem_limit_bytes=32 * MIB),
        name="select",
    )(lg_em)


def _block_lo(lo_ref, e, blk, rpe):
    return lo_ref[e * rpe + blk * (TOK_BLOCK // LANES)]


def _block_count(lo_ref, e, blk, rpe, n_blk, cap):
    nxt = lo_ref[e * rpe + jnp.minimum((blk + 1) * (TOK_BLOCK // LANES), rpe - 1)]
    return jnp.where(blk == n_blk - 1, cap, nxt) - _block_lo(lo_ref, e, blk, rpe)


def _floor_tile(row):
    return row - (row & (BF16_TILE_ROWS - 1))


def _dispatch_body(lo_ref, pm_ref, h_ref, xe_ref, stage_ref, carry_ref, sem_ref, *, rpe, n_blk, cap):
    b = pl.program_id(0)

    @pl.when(b == 0)
    def _():
        carry_ref[...] = jnp.zeros_like(carry_ref)

    pm = pm_ref[...]
    hb = h_ref[...]
    r_idx = lax.broadcasted_iota(I32, (DISPATCH_WIN, TOK_BLOCK), 0)
    head_row = lax.broadcasted_iota(I32, (BF16_TILE_ROWS, D_MODEL), 0)
    los = [_block_lo(lo_ref, e, b, rpe) for e in range(N_EXPERTS)]
    bases = [_floor_tile(lo) for lo in los]
    onehot = jnp.concatenate(
        [jnp.where(r_idx == pm[e:e + 1, :] - bases[e], 1.0, 0.0) for e in range(N_EXPERTS)], axis=0).astype(BF16)
    xw = jnp.dot(onehot, hb, preferred_element_type=F32)

    def window_copy(e, first_row):
        start = pl.multiple_of(first_row, BF16_TILE_ROWS)
        return pltpu.make_async_copy(stage_ref.at[e], xe_ref.at[e, pl.ds(start, DISPATCH_WIN), :], sem_ref.at[e])

    for e in range(N_EXPERTS):
        @pl.when(b > 0)
        def _():
            window_copy(e, 0).wait()

        rows = xw[e * DISPATCH_WIN:(e + 1) * DISPATCH_WIN, :]
        head = jnp.where(head_row < los[e] - bases[e], carry_ref[e].astype(F32), rows[:BF16_TILE_ROWS, :])
        stage_ref[e, :BF16_TILE_ROWS, :] = head.astype(BF16)
        stage_ref[e, BF16_TILE_ROWS:, :] = rows[BF16_TILE_ROWS:, :].astype(BF16)
        window_copy(e, bases[e]).start()

    ends = [los[e] + _block_count(lo_ref, e, b, rpe, n_blk, cap) for e in range(N_EXPERTS)]
    n_wins = [lax.div(ends[e] - bases[e] + (DISPATCH_WIN - 1), DISPATCH_WIN) for e in range(N_EXPERTS)]

    @pl.when(functools.reduce(jnp.maximum, n_wins) > 1)
    def _():
        for e in range(N_EXPERTS):
            def extra_window(k, carry):
                window_copy(e, 0).wait()
                oh = jnp.where(r_idx + k * DISPATCH_WIN == pm[e:e + 1, :] - bases[e], 1.0, 0.0).astype(BF16)
                stage_ref[e] = jnp.dot(oh, hb, preferred_element_type=F32).astype(BF16)
                window_copy(e, bases[e] + k * DISPATCH_WIN).start()
                return carry

            lax.fori_loop(1, n_wins[e], extra_window, 0)

    for e in range(N_EXPERTS):
        last_start = bases[e] + (jnp.maximum(n_wins[e], 1) - 1) * DISPATCH_WIN
        tile = jnp.minimum(_floor_tile(ends[e]) - last_start, DISPATCH_WIN - BF16_TILE_ROWS)
        carry_ref[e] = stage_ref[e, pl.ds(pl.multiple_of(tile, BF16_TILE_ROWS), BF16_TILE_ROWS), :]

    @pl.when(b == n_blk - 1)
    def _():
        for e in range(N_EXPERTS):
            window_copy(e, 0).wait()
            stage_ref[e] = jnp.zeros((DISPATCH_WIN, D_MODEL), BF16)
            window_copy(e, cap).start()
        for e in range(N_EXPERTS):
            window_copy(e, cap).wait()


def _dispatch(lo, posm, h2, cap):
    t = h2.shape[0]
    rpe = t // LANES
    n_blk = t // TOK_BLOCK
    grid_spec = pltpu.PrefetchScalarGridSpec(
        num_scalar_prefetch=1,
        grid=(n_blk,),
        in_specs=[pl.BlockSpec((N_EXPERTS, TOK_BLOCK), lambda b, lo_r: (0, b)),
                  pl.BlockSpec((TOK_BLOCK, D_MODEL), lambda b, lo_r: (b, 0))],
        out_specs=pl.BlockSpec(memory_space=pl.ANY),
        scratch_shapes=[pltpu.VMEM((N_EXPERTS, DISPATCH_WIN, D_MODEL), BF16),
                        pltpu.VMEM((N_EXPERTS, BF16_TILE_ROWS, D_MODEL), BF16),
                        pltpu.SemaphoreType.DMA((N_EXPERTS,))],
    )
    return pl.pallas_call(
        functools.partial(_dispatch_body, rpe=rpe, n_blk=n_blk, cap=cap),
        grid_spec=grid_spec,
        out_shape=jax.ShapeDtypeStruct((N_EXPERTS, cap + DISPATCH_WIN, D_MODEL), BF16),
        compiler_params=_params(("arbitrary",), 32),
        name="dispatch",
    )(lo, posm, h2)


def _ffn_body(x_ref, wg_ref, wu_ref, wd_ref, y_ref, acc_ref, *, n_f, n_sub, sub):
    f = pl.program_id(1)
    wg = wg_ref[0].astype(BF16)
    wu = wu_ref[0].astype(BF16)
    wd = wd_ref[0].astype(BF16)

    def sweep(first, last):
        def rows_step(i, carry):
            r = pl.ds(pl.multiple_of(i * sub, sub), sub)
            x = x_ref[0, r, :]
            a = jnp.dot(x, wg, preferred_element_type=F32)
            u = jnp.dot(x, wu, preferred_element_type=F32)
            h = (jax.nn.silu(a) * u).astype(BF16)
            d = jnp.dot(h, wd, preferred_element_type=F32)
            if not first:
                d = acc_ref[r, :] + d
            if last:
                y_ref[0, r, :] = d.astype(BF16)
            else:
                acc_ref[r, :] = d
            return carry

        lax.fori_loop(0, n_sub, rows_step, 0, unroll=True)

    if n_f == 1:
        sweep(True, True)
    else:
        pl.when(f == 0)(lambda: sweep(True, False))
        pl.when(jnp.logical_and(f > 0, f < n_f - 1))(lambda: sweep(False, False))
        pl.when(f == n_f - 1)(lambda: sweep(False, True))


def _ffn(xe, w_gate, w_up, w_down, cap, tf=512, sub=1024):
    sub = min(sub, cap)
    n_f = EXPERT_FF // tf
    return pl.pallas_call(
        functools.partial(_ffn_body, n_f=n_f, n_sub=cap // sub, sub=sub),
        grid=(N_EXPERTS, n_f),
        in_specs=[pl.BlockSpec((1, cap, D_MODEL), lambda e, f: (e, 0, 0)),
                  pl.BlockSpec((1, D_MODEL, tf), lambda e, f: (e, 0, f)),
                  pl.BlockSpec((1, D_MODEL, tf), lambda e, f: (e, 0, f)),
                  pl.BlockSpec((1, tf, D_MODEL), lambda e, f: (e, f, 0))],
        out_specs=pl.BlockSpec((1, cap, D_MODEL), lambda e, f: (e, 0, 0)),
        out_shape=jax.ShapeDtypeStruct((N_EXPERTS, cap, D_MODEL), BF16),
        scratch_shapes=[pltpu.VMEM((cap, D_MODEL), F32)],
        compiler_params=_params(("arbitrary", "arbitrary"), 56),
        name="expert_ffn",
    )(xe, w_gate, w_up, w_down)


def _combine_body(lo_ref, pm_ref, gm_ref, x1_ref, nf_ref, y_ref, out_ref,
                  a_ref, m_ref, ovw_ref, sem_ref, ovsem_ref, *, rpe, n_blk, cap):
    WIN = COMBINE_WIN
    b = pl.program_id(0)
    slot = b % 2

    def win_start(e, blk):
        return jnp.minimum(_floor_tile(_block_lo(lo_ref, e, blk, rpe)), cap - WIN)

    def window_copy(e, blk, sl):
        start = pl.multiple_of(win_start(e, blk), BF16_TILE_ROWS)
        return pltpu.make_async_copy(y_ref.at[e, pl.ds(start, WIN), :], a_ref.at[sl, pl.ds(e * WIN, WIN), :],
                                     sem_ref.at[sl, e])

    @pl.when(b == 0)
    def _():
        for e in range(N_EXPERTS):
            window_copy(e, 0, 0).start()

    @pl.when(b + 1 < n_blk)
    def _():
        for e in range(N_EXPERTS):
            window_copy(e, b + 1, 1 - slot).start()

    pm = pm_ref[...]
    gm = gm_ref[...]
    ws0 = [win_start(e, b) for e in range(N_EXPERTS)]
    e_col = lax.broadcasted_iota(I32, (1, N_EXPERTS), 1)
    ws_row = jnp.zeros((1, N_EXPERTS), I32)
    for e in range(N_EXPERTS):
        ws_row = jnp.where(e_col == e, ws0[e], ws_row)
    rel = jnp.clip(pm - ws_row, -1, WIN).astype(F32).astype(BF16)
    spread = jnp.where(lax.broadcasted_iota(I32, (N_EXPERTS, N_EXPERTS * WIN), 1) // WIN
                       == lax.broadcasted_iota(I32, (N_EXPERTS, N_EXPERTS * WIN), 0), 1.0, 0.0).astype(BF16)
    tgt = jnp.dot(rel, spread, preferred_element_type=F32)
    gv = jnp.dot(gm.astype(BF16), spread, preferred_element_type=F32)
    lane_r = (lax.broadcasted_iota(I32, (TOK_BLOCK, N_EXPERTS * WIN), 1) & (WIN - 1)).astype(F32)
    gmat = jnp.where(tgt == lane_r, gv, 0.0).astype(BF16)

    for e in range(N_EXPERTS):
        window_copy(e, b, slot).wait()
    mixed = jnp.dot(gmat, a_ref[slot], preferred_element_type=F32)

    def finish(m):
        xo = x1_ref[...] + m
        out_ref[...] = (xo * lax.rsqrt(jnp.mean(xo * xo, axis=-1, keepdims=True) + RMS_EPS)) * nf_ref[...]

    finish(mixed)
    ends = [_block_lo(lo_ref, e, b, rpe) + _block_count(lo_ref, e, b, rpe, n_blk, cap) for e in range(N_EXPERTS)]
    n_wins = [lax.div(jnp.maximum(ends[e] - ws0[e], 0) + (WIN - 1), WIN) for e in range(N_EXPERTS)]

    @pl.when(functools.reduce(jnp.maximum, n_wins) > 1)
    def _():
        m_ref[...] = mixed
        lane_w = lax.broadcasted_iota(I32, (TOK_BLOCK, WIN), 1)
        for e in range(N_EXPERTS):
            def extra_window(k, carry):
                first = ws0[e] + k * WIN
                ws = jnp.minimum(first, cap - WIN)
                cp = pltpu.make_async_copy(y_ref.at[e, pl.ds(pl.multiple_of(ws, BF16_TILE_ROWS), WIN), :], ovw_ref,
                                           ovsem_ref)
                cp.start()
                cp.wait()
                pe = pm[:, e:e + 1]
                gk = jnp.where((lane_w == pe - ws) & (pe >= first), gm[:, e:e + 1], 0.0).astype(BF16)
                m_ref[...] += jnp.dot(gk, ovw_ref[...], preferred_element_type=F32)
                return carry

            lax.fori_loop(1, n_wins[e], extra_window, 0)
        finish(m_ref[...])


def _combine(lo, posm_t, gate_t, x1, norm_final, y2, cap):
    t = x1.shape[0]
    rpe = t // LANES
    n_blk = t // TOK_BLOCK
    grid_spec = pltpu.PrefetchScalarGridSpec(
        num_scalar_prefetch=1,
        grid=(n_blk,),
        in_specs=[pl.BlockSpec((TOK_BLOCK, N_EXPERTS), lambda b, lo_r: (b, 0)),
                  pl.BlockSpec((TOK_BLOCK, N_EXPERTS), lambda b, lo_r: (b, 0)),
                  pl.BlockSpec((TOK_BLOCK, D_MODEL), lambda b, lo_r: (b, 0)),
                  pl.BlockSpec((1, D_MODEL), lambda b, lo_r: (0, 0)),
                  pl.BlockSpec(memory_space=pl.ANY)],
        out_specs=pl.BlockSpec((TOK_BLOCK, D_MODEL), lambda b, lo_r: (b, 0)),
        scratch_shapes=[pltpu.VMEM((2, N_EXPERTS * COMBINE_WIN, D_MODEL), BF16),
                        pltpu.VMEM((TOK_BLOCK, D_MODEL), F32),
                        pltpu.VMEM((COMBINE_WIN, D_MODEL), BF16),
                        pltpu.SemaphoreType.DMA((2, N_EXPERTS)),
                        pltpu.SemaphoreType.DMA(())],
    )
    return pl.pallas_call(
        functools.partial(_combine_body, rpe=rpe, n_blk=n_blk, cap=cap),
        grid_spec=grid_spec,
        out_shape=jax.ShapeDtypeStruct((t, D_MODEL), F32),
        compiler_params=_params(("arbitrary",), 40),
        name="combine",
    )(lo, posm_t, gate_t, x1, norm_final, y2)


def _trunk(x, p):
    b, l, _ = x.shape
    t = b * l
    n_seq = l // S5_CHUNK
    cap = max(1, EC_CAPACITY_FACTOR * t // N_EXPERTS)
    rpe = t // LANES

    hg, u, u_sm, gates = _inproj(x.reshape(t, D_MODEL), p['norm_mix'], p['w_in'])
    o_fw, o_bw = _hgrn(hg.reshape(b, l, 5 * A_WIDTH), p['lb'])
    ys5 = _s5(u_sm, p['s5_perm'], *p['s5'], n_seq)

    x1, h2, logits = _merge(x.reshape(t, D_MODEL), o_fw.reshape(t, A_WIDTH), o_bw.reshape(t, A_WIDTH), hg, ys5, u,
                            gates, p['hgrn_norm'], p['s5_d'], p['w_glu'], p['b_glu'], p['w_a'], p['w_b'],
                            p['w_out'], p['norm_ffn'], p['w_router'])

    posm, gate, lo = _select(logits.T.reshape(N_EXPERTS * rpe, LANES), rpe, cap)
    lo = lo[:, 0]
    posm = posm.reshape(N_EXPERTS, t)
    xe = _dispatch(lo, posm, h2, cap)
    y2 = _ffn(xe, p['w_gate'], p['w_up'], p['w_down'], cap)
    out = _combine(lo, posm.T, gate.reshape(N_EXPERTS, t).T, x1, p['norm_final'], y2, cap)
    return out.reshape(b, l, D_MODEL)


def kernel(x_prompt, x_sample, norm_mix, w_in, hgrn_gamma, hgrn_norm, s5_a_re, s5_a_im, s5_log_dt, s5_b_re, s5_b_im, s5_c_re, s5_c_im, s5_d, s5_w_glu, s5_b_glu, w_branch_a, w_branch_b, w_out, norm_ffn, w_router, w_exp_gate, w_exp_up, w_exp_down, norm_final):
    assert norm_mix.shape[0] == 1, "single-layer trunk"
    row = lambda a: a.astype(F32).reshape(1, -1)
    lb = jnp.cumsum(jax.nn.softmax(hgrn_gamma.astype(F32), axis=0), axis=0)[0]
    max_chunks = max(x_prompt.shape[1], x_sample.shape[1]) // S5_CHUNK
    s5_ops = _s5_operators(s5_a_re[0], s5_a_im[0], s5_log_dt[0], s5_b_re[0], s5_b_im[0], s5_c_re[0], s5_c_im[0],
                           max_chunks.bit_length() - 1)
    p = {
        'norm_mix': row(norm_mix[0]), 'w_in': w_in[0].astype(BF16), 'lb': lb, 's5': s5_ops,
        's5_perm': _s5_lane_permutation(),
        'hgrn_norm': row(hgrn_norm[0]), 's5_d': row(s5_d[0]), 'w_glu': s5_w_glu[0].astype(BF16),
        'b_glu': row(s5_b_glu[0]), 'w_a': w_branch_a[0].astype(BF16), 'w_b': w_branch_b[0].astype(BF16),
        'w_out': w_out[0].astype(BF16), 'norm_ffn': row(norm_ffn[0]),
        'w_router': _split_router(w_router[0]),
        'w_gate': w_exp_gate[0], 'w_up': w_exp_up[0], 'w_down': w_exp_down[0],
        'norm_final': row(norm_final),
    }
    return (_trunk(x_prompt, p), _trunk(x_sample, p))
```

```python
import functools
import math

import jax
import jax.numpy as jnp
from jax import lax
from jax.experimental import pallas as pl
from jax.experimental.pallas import tpu as pltpu

F32 = jnp.float32
BF16 = jnp.bfloat16
I32 = jnp.int32

D_MODEL = 1024
A_HEADS = 4
A_HEAD_DIM = 128
A_WIDTH = A_HEADS * A_HEAD_DIM
A_CHUNK = 64
S5_WIDTH = 512
S5_GROUP = 16
S5_GROUPS = S5_WIDTH // S5_GROUP
S5_STATE = 64
N_EXPERTS = 16
EXPERT_FF = 2048
EC_CAPACITY_FACTOR = 2
RMS_EPS = 1e-6
IN_COLS = 5 * A_WIDTH + S5_WIDTH + 2 * D_MODEL

LANES = 128
BF16_TILE_ROWS = 16
S5_CHUNK = 16
S5_COLS = S5_CHUNK * S5_GROUP
S5_GROUPS_PER_STEP = LANES // S5_GROUP
TOK_BLOCK = 256
COMBINE_WIN = 64
DISPATCH_WIN = 64
MIB = 2 ** 20


def _params(semantics, vmem_mib):
    return pltpu.CompilerParams(dimension_semantics=semantics, vmem_limit_bytes=vmem_mib * MIB)


def _inproj_body(x_ref, g_ref, w_ref, hg_ref, u_ref, us_ref, gt_ref, uf_ref, *, tm):
    x = x_ref[...]
    ms = jnp.mean(x * x, axis=-1, keepdims=True)
    h = ((x * lax.rsqrt(ms + RMS_EPS)) * g_ref[...]).astype(BF16)
    n_hg = 5 * A_WIDTH // 512
    for c in range(IN_COLS // 512):
        r = jnp.dot(h, w_ref[:, c * 512:(c + 1) * 512], preferred_element_type=F32)
        if c < n_hg:
            hg_ref[:, c * 512:(c + 1) * 512] = r.astype(BF16)
        elif c == n_hg:
            u_ref[...] = r.astype(BF16)
            for cb in range(S5_WIDTH // LANES):
                uf_ref[cb] = r[:, cb * LANES:(cb + 1) * LANES]
                for sg in range(S5_CHUNK):
                    us_ref[sg, :, cb * LANES:(cb + 1) * LANES] = uf_ref[
                        cb, pl.ds(sg, tm // S5_CHUNK, stride=S5_CHUNK), :].astype(BF16)
        else:
            gt_ref[:, (c - n_hg - 1) * 512:(c - n_hg) * 512] = r.astype(BF16)


def _inproj(x, gain, w_bf16, tm=512):
    t = x.shape[0]
    return pl.pallas_call(
        functools.partial(_inproj_body, tm=tm),
        grid=(t // tm,),
        in_specs=[pl.BlockSpec((tm, D_MODEL), lambda i: (i, 0)),
                  pl.BlockSpec((1, D_MODEL), lambda i: (0, 0)),
                  pl.BlockSpec((D_MODEL, IN_COLS), lambda i: (0, 0))],
        out_specs=[pl.BlockSpec((tm, 5 * A_WIDTH), lambda i: (i, 0)),
                   pl.BlockSpec((tm, S5_WIDTH), lambda i: (i, 0)),
                   pl.BlockSpec((S5_CHUNK, tm // S5_CHUNK, S5_WIDTH), lambda i: (0, i, 0)),
                   pl.BlockSpec((tm, 2 * D_MODEL), lambda i: (i, 0))],
        out_shape=[jax.ShapeDtypeStruct((t, 5 * A_WIDTH), BF16),
                   jax.ShapeDtypeStruct((t, S5_WIDTH), BF16),
                   jax.ShapeDtypeStruct((S5_CHUNK, t // S5_CHUNK, S5_WIDTH), BF16),
                   jax.ShapeDtypeStruct((t, 2 * D_MODEL), BF16)],
        scratch_shapes=[pltpu.VMEM((S5_WIDTH // LANES, tm, LANES), F32)],
        compiler_params=_params(("arbitrary",), 48),
        name="inproj",
    )(x, gain, w_bf16)


def _hgrn_pair(q, fr, v, lb, s_ref, forward):
    c, c2 = A_CHUNK, 2 * A_CHUNK
    t_idx = lax.broadcasted_iota(I32, (c2, c2), 0)
    s_idx = lax.broadcasted_iota(I32, (c2, c2), 1)
    one = lambda cond: jnp.where(cond, 1.0, 0.0)
    s_loc = s_idx & (c - 1)
    same_chunk = (t_idx < c) == (s_idx < c)
    tri = same_chunk & ((s_idx <= t_idx) if forward else (s_idx >= t_idx))
    m_run = one(tri)
    m_mid = one(same_chunk & ((s_loc <= c // 2 - 1) if forward else (s_loc >= c // 2)))
    m_all = one(same_chunk)
    order = (0, 1) if forward else (1, 0)

    q = q.astype(F32)
    f = lb + (1.0 - lb) * jax.nn.sigmoid(fr.astype(F32))
    lf = jnp.log2(f)
    kk = 1.0 - f
    hi = lf.astype(BF16)
    hl = jnp.concatenate([hi, (lf - hi.astype(F32)).astype(BF16)], axis=0)

    def sums(m):
        mb = m.astype(BF16)
        return jnp.dot(jnp.concatenate([mb, mb], axis=1), hl, preferred_element_type=F32)

    e_mid = jnp.exp2(sums(m_run - m_mid))
    q_in = (q * e_mid).astype(BF16)
    k_in = kk * (1.0 / e_mid)
    qdec = (q * jnp.exp2(sums(m_run))).astype(BF16)
    kdec = kk * jnp.exp2(sums(m_all - m_run))
    total = [sums(one(jnp.broadcast_to((s_idx < c) == (ch == 0), (c2, c2)))) for ch in (0, 1)]
    zero = jnp.zeros((), BF16)
    outs = []
    for h in range(A_HEADS):
        sl = slice(h * A_HEAD_DIM, (h + 1) * A_HEAD_DIM)
        vh = v[:, sl]
        k_t = k_in[:, sl].T.astype(BF16)
        kd_t = kdec[:, sl].T.astype(BF16)
        s = jnp.where(tri, jnp.dot(q_in[:, sl], k_t, preferred_element_type=F32), 0.0).astype(BF16)
        o_intra = jnp.dot(s, vh, preferred_element_type=F32)
        state = s_ref[h]
        o_inter = [None, None]
        for ch in order:
            rows = slice(ch * c, (ch + 1) * c)
            o_inter[ch] = jnp.dot(qdec[rows, sl], state.astype(BF16), preferred_element_type=F32)
            own_rows = (t_idx < c) if ch == 0 else (t_idx >= c)
            v_ch = jnp.where(own_rows, vh, zero)
            decay = jnp.exp2(total[ch][:, sl].T)
            state = decay * state + jnp.dot(kd_t, v_ch, preferred_element_type=F32)
        s_ref[h] = state
        outs.append(o_intra + jnp.concatenate(o_inter, axis=0))
    return jnp.concatenate(outs, axis=1).astype(BF16)


def _hgrn_body(qf_ref, ff_ref, vf_ref, qb_ref, fb_ref, vb_ref, lb_ref, of_ref, ob_ref, sf_ref, sb_ref,
               *, n_pairs):
    @pl.when(pl.program_id(1) == 0)
    def _():
        sf_ref[...] = jnp.zeros_like(sf_ref)
        sb_ref[...] = jnp.zeros_like(sb_ref)

    c2 = 2 * A_CHUNK

    def step(j, carry):
        rf = pl.ds(pl.multiple_of(j * c2, c2), c2)
        of_ref[0, rf, :] = _hgrn_pair(qf_ref[0, rf, :], ff_ref[0, rf, :], vf_ref[0, rf, :],
                                      lb_ref[0:1, :], sf_ref, True)
        rb = pl.ds(pl.multiple_of((n_pairs - 1 - j) * c2, c2), c2)
        ob_ref[0, rb, :] = _hgrn_pair(qb_ref[0, rb, :], fb_ref[0, rb, :], vb_ref[0, rb, :],
                                      lb_ref[1:2, :], sb_ref, False)
        return carry

    lax.fori_loop(0, n_pairs, step, 0, unroll=True)


def _hgrn(hg, lb, lb_tokens=1024):
    b, l, _ = hg.shape
    nb = l // lb_tokens
    blk = (1, lb_tokens, A_WIDTH)
    fwd = lambda col: pl.BlockSpec(blk, lambda bi, i: (bi, i, col))
    bwd = lambda col: pl.BlockSpec(blk, lambda bi, i: (bi, nb - 1 - i, col))
    return pl.pallas_call(
        functools.partial(_hgrn_body, n_pairs=lb_tokens // (2 * A_CHUNK)),
        grid=(b, nb),
        in_specs=[fwd(0), fwd(1), fwd(3), bwd(0), bwd(2), bwd(3),
                  pl.BlockSpec((2, A_WIDTH), lambda bi, i: (0, 0))],
        out_specs=[pl.BlockSpec(blk, lambda bi, i: (bi, i, 0)),
                   pl.BlockSpec(blk, lambda bi, i: (bi, nb - 1 - i, 0))],
        out_shape=[jax.ShapeDtypeStruct((b, l, A_WIDTH), BF16)] * 2,
        scratch_shapes=[pltpu.VMEM((A_HEADS, A_HEAD_DIM, A_HEAD_DIM), F32)] * 2,
        compiler_params=_params(("arbitrary", "arbitrary"), 32),
        name="hgrn2",
    )(hg, hg, hg, hg, hg, hg, lb)


def _s5_direction_terms(a_re, a_im, log_dt, b_re, b_im):
    a_re, a_im = a_re.astype(F32), a_im.astype(F32)
    b_re, b_im = b_re.astype(F32), b_im.astype(F32)
    dt = jnp.exp(log_dt.astype(F32))[:, None]
    j = jnp.arange(S5_CHUNK + 1, dtype=F32)[:, None, None]
    mag = jnp.exp(j * (dt * a_re))
    pw_re = mag * jnp.cos(j * (dt * a_im))
    pw_im = mag * jnp.sin(j * (dt * a_im))
    nr, ni = pw_re[1] - 1.0, pw_im[1]
    den = a_re * a_re + a_im * a_im
    f_re = (nr * a_re + ni * a_im) / den
    f_im = (ni * a_re - nr * a_im) / den
    bb_re = f_re[..., None] * b_re - f_im[..., None] * b_im
    bb_im = f_re[..., None] * b_im + f_im[..., None] * b_re
    pb_re = pw_re[..., None] * bb_re - pw_im[..., None] * bb_im
    pb_im = pw_re[..., None] * bb_im + pw_im[..., None] * bb_re
    return pw_re, pw_im, pb_re, pb_im


def _s5_assemble_body(strip_ref, inf_ref, inb_ref, outf_ref, outb_ref, w1_ref, w2_ref):
    c, half = S5_CHUNK, 2 * S5_STATE
    for g in range(strip_ref.shape[0]):
        strip = strip_ref[g]
        for s in range(c):
            lo = (c - 1 - s) * S5_GROUP
            w1_ref[g, s * S5_GROUP:(s + 1) * S5_GROUP, :S5_COLS] = strip[:, lo:lo + S5_COLS].astype(BF16)
        w1_ref[g, :, S5_COLS:S5_COLS + half] = inf_ref[g].astype(BF16)
        w1_ref[g, :, S5_COLS + half:] = inb_ref[g].astype(BF16)
        w2_ref[g, :half, :] = outf_ref[g].astype(BF16)
        w2_ref[g, half:, :] = outb_ref[g].astype(BF16)


def _s5_assemble(strip, w_in_f, w_in_b, w_out_f, w_out_b):
    g = strip.shape[0]
    gps = S5_GROUPS_PER_STEP
    per_g = lambda a: pl.BlockSpec((gps,) + a.shape[1:], lambda i: (i, 0, 0))
    return pl.pallas_call(
        _s5_assemble_body,
        grid=(g // gps,),
        in_specs=[per_g(strip), per_g(w_in_f), per_g(w_in_b), per_g(w_out_f), per_g(w_out_b)],
        out_specs=[pl.BlockSpec((gps, S5_COLS, 2 * S5_COLS), lambda i: (i, 0, 0)),
                   pl.BlockSpec((gps, S5_COLS, S5_COLS), lambda i: (i, 0, 0))],
        out_shape=[jax.ShapeDtypeStruct((g, S5_COLS, 2 * S5_COLS), BF16),
                   jax.ShapeDtypeStruct((g, S5_COLS, S5_COLS), BF16)],
        compiler_params=_params(("arbitrary",), 32),
        name="s5_assemble",
    )(strip, w_in_f, w_in_b, w_out_f, w_out_b)


def _s5_operators(a_re, a_im, log_dt, b_re, b_im, c_re, c_im, n_steps):
    hp = lax.Precision.HIGHEST
    c_re, c_im = c_re.astype(F32), c_im.astype(F32)
    g = S5_GROUPS
    terms = [_s5_direction_terms(a_re[d], a_im[d], log_dt[d], b_re, b_im) for d in range(2)]
    taps = []
    for pw_re, pw_im, pb_re, pb_im in terms:
        k = (jnp.einsum('gop,jgpi->jgoi', c_re, pb_re[:S5_CHUNK], precision=hp)
             - jnp.einsum('gop,jgpi->jgoi', c_im, pb_im[:S5_CHUNK], precision=hp))
        taps.append(k)
    c = S5_CHUNK
    strip = jnp.concatenate([taps[1][:0:-1], (taps[0][0] + taps[1][0])[None], taps[0][1:]], axis=0)
    strip = strip.transpose(1, 3, 0, 2).reshape(g, S5_GROUP, (2 * c - 1) * S5_GROUP)

    def w_in(pb_re, pb_im):
        re = pb_re.transpose(1, 0, 3, 2)
        im = pb_im.transpose(1, 0, 3, 2)
        return jnp.concatenate([re, im], axis=-1).reshape(g, S5_COLS, 2 * S5_STATE)

    def w_out(pr, pi):
        m_re = c_re[None] * pr[:, :, None, :] - c_im[None] * pi[:, :, None, :]
        m_im = c_re[None] * pi[:, :, None, :] + c_im[None] * pr[:, :, None, :]
        re = m_re.transpose(1, 3, 0, 2).reshape(g, S5_STATE, S5_COLS)
        im = -m_im.transpose(1, 3, 0, 2).reshape(g, S5_STATE, S5_COLS)
        return jnp.concatenate([re, im], axis=1)

    w_in_f = w_in(terms[0][2][c - 1::-1], terms[0][3][c - 1::-1])
    w_in_b = w_in(terms[1][2][:c], terms[1][3][:c])
    w_out_f = w_out(terms[0][0][1:], terms[0][1][1:])
    w_out_b = w_out(terms[1][0][:0:-1], terms[1][1][:0:-1])
    w1, w2 = _s5_assemble(strip, w_in_f, w_in_b, w_out_f, w_out_b)

    ars, ais = [], []
    cur = [(t[0][S5_CHUNK], t[1][S5_CHUNK]) for t in terms]
    for _ in range(n_steps):
        ars.append(jnp.concatenate([cur[0][0], cur[0][0], cur[1][0], cur[1][0]], axis=-1))
        ais.append(jnp.concatenate([-cur[0][1], cur[0][1], -cur[1][1], cur[1][1]], axis=-1))
        cur = [(r * r - i * i, 2.0 * r * i) for r, i in cur]
    return w1, w2, jnp.stack(ars, axis=1), jnp.stack(ais, axis=1)


def _s5_group(u, w1, w2, ar, ai, n, n_seq, n_steps):
    rows = u.shape[0]
    half = 2 * S5_STATE
    yv = jnp.dot(u, w1, preferred_element_type=F32)
    zf = jnp.where(n >= 1, pltpu.roll(yv[:, S5_COLS:S5_COLS + half], 1, axis=0), 0.0)
    zb = jnp.where(n < n_seq - 1, pltpu.roll(yv[:, S5_COLS + half:], rows - 1, axis=0), 0.0)
    for k in range(n_steps):
        s = 1 << k
        sf = jnp.where(n >= s, pltpu.roll(zf, s, axis=0), 0.0)
        sb = jnp.where(n < n_seq - s, pltpu.roll(zb, rows - s, axis=0), 0.0)
        zf = zf + ar[k:k + 1, :half] * sf + ai[k:k + 1, :half] * pltpu.roll(sf, S5_STATE, axis=1)
        zb = zb + ar[k:k + 1, half:] * sb + ai[k:k + 1, half:] * pltpu.roll(sb, S5_STATE, axis=1)
    x = jnp.concatenate([zf, zb], axis=1).astype(BF16)
    return yv[:, :S5_COLS] + jnp.dot(x, w2, preferred_element_type=F32)


def _s5_body(u_ref, perm_ref, w1_ref, w2_ref, ar_ref, ai_ref, y_ref, z_ref, *, n_seq, n_steps):
    gps = S5_GROUPS_PER_STEP
    rows = u_ref.shape[1]
    n = lax.broadcasted_iota(I32, (rows, 2 * S5_STATE), 0) & (n_seq - 1)
    perm = perm_ref[...]
    for hf in range(S5_CHUNK // gps):
        x = jnp.concatenate([u_ref[hf * gps + j] for j in range(gps)], axis=1)
        z_ref[:, hf * gps * LANES:(hf + 1) * gps * LANES] = jnp.dot(
            x, perm, preferred_element_type=F32).astype(BF16)
    for g in range(gps):
        u = jnp.concatenate([z_ref[:, (hf * gps + g) * LANES:(hf * gps + g + 1) * LANES]
                             for hf in range(S5_CHUNK // gps)], axis=1)
        y = _s5_group(u, w1_ref[g], w2_ref[g], ar_ref[g], ai_ref[g], n, n_seq, n_steps).astype(BF16)
        for hf in range(S5_CHUNK // gps):
            z_ref[:, (hf * gps + g) * LANES:(hf * gps + g + 1) * LANES] = y[:, hf * LANES:(hf + 1) * LANES]
    for hf in range(S5_CHUNK // gps):
        x = jnp.dot(z_ref[:, hf * gps * LANES:(hf + 1) * gps * LANES], perm,
                    preferred_element_type=F32).astype(BF16)
        for j in range(gps):
            y_ref[hf * gps + j] = x[:, j * LANES:(j + 1) * LANES]


def _s5_lane_permutation():
    idx = jnp.arange(S5_GROUPS_PER_STEP * LANES)
    a, b, c = idx // LANES, (idx % LANES) // S5_GROUP, idx % S5_GROUP
    dst = b * LANES + a * S5_GROUP + c
    return (dst[:, None] == idx[None, :]).astype(BF16)


def _s5(us, perm, w1, w2, ar, ai, n_seq):
    _, rows, _ = us.shape
    assert n_seq & (n_seq - 1) == 0, n_seq
    n_steps = n_seq.bit_length() - 1
    gps = S5_GROUPS_PER_STEP
    tok = pl.BlockSpec((S5_CHUNK, rows, LANES), lambda i: (0, 0, i))
    per_g = lambda a: pl.BlockSpec((gps,) + a.shape[1:], lambda i: (i, 0, 0))
    return pl.pallas_call(
        functools.partial(_s5_body, n_seq=n_seq, n_steps=n_steps),
        grid=(S5_GROUPS // gps,),
        in_specs=[tok, pl.BlockSpec(perm.shape, lambda i: (0, 0)), per_g(w1), per_g(w2), per_g(ar), per_g(ai)],
        out_specs=tok,
        out_shape=jax.ShapeDtypeStruct(us.shape, BF16),
        scratch_shapes=[pltpu.VMEM((rows, S5_CHUNK * LANES), BF16)],
        compiler_params=_params(("arbitrary",), 48),
        name="s5",
    )(us, perm, w1, w2, ar, ai)


def _gelu_tanh(x):
    return 0.5 * x * (1.0 + jnp.tanh(math.sqrt(2.0 / math.pi) * (x + 0.044715 * (x * x * x))))


def _merge_body(x_ref, of_ref, ob_ref, go_ref, ys_ref, u_ref, ga_ref, gb_ref,
                hn_ref, sd_ref, wglu_ref, bglu_ref, wa_ref, wb_ref, wo_ref, nf_ref, wr_ref,
                x1_ref, h2_ref, lg_ref, yn_ref, *, tm):
    for sg in range(S5_CHUNK):
        for cb in range(S5_WIDTH // LANES):
            yn_ref[cb, pl.ds(sg, tm // S5_CHUNK, stride=S5_CHUNK), :] = ys_ref[
                sg, :, cb * LANES:(cb + 1) * LANES].astype(F32)
    o = of_ref[...].astype(F32) + ob_ref[...].astype(F32)
    parts = []
    for h in range(A_HEADS):
        oh = o[:, h * A_HEAD_DIM:(h + 1) * A_HEAD_DIM]
        parts.append(oh * lax.rsqrt(jnp.mean(oh * oh, axis=-1, keepdims=True) + RMS_EPS))
    h_a = jnp.concatenate(parts, axis=1) * hn_ref[...] * jax.nn.silu(go_ref[...].astype(F32))

    u = u_ref[...].astype(F32)
    y_s5 = jnp.concatenate([yn_ref[cb] for cb in range(S5_WIDTH // LANES)], axis=1)
    ys = _gelu_tanh(y_s5 + sd_ref[...] * u)
    z = jnp.dot(ys.astype(BF16), wglu_ref[...], preferred_element_type=F32) + bglu_ref[...]
    h_b = ys * jax.nn.sigmoid(z)

    y_a = jnp.dot(h_a.astype(BF16), wa_ref[...], preferred_element_type=F32)
    y_b = jnp.dot(h_b.astype(BF16), wb_ref[...], preferred_element_type=F32)
    merged = jax.nn.sigmoid(ga_ref[...].astype(F32)) * y_a + jax.nn.sigmoid(gb_ref[...].astype(F32)) * y_b
    x1 = x_ref[...] + jnp.dot(merged.astype(BF16), wo_ref[...], preferred_element_type=F32)
    x1_ref[...] = x1

    h2 = (x1 * lax.rsqrt(jnp.mean(x1 * x1, axis=-1, keepdims=True) + RMS_EPS)) * nf_ref[...]
    h2_hi = h2.astype(BF16)
    h2_ref[...] = h2_hi
    h2_lo = (h2 - h2_hi.astype(F32)).astype(BF16)
    r_hi = jnp.dot(h2_hi, wr_ref[...], preferred_element_type=F32)
    r_lo = jnp.dot(h2_lo, wr_ref[...], preferred_element_type=F32)
    lg_ref[...] = r_hi[:, :N_EXPERTS] + r_hi[:, N_EXPERTS:2 * N_EXPERTS] + r_lo[:, :N_EXPERTS]


def _split_router(w_router):
    w = w_router.astype(F32)
    hi = w.astype(BF16)
    lo = (w - hi.astype(F32)).astype(BF16)
    pad = jnp.zeros((w.shape[0], LANES - 2 * N_EXPERTS), BF16)
    return jnp.concatenate([hi, lo, pad], axis=1)


def _merge(x, o_fw, o_bw, hg, ys5, u, gates, hn, sd, wglu, bglu, wa, wb, wo, nf, wr_pad, tm=512):
    t = x.shape[0]
    row = lambda w, col=0: pl.BlockSpec((tm, w), lambda i: (i, col))
    full = lambda a: pl.BlockSpec(a.shape, lambda i: (0,) * a.ndim)
    return pl.pallas_call(
        functools.partial(_merge_body, tm=tm),
        grid=(t // tm,),
        in_specs=[row(D_MODEL), row(A_WIDTH), row(A_WIDTH), row(A_WIDTH, 4),
                  pl.BlockSpec((S5_CHUNK, tm // S5_CHUNK, S5_WIDTH), lambda i: (0, i, 0)), row(S5_WIDTH),
                  row(D_MODEL, 0), row(D_MODEL, 1),
                  full(hn), full(sd), full(wglu), full(bglu), full(wa), full(wb), full(wo), full(nf), full(wr_pad)],
        out_specs=[row(D_MODEL), row(D_MODEL), row(N_EXPERTS)],
        out_shape=[jax.ShapeDtypeStruct((t, D_MODEL), F32),
                   jax.ShapeDtypeStruct((t, D_MODEL), BF16),
                   jax.ShapeDtypeStruct((t, N_EXPERTS), F32)],
        scratch_shapes=[pltpu.VMEM((S5_WIDTH // LANES, tm, LANES), F32)],
        compiler_params=_params(("arbitrary",), 48),
        name="merge",
    )(x, o_fw, o_bw, hg, ys5, u, gates, gates, hn, sd, wglu, bglu, wa, wb, wo, nf, wr_pad)


def _exclusive_prefix(m, upper, strict_lower):
    mb = m.astype(BF16)
    incl = jnp.dot(mb, upper, preferred_element_type=F32)
    row_off = jnp.sum(jnp.dot(strict_lower, mb, preferred_element_type=F32), axis=1, keepdims=True)
    return incl - m + row_off, row_off


def _select_body(lg_ref, pos_ref, gate_ref, lo_ref, aff_ref, *, rpe, cap):
    e_n = N_EXPERTS
    slab = lambda ref, e: ref[e * rpe:(e + 1) * rpe, :]
    mx = slab(lg_ref, 0)
    for e in range(1, e_n):
        mx = jnp.maximum(mx, slab(lg_ref, e))
    den = jnp.zeros_like(mx)
    for e in range(e_n):
        ex = jnp.exp(slab(lg_ref, e) - mx)
        aff_ref[e * rpe:(e + 1) * rpe, :] = ex
        den = den + ex
    for e in range(e_n):
        aff_ref[e * rpe:(e + 1) * rpe, :] = slab(aff_ref, e) / den

    capf = float(cap)

    def bit_step(i, cand):
        bit = jnp.left_shift(jnp.int32(1), 30 - i)
        rows = []
        for e in range(e_n):
            cur = cand[e:e + 1, :]
            trial = cur | bit
            cnt = jnp.sum(jnp.where(slab(aff_ref, e) >= pltpu.bitcast(trial, F32), 1.0, 0.0), keepdims=True)
            rows.append(jnp.where(cnt >= capf, trial, cur))
        return jnp.concatenate(rows, axis=0)

    thr = pltpu.bitcast(lax.fori_loop(0, 31, bit_step, jnp.zeros((e_n, LANES), I32)), F32)

    upper = jnp.where(lax.broadcasted_iota(I32, (LANES, LANES), 0) <= lax.broadcasted_iota(I32, (LANES, LANES), 1),
                      1.0, 0.0).astype(BF16)
    strict_lower = jnp.where(lax.broadcasted_iota(I32, (rpe, rpe), 1) < lax.broadcasted_iota(I32, (rpe, rpe), 0),
                             1.0, 0.0).astype(BF16)
    for e in range(e_n):
        aff = slab(aff_ref, e)
        t = thr[e:e + 1, :]
        gt = jnp.where(aff > t, 1.0, 0.0)
        eq = jnp.where(aff == t, 1.0, 0.0)
        need = capf - jnp.sum(gt, keepdims=True)
        rank_eq, _ = _exclusive_prefix(eq, upper, strict_lower)
        sel = gt + eq * jnp.where(rank_eq < need, 1.0, 0.0)
        pos, row_off = _exclusive_prefix(sel, upper, strict_lower)
        chosen = sel > 0.5
        pos_ref[e * rpe:(e + 1) * rpe, :] = jnp.where(chosen, pos, -1.0).astype(I32)
        gate_ref[e * rpe:(e + 1) * rpe, :] = jnp.where(chosen, aff, 0.0)
        lo_ref[e * rpe:(e + 1) * rpe, :] = jnp.broadcast_to(row_off, (rpe, LANES)).astype(I32)


def _select(lg_em, rpe, cap):
    rows = N_EXPERTS * rpe
    shp = lambda dt: jax.ShapeDtypeStruct((rows, LANES), dt)
    return pl.pallas_call(
        functools.partial(_select_body, rpe=rpe, cap=cap),
        out_shape=[shp(I32), shp(F32), shp(I32)],
        scratch_shapes=[pltpu.VMEM((rows, LANES), F32)],
        compiler_params=pltpu.CompilerParams(vmem_limit_bytes=32 * MIB),
        name="select",
    )(lg_em)


def _block_lo(lo_ref, e, blk, rpe):
    return lo_ref[e * rpe + blk * (TOK_BLOCK // LANES)]


def _block_count(lo_ref, e, blk, rpe, n_blk, cap):
    nxt = lo_ref[e * rpe + jnp.minimum((blk + 1) * (TOK_BLOCK // LANES), rpe - 1)]
    return jnp.where(blk == n_blk - 1, cap, nxt) - _block_lo(lo_ref, e, blk, rpe)


def _floor_tile(row):
    return row - (row & (BF16_TILE_ROWS - 1))


def _dispatch_body(lo_ref, pm_ref, h_ref, xe_ref, stage_ref, carry_ref, sem_ref, *, rpe, n_blk, cap):
    b = pl.program_id(0)

    @pl.when(b == 0)
    def _():
        carry_ref[...] = jnp.zeros_like(carry_ref)

    pm = pm_ref[...]
    hb = h_ref[...]
    r_idx = lax.broadcasted_iota(I32, (DISPATCH_WIN, TOK_BLOCK), 0)
    head_row = lax.broadcasted_iota(I32, (BF16_TILE_ROWS, D_MODEL), 0)
    los = [_block_lo(lo_ref, e, b, rpe) for e in range(N_EXPERTS)]
    bases = [_floor_tile(lo) for lo in los]
    onehot = jnp.concatenate(
        [jnp.where(r_idx == pm[e:e + 1, :] - bases[e], 1.0, 0.0) for e in range(N_EXPERTS)], axis=0).astype(BF16)
    xw = jnp.dot(onehot, hb, preferred_element_type=F32)

    def window_copy(e, first_row):
        start = pl.multiple_of(first_row, BF16_TILE_ROWS)
        return pltpu.make_async_copy(stage_ref.at[e], xe_ref.at[e, pl.ds(start, DISPATCH_WIN), :], sem_ref.at[e])

    for e in range(N_EXPERTS):
        @pl.when(b > 0)
        def _():
            window_copy(e, 0).wait()

        rows = xw[e * DISPATCH_WIN:(e + 1) * DISPATCH_WIN, :]
        head = jnp.where(head_row < los[e] - bases[e], carry_ref[e].astype(F32), rows[:BF16_TILE_ROWS, :])
        stage_ref[e, :BF16_TILE_ROWS, :] = head.astype(BF16)
        stage_ref[e, BF16_TILE_ROWS:, :] = rows[BF16_TILE_ROWS:, :].astype(BF16)
        window_copy(e, bases[e]).start(priority=e % 2)

    ends =[los[e] + _block_count(lo_ref, e, b, rpe, n_blk, cap) for e in range(N_EXPERTS)]
    n_wins = [lax.div(ends[e] - bases[e] + (DISPATCH_WIN - 1), DISPATCH_WIN) for e in range(N_EXPERTS)]

    @pl.when(functools.reduce(jnp.maximum, n_wins) > 1)
    def _():
        for e in range(N_EXPERTS):
            def extra_window(k, carry):
                window_copy(e, 0).wait()
                oh = jnp.where(r_idx + k * DISPATCH_WIN == pm[e:e + 1, :] - bases[e], 1.0, 0.0).astype(BF16)
                stage_ref[e] = jnp.dot(oh, hb, preferred_element_type=F32).astype(BF16)
                window_copy(e, bases[e] + k * DISPATCH_WIN).start()
                return carry

            lax.fori_loop(1, n_wins[e], extra_window, 0)

    for e in range(N_EXPERTS):
        last_start = bases[e] + (jnp.maximum(n_wins[e], 1) - 1) * DISPATCH_WIN
        tile = jnp.minimum(_floor_tile(ends[e]) - last_start, DISPATCH_WIN - BF16_TILE_ROWS)
        carry_ref[e] = stage_ref[e, pl.ds(pl.multiple_of(tile, BF16_TILE_ROWS), BF16_TILE_ROWS), :]

    @pl.when(b == n_blk - 1)
    def _():
        for e in range(N_EXPERTS):
            window_copy(e, 0).wait()
            stage_ref[e] = jnp.zeros((DISPATCH_WIN, D_MODEL), BF16)
            window_copy(e, cap).start()
        for e in range(N_EXPERTS):
            window_copy(e, cap).wait()


def _dispatch(lo, posm, h2, cap):
    t = h2.shape[0]
    rpe = t // LANES
    n_blk = t // TOK_BLOCK
    grid_spec = pltpu.PrefetchScalarGridSpec(
        num_scalar_prefetch=1,
        grid=(n_blk,),
        in_specs=[pl.BlockSpec((N_EXPERTS, TOK_BLOCK), lambda b, lo_r: (0, b)),
                  pl.BlockSpec((TOK_BLOCK, D_MODEL), lambda b, lo_r: (b, 0))],
        out_specs=pl.BlockSpec(memory_space=pl.ANY),
        scratch_shapes=[pltpu.VMEM((N_EXPERTS, DISPATCH_WIN, D_MODEL), BF16),
                        pltpu.VMEM((N_EXPERTS, BF16_TILE_ROWS, D_MODEL), BF16),
                        pltpu.SemaphoreType.DMA((N_EXPERTS,))],
    )
    return pl.pallas_call(
        functools.partial(_dispatch_body, rpe=rpe, n_blk=n_blk, cap=cap),
        grid_spec=grid_spec,
        out_shape=jax.ShapeDtypeStruct((N_EXPERTS, cap + DISPATCH_WIN, D_MODEL), BF16),
        compiler_params=_params(("arbitrary",), 32),
        name="dispatch",
    )(lo, posm, h2)


def _ffn_body(x_ref, wg_ref, wu_ref, wd_ref, y_ref, acc_ref, *, n_f, n_sub, sub):
    f = pl.program_id(1)
    wg = wg_ref[0].astype(BF16)
    wu = wu_ref[0].astype(BF16)
    wd = wd_ref[0].astype(BF16)

    def sweep(first, last):
        def rows_step(i, carry):
            r = pl.ds(pl.multiple_of(i * sub, sub), sub)
            x = x_ref[0, r, :]
            a = jnp.dot(x, wg, preferred_element_type=F32)
            u = jnp.dot(x, wu, preferred_element_type=F32)
            h = (jax.nn.silu(a) * u).astype(BF16)
            d = jnp.dot(h, wd, preferred_element_type=F32)
            if not first:
                d = acc_ref[r, :] + d
            if last:
                y_ref[0, r, :] = d.astype(BF16)
            else:
                acc_ref[r, :] = d
            return carry

        lax.fori_loop(0, n_sub, rows_step, 0, unroll=True)

    if n_f == 1:
        sweep(True, True)
    else:
        pl.when(f == 0)(lambda: sweep(True, False))
        pl.when(jnp.logical_and(f > 0, f < n_f - 1))(lambda: sweep(False, False))
        pl.when(f == n_f - 1)(lambda: sweep(False, True))


def _ffn(xe, w_gate, w_up, w_down, cap, tf=512, sub=1024):
    sub = min(sub, cap)
    n_f = EXPERT_FF // tf
    return pl.pallas_call(
        functools.partial(_ffn_body, n_f=n_f, n_sub=cap // sub, sub=sub),
        grid=(N_EXPERTS, n_f),
        in_specs=[pl.BlockSpec((1, cap, D_MODEL), lambda e, f: (e, 0, 0)),
                  pl.BlockSpec((1, D_MODEL, tf), lambda e, f: (e, 0, f)),
                  pl.BlockSpec((1, D_MODEL, tf), lambda e, f: (e, 0, f)),
                  pl.BlockSpec((1, tf, D_MODEL), lambda e, f: (e, f, 0))],
        out_specs=pl.BlockSpec((1, cap, D_MODEL), lambda e, f: (e, 0, 0)),
        out_shape=jax.ShapeDtypeStruct((N_EXPERTS, cap, D_MODEL), BF16),
        scratch_shapes=[pltpu.VMEM((cap, D_MODEL), F32)],
        compiler_params=_params(("arbitrary", "arbitrary"), 56),
        name="expert_ffn",
    )(xe, w_gate, w_up, w_down)


def _combine_body(lo_ref, pm_ref, gm_ref, x1_ref, nf_ref, y_ref, out_ref,
                  a_ref, m_ref, ovw_ref, sem_ref, ovsem_ref, *, rpe, n_blk, cap):
    WIN = COMBINE_WIN
    b = pl.program_id(0)
    slot = b % 2

    def win_start(e, blk):
        return jnp.minimum(_floor_tile(_block_lo(lo_ref, e, blk, rpe)), cap - WIN)

    def window_copy(e, blk, sl):
        start = pl.multiple_of(win_start(e, blk), BF16_TILE_ROWS)
        return pltpu.make_async_copy(y_ref.at[e, pl.ds(start, WIN), :], a_ref.at[sl, pl.ds(e * WIN, WIN), :],
                                     sem_ref.at[sl, e])

    @pl.when(b == 0)
    def _():
        for e in range(N_EXPERTS):
            window_copy(e, 0, 0).start(priority=e % 2)

    @pl.when(b + 1 < n_blk)
    def _():
        for e in range(N_EXPERTS):
            window_copy(e, b + 1, 1 - slot).start(priority=e % 2)

    pm = pm_ref[...]
    gm = gm_ref[...]
    ws0 = [win_start(e, b) for e in range(N_EXPERTS)]
    e_col = lax.broadcasted_iota(I32, (1, N_EXPERTS), 1)
    ws_row = jnp.zeros((1, N_EXPERTS), I32)
    for e in range(N_EXPERTS):
        ws_row = jnp.where(e_col == e, ws0[e], ws_row)
    rel = jnp.clip(pm - ws_row, -1, WIN).astype(F32).astype(BF16)
    spread = jnp.where(lax.broadcasted_iota(I32, (N_EXPERTS, N_EXPERTS * WIN), 1) // WIN
                       == lax.broadcasted_iota(I32, (N_EXPERTS, N_EXPERTS * WIN), 0), 1.0, 0.0).astype(BF16)
    tgt = jnp.dot(rel, spread, preferred_element_type=F32)
    gv = jnp.dot(gm.astype(BF16), spread, preferred_element_type=F32)
    lane_r = (lax.broadcasted_iota(I32, (TOK_BLOCK, N_EXPERTS * WIN), 1) & (WIN - 1)).astype(F32)
    gmat = jnp.where(tgt == lane_r, gv, 0.0).astype(BF16)

    for e in range(N_EXPERTS):
        window_copy(e, b, slot).wait()
    mixed = jnp.dot(gmat, a_ref[slot], preferred_element_type=F32)

    def finish(m):
        xo = x1_ref[...] + m
        out_ref[...] = (xo * lax.rsqrt(jnp.mean(xo * xo, axis=-1, keepdims=True) + RMS_EPS)) * nf_ref[...]

    finish(mixed)
    ends = [_block_lo(lo_ref, e, b, rpe) + _block_count(lo_ref, e, b, rpe, n_blk, cap) for e in range(N_EXPERTS)]
    n_wins = [lax.div(jnp.maximum(ends[e] - ws0[e], 0) + (WIN - 1), WIN) for e in range(N_EXPERTS)]

    @pl.when(functools.reduce(jnp.maximum, n_wins) > 1)
    def _():
        m_ref[...] = mixed
        lane_w = lax.broadcasted_iota(I32, (TOK_BLOCK, WIN), 1)
        for e in range(N_EXPERTS):
            def extra_window(k, carry):
                first = ws0[e] + k * WIN
                ws = jnp.minimum(first, cap - WIN)
                cp = pltpu.make_async_copy(y_ref.at[e, pl.ds(pl.multiple_of(ws, BF16_TILE_ROWS), WIN), :], ovw_ref,
                                           ovsem_ref)
                cp.start()
                cp.wait()
                pe = pm[:, e:e + 1]
                gk = jnp.where((lane_w == pe - ws) & (pe >= first), gm[:, e:e + 1], 0.0).astype(BF16)
                m_ref[...] += jnp.dot(gk, ovw_ref[...], preferred_element_type=F32)
                return carry

            lax.fori_loop(1, n_wins[e], extra_window, 0)
        finish(m_ref[...])


def _combine(lo, posm_t, gate_t, x1, norm_final, y2, cap):
    t = x1.shape[0]
    rpe = t // LANES
    n_blk = t // TOK_BLOCK
    grid_spec = pltpu.PrefetchScalarGridSpec(
        num_scalar_prefetch=1,
        grid=(n_blk,),
        in_specs=[pl.BlockSpec((TOK_BLOCK, N_EXPERTS), lambda b, lo_r: (b, 0)),
                  pl.BlockSpec((TOK_BLOCK, N_EXPERTS), lambda b, lo_r: (b, 0)),
                  pl.BlockSpec((TOK_BLOCK, D_MODEL), lambda b, lo_r: (b, 0)),
                  pl.BlockSpec((1, D_MODEL), lambda b, lo_r: (0, 0)),
                  pl.BlockSpec(memory_space=pl.ANY)],
        out_specs=pl.BlockSpec((TOK_BLOCK, D_MODEL), lambda b, lo_r: (b, 0)),
        scratch_shapes=[pltpu.VMEM((2, N_EXPERTS * COMBINE_WIN, D_MODEL), BF16),
                        pltpu.VMEM((TOK_BLOCK, D_MODEL), F32),
                        pltpu.VMEM((COMBINE_WIN, D_MODEL), BF16),
                        pltpu.SemaphoreType.DMA((2, N_EXPERTS)),
                        pltpu.SemaphoreType.DMA(())],
    )
    return pl.pallas_call(
        functools.partial(_combine_body, rpe=rpe, n_blk=n_blk, cap=cap),
        grid_spec=grid_spec,
        out_shape=jax.ShapeDtypeStruct((t, D_MODEL), F32),
        compiler_params=_params(("arbitrary",), 40),
        name="combine",
    )(lo, posm_t, gate_t, x1, norm_final, y2)


def _trunk(x, p):
    b, l, _ = x.shape
    t = b * l
    n_seq = l // S5_CHUNK
    cap = max(1, EC_CAPACITY_FACTOR * t // N_EXPERTS)
    rpe = t // LANES

    hg, u, u_sm, gates = _inproj(x.reshape(t, D_MODEL), p['norm_mix'], p['w_in'])
    o_fw, o_bw = _hgrn(hg.reshape(b, l, 5 * A_WIDTH), p['lb'])
    ys5 = _s5(u_sm, p['s5_perm'], *p['s5'], n_seq)

    x1, h2, logits = _merge(x.reshape(t, D_MODEL), o_fw.reshape(t, A_WIDTH), o_bw.reshape(t, A_WIDTH), hg, ys5, u,
                            gates, p['hgrn_norm'], p['s5_d'], p['w_glu'], p['b_glu'], p['w_a'], p['w_b'],
                            p['w_out'], p['norm_ffn'], p['w_router'])

    posm, gate, lo = _select(logits.T.reshape(N_EXPERTS * rpe, LANES), rpe, cap)
    lo = lo[:, 0]
    posm = posm.reshape(N_EXPERTS, t)
    xe = _dispatch(lo, posm, h2, cap)
    y2 = _ffn(xe, p['w_gate'], p['w_up'], p['w_down'], cap)
    out = _combine(lo, posm.T, gate.reshape(N_EXPERTS, t).T, x1, p['norm_final'], y2, cap)
    return out.reshape(b, l, D_MODEL)


def kernel(x_prompt, x_sample, norm_mix, w_in, hgrn_gamma, hgrn_norm, s5_a_re, s5_a_im, s5_log_dt, s5_b_re, s5_b_im, s5_c_re, s5_c_im, s5_d, s5_w_glu, s5_b_glu, w_branch_a, w_branch_b, w_out, norm_ffn, w_router, w_exp_gate, w_exp_up, w_exp_down, norm_final):
    assert norm_mix.shape[0] == 1, "single-layer trunk"
    row = lambda a: a.astype(F32).reshape(1, -1)
    lb = jnp.cumsum(jax.nn.softmax(hgrn_gamma.astype(F32), axis=0), axis=0)[0]
    max_chunks = max(x_prompt.shape[1], x_sample.shape[1]) // S5_CHUNK
    s5_ops = _s5_operators(s5_a_re[0], s5_a_im[0], s5_log_dt[0], s5_b_re[0], s5_b_im[0], s5_c_re[0], s5_c_im[0],
                           max_chunks.bit_length() - 1)
    p = {
        'norm_mix': row(norm_mix[0]), 'w_in': w_in[0].astype(BF16), 'lb': lb, 's5': s5_ops,
        's5_perm': _s5_lane_permutation(),
        'hgrn_norm': row(hgrn_norm[0]), 's5_d': row(s5_d[0]), 'w_glu': s5_w_glu[0].astype(BF16),
        'b_glu': row(s5_b_glu[0]), 'w_a': w_branch_a[0].astype(BF16), 'w_b': w_branch_b[0].astype(BF16),
        'w_out': w_out[0].astype(BF16), 'norm_ffn': row(norm_ffn[0]),
        'w_router': _split_router(w_router[0]),
        'w_gate': w_exp_gate[0], 'w_up': w_exp_up[0], 'w_down': w_exp_down[0],
        'norm_final': row(norm_final),
    }
    return (_trunk(x_prompt, p), _trunk(x_sample, p))
```
